```python
import functools
import jax, jax.numpy as jnp
from jax import lax
import numpy as np

D_MODEL = 2048
BATCH = 1
SEQ = 8192
DEPTH = 4
DEC_BATCH = 32
DEC_SEQ = 16
PAST_LEN = 1024

CHUNK = 64
HEAD_DIM = 64
A_WIDTH = 1024
A_HEADS = A_WIDTH // HEAD_DIM
DECAY_LORA = 64
ICLR_LORA = 64
GATE_LORA = 160
GN_EPS = 64e-5
B_HEADS = 16
B_KV_HEADS = 4
B_GROUP = B_HEADS // B_KV_HEADS
WINDOW = 128
WINDOW_CHUNKS = WINDOW // CHUNK
BAND = (WINDOW_CHUNKS + 1) * CHUNK
ROPE_THETA = 10000.0
D_FF = 5504
RMS_EPS = 1e-6

A_COLS = 3 * A_WIDTH + DECAY_LORA + ICLR_LORA + GATE_LORA
B_Q = B_HEADS * HEAD_DIM
B_KV = B_KV_HEADS * HEAD_DIM
B_COLS = B_Q + 2 * B_KV
IN_COLS = A_COLS + B_COLS + 2 * D_MODEL
A_SPLITS = (A_WIDTH, A_WIDTH + DECAY_LORA, 2 * A_WIDTH + DECAY_LORA, 3 * A_WIDTH + DECAY_LORA,
            3 * A_WIDTH + DECAY_LORA + ICLR_LORA)
IN_SPLITS = (A_COLS, A_COLS + B_Q, A_COLS + B_Q + B_KV, A_COLS + B_COLS, A_COLS + B_COLS + D_MODEL)

kernel_name = 'streaming_rwkv7_swa_sink_hybrid_step'


def rmsnorm(x, g):
    xf = x.astype(jnp.float32)
    y = xf * lax.rsqrt(jnp.mean(xf * xf, axis=-1, keepdims=True) + RMS_EPS)
    return (y * g.astype(jnp.float32)).astype(x.dtype)


def swiglu(x, w_gate, w_up, w_down):
    return (jax.nn.silu(x @ w_gate) * (x @ w_up)) @ w_down


def rope(x, pos):
    half = HEAD_DIM // 2
    inv = ROPE_THETA ** (-jnp.arange(half, dtype=jnp.float32) / half)
    ang = pos.astype(jnp.float32)[:, None] * inv[None, :]
    cos = jnp.cos(ang)[None, :, None, :]
    sin = jnp.sin(ang)[None, :, None, :]
    xf = x.astype(jnp.float32)
    x1, x2 = xf[..., :half], xf[..., half:]
    return jnp.concatenate([x1 * cos - x2 * sin, x2 * cos + x1 * sin], axis=-1).astype(x.dtype)


def rwkv7_time_mix(pa, shift_prev, wkv0, shift_mu, decay_w0, decay_w2, iclr_a0, iclr_a2, gate_g2,
                   k_k, k_a, r_k, gn_gain, gn_bias):
    f32 = jnp.float32
    B, T, _ = pa.shape
    prev = jnp.concatenate([shift_prev[:, None, :].astype(pa.dtype), pa[:, :-1]], axis=1)
    xm = pa + (prev - pa) * shift_mu
    r, w_lo, k, v, a_lo, g_lo = jnp.split(xm, A_SPLITS, axis=-1)
    w = -jax.nn.softplus(-(decay_w0 + jnp.tanh(w_lo) @ decay_w2).astype(f32)) - 0.5
    decay = jnp.exp(-jnp.exp(w))
    a = jax.nn.sigmoid((iclr_a0 + a_lo @ iclr_a2).astype(f32))
    g = jax.nn.sigmoid(g_lo) @ gate_g2
    heads = lambda t: t.reshape(B, T, A_HEADS, HEAD_DIM).astype(f32)
    kk = heads(k * k_k)
    kk = kk / jnp.maximum(jnp.sqrt(jnp.sum(kk * kk, axis=-1, keepdims=True)), 1e-12)
    k = k * (1.0 + (a - 1.0) * k_a)
    r_h, d_h, k_h, v_h, a_h = heads(r), heads(decay), heads(k), heads(v), heads(a)
    xs = tuple(jnp.swapaxes(t, 0, 1) for t in (r_h, d_h, k_h, v_h, kk, kk * a_h))

    def step(S, inp):
        r_t, d_t, k_t, v_t, kk_t, b_t = inp
        sa = jnp.einsum('bhij,bhj->bhi', S, kk_t)
        S = (S * d_t[:, :, None, :] - sa[..., None] * b_t[:, :, None, :]
             + v_t[..., None] * k_t[:, :, None, :])
        return S, jnp.einsum('bhij,bhj->bhi', S, r_t)

    S_T, ys = lax.scan(step, wkv0.astype(f32), xs)
    y = jnp.swapaxes(ys, 0, 1)
    mean = jnp.mean(y, axis=-1, keepdims=True)
    var = jnp.mean(jnp.square(y - mean), axis=-1, keepdims=True)
    y = ((y - mean) * lax.rsqrt(var + GN_EPS)).reshape(B, T, A_WIDTH) * gn_gain + gn_bias
    bonus = jnp.sum(r_h * k_h * r_k, axis=-1, keepdims=True) * v_h
    y = (y + bonus.reshape(B, T, A_WIDTH)) * g
    return y.astype(pa.dtype), S_T, pa[:, -1]


def _sink_softmax(s, sinks):
    sk = jnp.broadcast_to(sinks.reshape(B_KV_HEADS, B_GROUP, 1, 1).astype(jnp.float32),
                          s.shape[:-1] + (1,))
    return jax.nn.softmax(jnp.concatenate([s, sk], axis=-1), axis=-1)[..., :-1]


def band_attention(q, k, v, sinks):
    B, T = q.shape[:2]
    nc = T // CHUNK
    qc = q.reshape(B, nc, CHUNK, B_KV_HEADS, B_GROUP, HEAD_DIM)

    def band(t):
        tc = t.reshape(B, nc, CHUNK, B_KV_HEADS, HEAD_DIM)
        tp = jnp.pad(tc, ((0, 0), (WINDOW_CHUNKS, 0), (0, 0), (0, 0), (0, 0)))
        return jnp.concatenate([tp[:, j:j + nc] for j in range(WINDOW_CHUNKS + 1)], axis=2)

    kb, vb = band(k), band(v)
    valid = (jnp.arange(nc)[:, None] - WINDOW_CHUNKS + jnp.arange(BAND)[None, :] // CHUNK) >= 0
    s = jnp.einsum('bnqkgd,bnskd->bnkgqs', qc, kb).astype(jnp.float32) * (HEAD_DIM ** -0.5)
    s = jnp.where(valid[None, :, None, None, None, :], s, -1e30)
    p = _sink_softmax(s, sinks)
    o = jnp.einsum('bnkgqs,bnskd->bnqkgd', p.astype(v.dtype), vb)
    return o.reshape(B, T, B_Q)


def cached_attention(q, k, v, sinks, ck, cv):
    B, T = q.shape[:2]
    kf = jnp.concatenate([ck.astype(k.dtype), k], axis=1)
    vf = jnp.concatenate([cv.astype(v.dtype), v], axis=1)
    qg = q.reshape(B, T, B_KV_HEADS, B_GROUP, HEAD_DIM)
    s = jnp.einsum('btkgd,bskd->bkgts', qg, kf).astype(jnp.float32) * (HEAD_DIM ** -0.5)
    p = _sink_softmax(s, sinks)
    o = jnp.einsum('bkgts,bskd->btkgd', p.astype(vf.dtype), vf)
    return o.reshape(B, T, B_Q)


def trunk_layer(x, pos, shift_prev, wkv0, attend, lw):
    B, T, _ = x.shape
    x = x + 0.5 * swiglu(rmsnorm(x, lw['norm_ff1']), lw['ff1_gate'], lw['ff1_up'], lw['ff1_down'])
    h = rmsnorm(x, lw['norm_mix'])
    p = h @ lw['w_in']
    pa, pq, pk, pv, pga, pgb = jnp.split(p, IN_SPLITS, axis=-1)
    ya, wkv, shift = rwkv7_time_mix(pa, shift_prev, wkv0, lw['shift_mu'], lw['decay_w0'], lw['decay_w2'],
                                    lw['iclr_a0'], lw['iclr_a2'], lw['gate_g2'], lw['k_k'], lw['k_a'],
                                    lw['r_k'], lw['gn_gain'], lw['gn_bias'])
    q = rope(rmsnorm(pq.reshape(B, T, B_HEADS, HEAD_DIM), lw['q_norm']), pos)
    k = rope(rmsnorm(pk.reshape(B, T, B_KV_HEADS, HEAD_DIM), lw['k_norm']), pos)
    v = pv.reshape(B, T, B_KV_HEADS, HEAD_DIM)
    yb = attend(q, k, v, lw['sinks'])
    merged = jax.nn.sigmoid(pga) * (ya @ lw['w_up_a']) + jax.nn.sigmoid(pgb) * (yb @ lw['w_up_b'])
    x = x + merged @ lw['w_o']
    x = x + 0.5 * swiglu(rmsnorm(x, lw['norm_ff2']), lw['ff2_gate'], lw['ff2_up'], lw['ff2_down'])
    return x, wkv, shift, k, v


def setup_inputs(seed: int = 0) -> dict:
    key = jax.random.key(seed)
    ks = jax.random.split(key, 40)
    f32 = jnp.float32
    L, D, F = DEPTH, D_MODEL, D_FF
    rows = min(WINDOW, PAST_LEN)

    def nrm(i, shape, scale=1.0):
        return jax.random.normal(ks[i], shape, f32) * scale

    def uni(i, shape, lo, hi):
        return jax.random.uniform(ks[i], shape, f32, lo, hi)

    return {
        'x_prompt': nrm(0, (BATCH, SEQ, D)),
        'x_sample': nrm(1, (DEC_BATCH, DEC_SEQ, D)),
        'cache_k': nrm(2, (L, DEC_BATCH, rows, B_KV_HEADS, HEAD_DIM)),
        'cache_v': nrm(3, (L, DEC_BATCH, rows, B_KV_HEADS, HEAD_DIM)),
        'state_wkv': nrm(4, (L, DEC_BATCH, A_HEADS, HEAD_DIM, HEAD_DIM), 0.5),
        'state_shift': nrm(5, (L, DEC_BATCH, A_COLS)),
        'norm_ff1': 1.0 + nrm(6, (L, D), 0.05),
        'ff1_gate': nrm(7, (L, D, F), D ** -0.5),
        'ff1_up': nrm(8, (L, D, F), D ** -0.5),
        'ff1_down': nrm(9, (L, F, D), F ** -0.5),
        'norm_mix': 1.0 + nrm(10, (L, D), 0.05),
        'w_in': nrm(11, (L, D, IN_COLS), D ** -0.5),
        'shift_mu': uni(12, (L, A_COLS), 0.0, 1.0),
        'decay_w0': uni(13, (L, A_WIDTH), -6.0, -1.0),
        'decay_w2': nrm(14, (L, DECAY_LORA, A_WIDTH), 0.5 * DECAY_LORA ** -0.5),
        'iclr_a0': nrm(15, (L, A_WIDTH), 0.1),
        'iclr_a2': nrm(16, (L, ICLR_LORA, A_WIDTH), 0.5 * ICLR_LORA ** -0.5),
        'gate_g2': nrm(17, (L, GATE_LORA, A_WIDTH), GATE_LORA ** -0.5),
        'k_k': 0.85 + nrm(18, (L, A_WIDTH), 0.1),
        'k_a': 1.0 + nrm(19, (L, A_WIDTH), 0.1),
        'r_k': nrm(20, (L, A_HEADS, HEAD_DIM), 0.1),
        'gn_gain': 1.0 + nrm(21, (L, A_WIDTH), 0.05),
        'gn_bias': nrm(22, (L, A_WIDTH), 0.02),
        'q_norm': 1.0 + nrm(23, (L, HEAD_DIM), 0.05),
        'k_norm': 1.0 + nrm(24, (L, HEAD_DIM), 0.05),
        'sinks': nrm(25, (L, B_HEADS), 0.5),
        'w_up_a': nrm(26, (L, A_WIDTH, D), A_WIDTH ** -0.5),
        'w_up_b': nrm(27, (L, B_Q, D), B_Q ** -0.5),
        'w_o': nrm(28, (L, D, D), D ** -0.5),
        'norm_ff2': 1.0 + nrm(29, (L, D), 0.05),
        'ff2_gate': nrm(30, (L, D, F), D ** -0.5),
        'ff2_up': nrm(31, (L, D, F), D ** -0.5),
        'ff2_down': nrm(32, (L, F, D), F ** -0.5),
    }


def reference(x_prompt, x_sample, cache_k, cache_v, state_wkv, state_shift,
              norm_ff1, ff1_gate, ff1_up, ff1_down, norm_mix, w_in, shift_mu, decay_w0, decay_w2,
              iclr_a0, iclr_a2, gate_g2, k_k, k_a, r_k, gn_gain, gn_bias, q_norm, k_norm, sinks,
              w_up_a, w_up_b, w_o, norm_ff2, ff2_gate, ff2_up, ff2_down):
    bp, tp = x_prompt.shape[:2]
    ts = x_sample.shape[1]
    pos_p = jnp.arange(tp)
    pos_s = PAST_LEN + jnp.arange(ts)
    xp, xs = x_prompt, x_sample
    p_wkv, p_shift, p_k, p_v = [], [], [], []
    s_wkv, s_shift, s_k, s_v = [], [], [], []
    for l in range(DEPTH):
        lw = dict(norm_ff1=norm_ff1[l], ff1_gate=ff1_gate[l], ff1_up=ff1_up[l], ff1_down=ff1_down[l],
                  norm_mix=norm_mix[l], w_in=w_in[l], shift_mu=shift_mu[l], decay_w0=decay_w0[l],
                  decay_w2=decay_w2[l], iclr_a0=iclr_a0[l], iclr_a2=iclr_a2[l], gate_g2=gate_g2[l],
                  k_k=k_k[l], k_a=k_a[l], r_k=r_k[l], gn_gain=gn_gain[l], gn_bias=gn_bias[l],
                  q_norm=q_norm[l], k_norm=k_norm[l], sinks=sinks[l], w_up_a=w_up_a[l],
                  w_up_b=w_up_b[l], w_o=w_o[l], norm_ff2=norm_ff2[l], ff2_gate=ff2_gate[l],
                  ff2_up=ff2_up[l], ff2_down=ff2_down[l])
        xp, wkv_p, sh_p, k_pr, v_pr = trunk_layer(
            xp, pos_p, jnp.zeros((bp, A_COLS), xp.dtype),
            jnp.zeros((bp, A_HEADS, HEAD_DIM, HEAD_DIM), jnp.float32), band_attention, lw)
        attend_s = functools.partial(cached_attention, ck=cache_k[l], cv=cache_v[l])
        xs, wkv_s, sh_s, k_sm, v_sm = trunk_layer(xs, pos_s, state_shift[l], state_wkv[l], attend_s, lw)
        p_wkv.append(wkv_p.astype(state_wkv.dtype))
        p_shift.append(sh_p.astype(state_shift.dtype))
        p_k.append(k_pr[:, -WINDOW:].astype(cache_k.dtype))
        p_v.append(v_pr[:, -WINDOW:].astype(cache_v.dtype))
        s_wkv.append(wkv_s.astype(state_wkv.dtype))
        s_shift.append(sh_s.astype(state_shift.dtype))
        s_k.append(k_sm.astype(cache_k.dtype))
        s_v.append(v_sm.astype(cache_v.dtype))
    return (xp, xs, jnp.stack(p_wkv), jnp.stack(p_shift), jnp.stack(p_k), jnp.stack(p_v),
            jnp.stack(s_wkv), jnp.stack(s_shift), jnp.stack(s_k), jnp.stack(s_v))
```

```python
import functools

import jax
import jax.numpy as jnp
from jax import lax
from jax.experimental import pallas as pl
from jax.experimental.pallas import tpu as pltpu

F32 = jnp.float32
BF16 = jnp.bfloat16

HEAD_DIM = 64
A_WIDTH = 1024
A_HEADS = A_WIDTH // HEAD_DIM
DECAY_LORA = 64
ICLR_LORA = 64
GATE_LORA = 160
GN_EPS = 64e-5
RMS_EPS = 1e-6
B_HEADS = 16
B_KV_HEADS = 4
B_GROUP = B_HEADS // B_KV_HEADS
B_Q = B_HEADS * HEAD_DIM
B_KV = B_KV_HEADS * HEAD_DIM
CHUNK = 64
WINDOW = 128
ROPE_THETA = 10000.0
PAST_LEN = 1024
D_MODEL = 2048
A_COLS = 3 * A_WIDTH + DECAY_LORA + ICLR_LORA + GATE_LORA

LANES = 128
SUBLANES = 8
VMEM_LIMIT_BYTES = 56 * 1024 * 1024

_COL_R = 0
_COL_K = _COL_R + A_WIDTH
_COL_V = _COL_K + A_WIDTH
_COL_Q = _COL_V + A_WIDTH
_COL_GA = _COL_Q + B_Q
_COL_GB = _COL_GA + D_MODEL
_COL_LORA = _COL_GB + D_MODEL
_LORA_W = 512
_COL_KV = _COL_LORA + _LORA_W
_KV_W = 2 * B_KV
NP_COLS = _COL_KV + _KV_W
_LORA_G0 = LANES
_LORA_GW = 2 * LANES
PAIRS = A_HEADS // 2
_VEC_ROWS = 16


def _cparams(n_axes):
    return pltpu.CompilerParams(dimension_semantics=("arbitrary",) * n_axes,
                                vmem_limit_bytes=VMEM_LIMIT_BYTES)


def _dot(a, b):
    return jnp.dot(a.astype(BF16), b.astype(BF16), preferred_element_type=F32)


def _dot_nt(a, b):
    return lax.dot_general(a.astype(BF16), b.astype(BF16), (((1,), (1,)), ((), ())),
                           preferred_element_type=F32)


def _dot_tn(a, b):
    return lax.dot_general(a.astype(BF16), b.astype(BF16), (((0,), (0,)), ((), ())),
                           preferred_element_type=F32)


def _split3(x):
    hi = x.astype(BF16)
    r1 = x - hi.astype(F32)
    mid = r1.astype(BF16)
    lo = (r1 - mid.astype(F32)).astype(BF16)
    return hi, mid, lo


def _rms_rows(x, gain):
    ms = jnp.mean(x * x, axis=-1, keepdims=True)
    return x * lax.rsqrt(ms + RMS_EPS) * gain


def _ffn_kernel(x_ref, g_ref, wg_ref, wu_ref, wd_ref, o_ref, h_ref):
    f = pl.program_id(1)

    @pl.when(f == 0)
    def _():
        x = x_ref[...]
        h_ref[...] = _rms_rows(x, g_ref[...]).astype(BF16)
        o_ref[...] = x

    h = h_ref[...]
    gate = jnp.dot(h, wg_ref[...], preferred_element_type=F32)
    up = jnp.dot(h, wu_ref[...], preferred_element_type=F32)
    act = (0.5 * gate * jax.nn.sigmoid(gate) * up).astype(BF16)
    o_ref[...] += jnp.dot(act, wd_ref[...], preferred_element_type=F32)


def _ffn(x, gain, wg, wu, wd, *, tm, tf):
    t, d = x.shape
    fp = wg.shape[1]
    return pl.pallas_call(
        _ffn_kernel,
        grid=(t // tm, fp // tf),
        in_specs=[pl.BlockSpec((tm, d), lambda i, f: (i, 0)),
                  pl.BlockSpec((1, d), lambda i, f: (0, 0)),
                  pl.BlockSpec((d, tf), lambda i, f: (0, f)),
                  pl.BlockSpec((d, tf), lambda i, f: (0, f)),
                  pl.BlockSpec((tf, d), lambda i, f: (f, 0))],
        out_specs=pl.BlockSpec((tm, d), lambda i, f: (i, 0)),
        out_shape=jax.ShapeDtypeStruct((t, d), F32),
        scratch_shapes=[pltpu.VMEM((tm, d), BF16)],
        compiler_params=_cparams(2),
        name="ffn",
    )(x, gain, wg, wu, wd)


def _inproj_kernel(x_ref, g_ref, w_ref, o_ref, h_ref):
    @pl.when(pl.program_id(1) == 0)
    def _():
        h_ref[...] = _rms_rows(x_ref[...], g_ref[...]).astype(BF16)

    o_ref[...] = jnp.dot(h_ref[...], w_ref[...], preferred_element_type=F32)


def _inproj(x, gain, w, *, tm, tn):
    t, d = x.shape
    n = w.shape[1]
    return pl.pallas_call(
        _inproj_kernel,
        grid=(t // tm, n // tn),
        in_specs=[pl.BlockSpec((tm, d), lambda i, j: (i, 0)),
                  pl.BlockSpec((1, d), lambda i, j: (0, 0)),
                  pl.BlockSpec((d, tn), lambda i, j: (0, j))],
        out_specs=pl.BlockSpec((tm, tn), lambda i, j: (i, j)),
        out_shape=jax.ShapeDtypeStruct((t, n), F32),
        scratch_shapes=[pltpu.VMEM((tm, d), BF16)],
        compiler_params=_cparams(2),
        name="inproj",
    )(x, gain, w)


def _seg_sum(x, first):
    s0 = jnp.sum(jnp.where(first, x, 0.0), axis=1, keepdims=True)
    s1 = jnp.sum(jnp.where(first, 0.0, x), axis=1, keepdims=True)
    return jnp.where(first, s0, s1)


def _unit_lower_inverse(n, eye, c):
    x = eye - n
    p = n
    k = 1
    while 2 * k < c:
        p = _dot(p, p)
        x = x + _dot(x, p)
        k *= 2
    return x


def _rwkv_kernel(r_ref, k_ref, v_ref, l_ref, rp_ref, kp_ref, vp_ref, lp_ref,
                 sr_ref, sk_ref, sv_ref, sl_ref, vec_ref, mul_ref, wl1_ref, wl2_ref, s0_ref,
                 y_ref, sout_ref, z_ref, *, sb, tb, c):
    b = pl.program_id(2)
    nc = tb // c
    hd = HEAD_DIM

    vec = vec_ref[...]
    mu_r, mu_k, mu_v = vec[0:1], vec[1:2], vec[2:3]
    w0, a0, k_k, k_a, r_k = vec[3:4], vec[4:5], vec[5:6], vec[6:7], vec[7:8]
    gn_g, gn_b = vec[8:9], vec[9:10]
    mu_l = mul_ref[...]

    lane = lax.broadcasted_iota(jnp.int32, (1, LANES), 1)
    first = lane < hd
    row_t = lax.broadcasted_iota(jnp.int32, (tb, 1), 0)
    ri = lax.broadcasted_iota(jnp.int32, (c, c), 0)
    ci = lax.broadcasted_iota(jnp.int32, (c, c), 1)
    lower_incl = ri >= ci
    lower_strict = ri > ci
    eye_c = jnp.where(ri == ci, 1.0, 0.0).astype(F32)
    tri = jnp.where(lower_incl, 1.0, 0.0).astype(BF16)
    rj = lax.broadcasted_iota(jnp.int32, (hd, hd), 0)
    cj = lax.broadcasted_iota(jnp.int32, (hd, hd), 1)
    eye_h = rj == cj

    @pl.when(b == 0)
    def _():
        for s in range(sb):
            for h in range(2):
                z_ref[s, h] = s0_ref[s, h].T

    def shifted(ref, pref, sref, s, mu):
        raw = ref[s]
        before = jnp.where(b == 0, sref[s], pref[s][SUBLANES - 1:SUBLANES, :])
        prev = jnp.where(row_t == 0, before, pltpu.roll(raw, 1, 0))
        return raw + (prev - raw) * mu

    for s in range(sb):
        rx = shifted(r_ref, rp_ref, sr_ref, s, mu_r)
        kx = shifted(k_ref, kp_ref, sk_ref, s, mu_k)
        vx = shifted(v_ref, vp_ref, sv_ref, s, mu_v)
        lx = shifted(l_ref, lp_ref, sl_ref, s, mu_l)

        l01 = lx[:, 0:LANES]
        z01 = jnp.where(first, jnp.tanh(l01), l01)
        wa = _dot(z01, wl1_ref[0])
        w_pre = wa[:, 0:LANES] + w0
        a_pre = wa[:, LANES:2 * LANES] + a0
        gate = _dot(jax.nn.sigmoid(lx[:, _LORA_G0:_LORA_G0 + _LORA_GW]), wl2_ref[0])
        softplus = jnp.maximum(-w_pre, 0.0) + jnp.log1p(jnp.exp(-jnp.abs(w_pre)))
        logd = -jnp.exp(-softplus - 0.5)
        a = jax.nn.sigmoid(a_pre)
        kxk = kx * k_k
        kk = kxk / jnp.maximum(jnp.sqrt(_seg_sum(kxk * kxk, first)), 1e-12)
        kp = kx * (1.0 + (a - 1.0) * k_a)
        bb = kk * a
        bonus = _seg_sum(rx * kp * r_k, first) * vx

        z = [z_ref[s, 0], z_ref[s, 1]]
        for ch in range(nc):
            rows = slice(ch * c, (ch + 1) * c)
            ld = logd[rows]
            hi, mid, lo = _split3(ld)
            cin = (jnp.dot(tri, hi, preferred_element_type=F32)
                   + jnp.dot(tri, mid, preferred_element_type=F32)
                   + jnp.dot(tri, lo, preferred_element_type=F32))
            c_end = cin[c - 1:c, :]
            e_in = jnp.exp(cin)
            e_ex = jnp.exp(cin - ld)
            e_inv = jnp.exp(-cin)
            e_dec = jnp.exp(c_end - cin)
            p_end = jnp.exp(c_end)
            kk_t = kk[rows] * e_ex
            r_t = rx[rows] * e_in
            k_h = kp[rows] * e_inv
            b_h = bb[rows] * e_inv
            k_d = kp[rows] * e_dec
            b_d = bb[rows] * e_dec
            v_c = vx[rows]
            ys = []
            for h in range(2):
                sl = slice(h * hd, (h + 1) * hd)
                kkt, rt, kh, bh, kd, bd, vv = (kk_t[:, sl], r_t[:, sl], k_h[:, sl], b_h[:, sl],
                                               k_d[:, sl], b_d[:, sl], v_c[:, sl])
                lhs = jnp.concatenate([kkt, rt], axis=0)
                g_b = _dot_nt(lhs, bh)
                g_k = _dot_nt(lhs, kh)
                m_ab = jnp.where(lower_strict, g_b[0:c], 0.0)
                a_rb = jnp.where(lower_incl, g_b[c:2 * c], 0.0)
                m_ak = jnp.where(lower_strict, g_k[0:c], 0.0)
                a_rk = jnp.where(lower_incl, g_k[c:2 * c], 0.0)
                t_inv = _unit_lower_inverse(m_ab, eye_c, c)
                tkk = _dot(t_inv, kkt)
                w_v = _dot(t_inv, _dot(m_ak, vv))
                a_c = jnp.where(eye_h, p_end[:, sl], 0.0) - _dot_tn(bd, tkk)
                u_c = _dot_tn(kd, vv) - _dot_tn(bd, w_v)
                q_e = rt - _dot(a_rb, tkk)
                y_i = _dot(a_rk, vv) - _dot(a_rb, w_v)
                y = _dot(q_e, z[h]) + y_i
                z[h] = _dot(a_c, z[h]) + u_c
                mean = jnp.mean(y, axis=-1, keepdims=True)
                yc = y - mean
                var = jnp.mean(yc * yc, axis=-1, keepdims=True)
                ys.append(yc * lax.rsqrt(var + GN_EPS))
            yn = jnp.concatenate(ys, axis=1)
            out = (yn * gn_g + gn_b + bonus[rows]) * gate[rows]
            y_ref[s, rows, :] = out.astype(y_ref.dtype)
        for h in range(2):
            z_ref[s, h] = z[h]
            sout_ref[s, h] = z[h].T


def _rwkv(p3, shiftp, s0, vec, mu_l, wl1, wl2, *, sb, tb, c):
    s, t, _ = p3.shape
    kb, vb = _COL_K // LANES, _COL_V // LANES
    lb = _COL_LORA // _LORA_W
    tpb = tb // SUBLANES

    def cur(col0):
        return pl.BlockSpec((sb, tb, LANES), lambda i, p, b: (i, b, col0 + p))

    def prv(col0):
        return pl.BlockSpec((sb, SUBLANES, LANES), lambda i, p, b: (i, jnp.maximum(b * tpb - 1, 0), col0 + p))

    def shf(col0):
        return pl.BlockSpec((sb, 1, LANES), lambda i, p, b: (i, 0, col0 + p))

    in_specs = [
        cur(0), cur(kb), cur(vb),
        pl.BlockSpec((sb, tb, _LORA_W), lambda i, p, b: (i, b, lb)),
        prv(0), prv(kb), prv(vb),
        pl.BlockSpec((sb, SUBLANES, _LORA_W), lambda i, p, b: (i, jnp.maximum(b * tpb - 1, 0), lb)),
        shf(0), shf(kb), shf(vb),
        pl.BlockSpec((sb, 1, _LORA_W), lambda i, p, b: (i, 0, 3 * A_WIDTH // _LORA_W)),
        pl.BlockSpec((_VEC_ROWS, LANES), lambda i, p, b: (0, p)),
        pl.BlockSpec((1, _LORA_W), lambda i, p, b: (0, 0)),
        pl.BlockSpec((1, LANES, 2 * LANES), lambda i, p, b: (p, 0, 0)),
        pl.BlockSpec((1, _LORA_GW, LANES), lambda i, p, b: (p, 0, 0)),
        pl.BlockSpec((sb, 2, HEAD_DIM, HEAD_DIM), lambda i, p, b: (i, p, 0, 0)),
    ]
    out_specs = [
        pl.BlockSpec((sb, tb, LANES), lambda i, p, b: (i, b, p)),
        pl.BlockSpec((sb, 2, HEAD_DIM, HEAD_DIM), lambda i, p, b: (i, p, 0, 0)),
    ]
    return pl.pallas_call(
        functools.partial(_rwkv_kernel, sb=sb, tb=tb, c=c),
        grid=(s // sb, PAIRS, t // tb),
        in_specs=in_specs,
        out_specs=out_specs,
        out_shape=[jax.ShapeDtypeStruct((s, t, A_WIDTH), BF16),
                   jax.ShapeDtypeStruct((s, A_HEADS, HEAD_DIM, HEAD_DIM), F32)],
        scratch_shapes=[pltpu.VMEM((sb, 2, HEAD_DIM, HEAD_DIM), F32)],
        compiler_params=_cparams(3),
        name="rwkv",
    )(p3, p3, p3, p3, p3, p3, p3, p3, shiftp, shiftp, shiftp, shiftp, vec, mu_l, wl1, wl2, s0)


def _norm_rope(x, gain, cos, sin_signed):
    lane = lax.broadcasted_iota(jnp.int32, (1, LANES), 1)
    first = lane < HEAD_DIM
    low_half = (lane % HEAD_DIM) < (HEAD_DIM // 2)
    parts = []
    for j in range(x.shape[1] // LANES):
        xs = x[:, j * LANES:(j + 1) * LANES]
        ms = _seg_sum(xs * xs, first) * (1.0 / HEAD_DIM)
        xn = xs * lax.rsqrt(ms + RMS_EPS) * gain
        partner = jnp.where(low_half, pltpu.roll(xn, LANES - HEAD_DIM // 2, 1),
                            pltpu.roll(xn, HEAD_DIM // 2, 1))
        parts.append(xn * cos + partner * sin_signed)
    return jnp.concatenate(parts, axis=1) if len(parts) > 1 else parts[0]


def _sink_attend(q, k_all, v_all, sink, mask):
    s = _dot_nt(q, k_all) * (HEAD_DIM ** -0.5)
    if mask is not None:
        s = jnp.where(mask, s, -1e30)
    m = jnp.maximum(jnp.max(s, axis=-1, keepdims=True), sink)
    p = jnp.exp(s - m)
    denom = jnp.sum(p, axis=-1, keepdims=True) + jnp.exp(sink - m)
    return _dot(p / denom, v_all)


def _band_kernel(sink_ref, q_ref, kv_ref, kvp_ref, cq_ref, sq_ref, cp_ref, sp_ref, gq_ref, gk_ref,
                 y_ref, ko_ref, *, tq):
    i = pl.program_id(0)
    hd = HEAD_DIM
    q = _norm_rope(q_ref[...], gq_ref[...], cq_ref[...], sq_ref[...])
    kv = kv_ref[...]
    kvp = kvp_ref[...]
    k_cur = _norm_rope(kv[:, 0:B_KV], gk_ref[...], cq_ref[...], sq_ref[...])
    k_prev = _norm_rope(kvp[:, 0:B_KV], gk_ref[...], cp_ref[...], sp_ref[...])
    ko_ref[...] = k_cur
    v_cur = kv[:, B_KV:2 * B_KV]
    v_prev = kvp[:, B_KV:2 * B_KV]

    nk = WINDOW + tq
    qc = lax.broadcasted_iota(jnp.int32, (tq, nk), 0) // CHUNK
    col = lax.broadcasted_iota(jnp.int32, (tq, nk), 1)
    rel = col // CHUNK - qc
    mask = (rel >= 0) & (rel <= WINDOW // CHUNK) & ((col >= WINDOW) | (i > 0))

    outs = []
    for g in range(B_KV_HEADS):
        gs = slice(g * hd, (g + 1) * hd)
        k_all = jnp.concatenate([k_prev[:, gs], k_cur[:, gs]], axis=0)
        v_all = jnp.concatenate([v_prev[:, gs], v_cur[:, gs]], axis=0)
        for j in range(B_GROUP):
            h = g * B_GROUP + j
            outs.append(_sink_attend(q[:, h * hd:(h + 1) * hd], k_all, v_all, sink_ref[h], mask))
    y_ref[...] = jnp.concatenate(outs, axis=1).astype(y_ref.dtype)


def _attn_band(p2, cos, sin_signed, gq, gk, sinks, *, tq):
    t = p2.shape[0]
    qb, kvb = _COL_Q // B_Q, _COL_KV // _KV_W
    wpb = tq // WINDOW

    def prev_idx(i):
        return jnp.maximum(i * wpb - 1, 0)

    return pl.pallas_call(
        functools.partial(_band_kernel, tq=tq),
        grid=(t // tq,),
        in_specs=[pl.BlockSpec(memory_space=pltpu.SMEM),
                  pl.BlockSpec((tq, B_Q), lambda i: (i, qb)),
                  pl.BlockSpec((tq, _KV_W), lambda i: (i, kvb)),
                  pl.BlockSpec((WINDOW, _KV_W), lambda i: (prev_idx(i), kvb)),
                  pl.BlockSpec((tq, LANES), lambda i: (i, 0)),
                  pl.BlockSpec((tq, LANES), lambda i: (i, 0)),
                  pl.BlockSpec((WINDOW, LANES), lambda i: (prev_idx(i), 0)),
                  pl.BlockSpec((WINDOW, LANES), lambda i: (prev_idx(i), 0)),
                  pl.BlockSpec((1, LANES), lambda i: (0, 0)),
                  pl.BlockSpec((1, LANES), lambda i: (0, 0))],
        out_specs=[pl.BlockSpec((tq, B_Q), lambda i: (i, 0)),
                   pl.BlockSpec((tq, B_KV), lambda i: (i, 0))],
        out_shape=[jax.ShapeDtypeStruct((t, B_Q), BF16),
                   jax.ShapeDtypeStruct((t, B_KV), F32)],
        compiler_params=_cparams(1),
        name="attn_band",
    )(sinks, p2, p2, p2, cos, sin_signed, cos, sin_signed, gq, gk)


def _cached_kernel(sink_ref, q_ref, kv_ref, ck_ref, cv_ref, c_ref, s_ref, gq_ref, gk_ref,
                   y_ref, ko_ref, *, sb):
    hd = HEAD_DIM
    for s in range(sb):
        q = _norm_rope(q_ref[s], gq_ref[...], c_ref[...], s_ref[...])
        kv = kv_ref[s]
        k_new = _norm_rope(kv[:, 0:B_KV], gk_ref[...], c_ref[...], s_ref[...])
        ko_ref[s] = k_new
        v_new = kv[:, B_KV:2 * B_KV]
        ck = ck_ref[s]
        cv = cv_ref[s]
        outs = []
        for g in range(B_KV_HEADS):
            gs = slice(g * hd, (g + 1) * hd)
            k_all = jnp.concatenate([ck[:, gs], k_new[:, gs]], axis=0)
            v_all = jnp.concatenate([cv[:, gs], v_new[:, gs]], axis=0)
            for j in range(B_GROUP):
                h = g * B_GROUP + j
                outs.append(_sink_attend(q[:, h * hd:(h + 1) * hd], k_all, v_all, sink_ref[h], None))
        y_ref[s] =jnp.concatenate(outs, axis=1).astype(y_ref.dtype)


def _attn_cached(p3, ck, cv, cos, sin_signed, gq, gk, sinks, *, sb):
    s, t, _ = p3.shape
    w = ck.shape[1]
    qb, kvb = _COL_Q // B_Q, _COL_KV // _KV_W
    return pl.pallas_call(
        functools.partial(_cached_kernel, sb=sb),
        grid=(s // sb,),
        in_specs=[pl.BlockSpec(memory_space=pltpu.SMEM),
                  pl.BlockSpec((sb, t, B_Q), lambda i: (i, 0, qb)),
                  pl.BlockSpec((sb, t, _KV_W), lambda i: (i, 0, kvb)),
                  pl.BlockSpec((sb, w, B_KV), lambda i: (i, 0, 0)),
                  pl.BlockSpec((sb, w, B_KV), lambda i: (i, 0, 0)),
                  pl.BlockSpec((t, LANES), lambda i: (0, 0)),
                  pl.BlockSpec((t, LANES), lambda i: (0, 0)),
                  pl.BlockSpec((1, LANES), lambda i: (0, 0)),
                  pl.BlockSpec((1, LANES), lambda i: (0, 0))],
        out_specs=[pl.BlockSpec((sb, t, B_Q), lambda i: (i, 0, 0)),
                   pl.BlockSpec((sb, t, B_KV), lambda i: (i, 0, 0))],
        out_shape=[jax.ShapeDtypeStruct((s, t, B_Q), BF16),
                   jax.ShapeDtypeStruct((s, t, B_KV), F32)],
        compiler_params=_cparams(1),
        name="attn_cached",
    )(sinks, p3, p3, ck, cv, cos, sin_signed, gq, gk)


def _outproj_kernel(x_ref, ya_ref, yb_ref, ga_ref, gb_ref, wa_ref, wb_ref, wo_ref, o_ref, m_ref):
    @pl.when(pl.program_id(1) == 0)
    def _():
        ua = jnp.dot(ya_ref[...], wa_ref[...], preferred_element_type=F32)
        ub = jnp.dot(yb_ref[...], wb_ref[...], preferred_element_type=F32)
        merged = jax.nn.sigmoid(ga_ref[...]) * ua + jax.nn.sigmoid(gb_ref[...]) * ub
        m_ref[...] = merged.astype(BF16)

    o_ref[...] = x_ref[...] + jnp.dot(m_ref[...], wo_ref[...], preferred_element_type=F32)


def _outproj(x, ya, yb, p2, wa, wb, wo, *, tm, tn):
    t, d = x.shape
    gab, gbb = _COL_GA // d, _COL_GB // d
    return pl.pallas_call(
        _outproj_kernel,
        grid=(t // tm, d // tn),
        in_specs=[pl.BlockSpec((tm, tn), lambda i, j: (i, j)),
                  pl.BlockSpec((tm, A_WIDTH), lambda i, j: (i, 0)),
                  pl.BlockSpec((tm, B_Q), lambda i, j: (i, 0)),
                  pl.BlockSpec((tm, d), lambda i, j: (i, gab)),
                  pl.BlockSpec((tm, d), lambda i, j: (i, gbb)),
                  pl.BlockSpec((A_WIDTH, d), lambda i, j: (0, 0)),
                  pl.BlockSpec((B_Q, d), lambda i, j: (0, 0)),
                  pl.BlockSpec((d, tn), lambda i, j: (0, j))],
        out_specs=pl.BlockSpec((tm, tn), lambda i, j: (i, j)),
        out_shape=jax.ShapeDtypeStruct((t, d), F32),
        scratch_shapes=[pltpu.VMEM((tm, d), BF16)],
        compiler_params=_cparams(2),
        name="outproj",
    )(x, ya, yb, p2, p2, wa, wb, wo)


def _a_pieces(a):
    o = 0
    out = []
    for w in (A_WIDTH, DECAY_LORA, A_WIDTH, A_WIDTH, ICLR_LORA, GATE_LORA):
        out.append(a[..., o:o + w])
        o += w
    return out


def _regroup_a(a, pad_to):
    r, w_lo, k, v, a_lo, g_lo = _a_pieces(a)
    lora = jnp.concatenate([w_lo, a_lo, g_lo], axis=-1)
    lora = jnp.pad(lora, [(0, 0)] * (a.ndim - 1) + [(0, pad_to - lora.shape[-1])])
    return r, k, v, lora


def _regroup_w_in(w):
    wa = w[:, :A_COLS]
    o = A_COLS
    wq = w[:, o:o + B_Q]
    wk = w[:, o + B_Q:o + B_Q + B_KV]
    wv = w[:, o + B_Q + B_KV:o + B_Q + 2 * B_KV]
    o += B_Q + 2 * B_KV
    wga = w[:, o:o + D_MODEL]
    wgb = w[:, o + D_MODEL:o + 2 * D_MODEL]
    r, k, v, lora = _regroup_a(wa, _LORA_W)
    return jnp.concatenate([r, k, v, wq, wga, wgb, lora, wk, wv], axis=1).astype(BF16)


def _shift_row_to_a(p_last):
    r = p_last[..., _COL_R:_COL_R + A_WIDTH]
    k = p_last[..., _COL_K:_COL_K + A_WIDTH]
    v = p_last[..., _COL_V:_COL_V + A_WIDTH]
    o = _COL_LORA
    w_lo = p_last[..., o:o + DECAY_LORA]
    a_lo = p_last[..., o + DECAY_LORA:o + DECAY_LORA + ICLR_LORA]
    g_lo = p_last[..., o + DECAY_LORA + ICLR_LORA:o + DECAY_LORA + ICLR_LORA + GATE_LORA]
    return jnp.concatenate([r, w_lo, k, v, a_lo, g_lo], axis=-1)


def _lora_weights(decay_w2, iclr_a2, gate_g2):
    dw = decay_w2.reshape(DECAY_LORA, PAIRS, LANES)
    ia = iclr_a2.reshape(ICLR_LORA, PAIRS, LANES)
    top = jnp.concatenate([dw, jnp.zeros_like(dw)], axis=2)
    bot = jnp.concatenate([jnp.zeros_like(ia), ia], axis=2)
    wl1 = jnp.concatenate([top, bot], axis=0).transpose(1, 0, 2)
    gg = gate_g2.reshape(GATE_LORA, PAIRS, LANES)
    gg = jnp.pad(gg, ((0, _LORA_GW - GATE_LORA), (0, 0), (0, 0)))
    wl2 = gg.transpose(1, 0, 2)
    return wl1.astype(BF16), wl2.astype(BF16)


def _rope_tables(pos):
    half = HEAD_DIM // 2
    inv = ROPE_THETA ** (-jnp.arange(half, dtype=F32) / half)
    ang = pos.astype(F32)[:, None] * inv[None, :]
    cos = jnp.cos(ang)
    sin = jnp.sin(ang)
    cos_t = jnp.tile(cos, (1, LANES // half))
    sin_t = jnp.tile(jnp.concatenate([-sin, sin], axis=1), (1, LANES // HEAD_DIM))
    return cos_t, sin_t


def _pad_cols(w, n):
    return jnp.pad(w, [(0, 0)] * (w.ndim - 1) + [(0, n - w.shape[-1])])


_FF_TILE = 512


def kernel(x_prompt, x_sample, cache_k, cache_v, state_wkv, state_shift, norm_ff1, ff1_gate, ff1_up, ff1_down,
           norm_mix, w_in, shift_mu, decay_w0, decay_w2, iclr_a0, iclr_a2, gate_g2, k_k, k_a, r_k, gn_gain,
           gn_bias, q_norm, k_norm, sinks, w_up_a, w_up_b, w_o, norm_ff2, ff2_gate, ff2_up, ff2_down):
    depth = norm_ff1.shape[0]
    bp, tp, d = x_prompt.shape
    bs, ts, _ = x_sample.shape
    assert bp == 1 and d == D_MODEL
    d_ff = ff1_gate.shape[2]
    ffp = -(-d_ff // _FF_TILE) * _FF_TILE

    xp = x_prompt.reshape(tp, d)
    xs = x_sample.reshape(bs * ts, d)

    cos_p, sin_p = _rope_tables(jnp.arange(tp))
    cos_s, sin_s = _rope_tables(PAST_LEN + jnp.arange(ts))
    zero_shift = jnp.zeros((bp, 1, 3 * A_WIDTH + _LORA_W), F32)
    zero_state = jnp.zeros((bp, A_HEADS, HEAD_DIM, HEAD_DIM), F32)

    outs = {k: [] for k in ("p_wkv", "p_shift", "p_k", "p_v", "s_wkv", "s_shift", "s_k", "s_v")}
    for l in range(depth):
        wg1 = _pad_cols(ff1_gate[l].astype(BF16), ffp)
        wu1 = _pad_cols(ff1_up[l].astype(BF16), ffp)
        wd1 = jnp.pad(ff1_down[l].astype(BF16), ((0, ffp - d_ff), (0, 0)))
        wg2 = _pad_cols(ff2_gate[l].astype(BF16), ffp)
        wu2 = _pad_cols(ff2_up[l].astype(BF16), ffp)
        wd2 = jnp.pad(ff2_down[l].astype(BF16), ((0, ffp - d_ff), (0, 0)))
        w_in_l = _regroup_w_in(w_in[l])
        mu_r, mu_k, mu_v, mu_l = _regroup_a(shift_mu[l][None, :], _LORA_W)
        rows = [mu_r, mu_k, mu_v, decay_w0[l][None], iclr_a0[l][None], k_k[l][None], k_a[l][None],
                r_k[l].reshape(1, A_WIDTH), gn_gain[l][None], gn_bias[l][None]]
        vec = jnp.concatenate(rows + [jnp.zeros((_VEC_ROWS - len(rows), A_WIDTH), F32)], axis=0)
        wl1, wl2 = _lora_weights(decay_w2[l], iclr_a2[l], gate_g2[l])
        gq = jnp.tile(q_norm[l], LANES // HEAD_DIM)[None, :]
        gk = jnp.tile(k_norm[l], LANES // HEAD_DIM)[None, :]
        wa = w_up_a[l].astype(BF16)
        wb = w_up_b[l].astype(BF16)
        wo = w_o[l].astype(BF16)
        sr, sk, sv, sl = _regroup_a(state_shift[l], _LORA_W)
        shift_s = jnp.concatenate([sr, sk, sv, sl], axis=-1)[:, None, :]

        g1 = norm_ff1[l][None, :]
        xp = _ffn(xp, g1, wg1, wu1, wd1, tm=512, tf=_FF_TILE)
        xs = _ffn(xs, g1, wg1, wu1, wd1, tm=512, tf=_FF_TILE)

        gm = norm_mix[l][None, :]
        pp = _inproj(xp, gm, w_in_l, tm=512, tn=512)
        ps = _inproj(xs, gm, w_in_l, tm=512, tn=512)
        pp3 = pp.reshape(bp, tp, NP_COLS)
        ps3 = ps.reshape(bs, ts, NP_COLS)

        ya_p, wkv_p = _rwkv(pp3, zero_shift, zero_state, vec, mu_l, wl1, wl2, sb=1, tb=512, c=CHUNK)
        ya_s, wkv_s = _rwkv(ps3, shift_s, state_wkv[l], vec, mu_l, wl1, wl2, sb=8, tb=ts, c=ts)

        yb_p, kr_p = _attn_band(pp, cos_p, sin_p, gq, gk, sinks[l], tq=256)
        ck = cache_k[l].reshape(bs, -1, B_KV)
        cv = cache_v[l].reshape(bs, -1, B_KV)
        yb_s, kr_s = _attn_cached(ps3, ck, cv, cos_s, sin_s, gq, gk, sinks[l], sb=8)

        xp = _outproj(xp, ya_p.reshape(tp, A_WIDTH), yb_p, pp, wa, wb, wo, tm=256, tn=512)
        xs = _outproj(xs, ya_s.reshape(bs * ts, A_WIDTH), yb_s.reshape(bs * ts, B_Q), ps, wa, wb, wo,
                      tm=256, tn=512)

        g2 = norm_ff2[l][None, :]
        xp = _ffn(xp, g2, wg2, wu2, wd2, tm=512, tf=_FF_TILE)
        xs = _ffn(xs, g2, wg2, wu2, wd2, tm=512, tf=_FF_TILE)

        vcol = _COL_KV + B_KV
        outs["p_wkv"].append(wkv_p)
        outs["p_shift"].append(_shift_row_to_a(pp3[:, -1, :]))
        outs["p_k"].append(kr_p[-WINDOW:].reshape(bp, WINDOW, B_KV_HEADS, HEAD_DIM))
        outs["p_v"].append(pp3[:, -WINDOW:, vcol:vcol + B_KV].reshape(bp, WINDOW, B_KV_HEADS, HEAD_DIM))
        outs["s_wkv"].append(wkv_s)
        outs["s_shift"].append(_shift_row_to_a(ps3[:, -1, :]))
        outs["s_k"].append(kr_s.reshape(bs, ts, B_KV_HEADS, HEAD_DIM))
        outs["s_v"].append(ps3[:, :, vcol:vcol + B_KV].reshape(bs, ts, B_KV_HEADS, HEAD_DIM))

    return (xp.reshape(bp, tp, d), xs.reshape(bs, ts, d),
            jnp.stack(outs["p_wkv"]), jnp.stack(outs["p_shift"]), jnp.stack(outs["p_k"]), jnp.stack(outs["p_v"]),
            jnp.stack(outs["s_wkv"]), jnp.stack(outs["s_shift"]), jnp.stack(outs["s_k"]), jnp.stack(outs["s_v"]))
```

```python
import functools

import jax
import jax.numpy as jnp
from jax import lax
from jax.experimental import pallas as pl
from jax.experimental.pallas import tpu as pltpu

F32 = jnp.float32
BF16 = jnp.bfloat16

HEAD_DIM = 64
A_WIDTH = 1024
A_HEADS = A_WIDTH // HEAD_DIM
DECAY_LORA = 64
ICLR_LORA = 64
GATE_LORA = 160
GN_EPS = 64e-5
RMS_EPS = 1e-6
B_HEADS = 16
B_KV_HEADS = 4
B_GROUP = B_HEADS // B_KV_HEADS
B_Q = B_HEADS * HEAD_DIM
B_KV = B_KV_HEADS * HEAD_DIM
CHUNK = 64
WINDOW = 128
ROPE_THETA = 10000.0
PAST_LEN = 1024
D_MODEL = 2048
A_COLS = 3 * A_WIDTH + DECAY_LORA + ICLR_LORA + GATE_LORA

LANES = 128
SUBLANES = 8
VMEM_LIMIT_BYTES = 56 * 1024 * 1024

_COL_R = 0
_COL_K = _COL_R + A_WIDTH
_COL_V = _COL_K + A_WIDTH
_COL_Q = _COL_V + A_WIDTH
_COL_GA = _COL_Q + B_Q
_COL_GB = _COL_GA + D_MODEL
_COL_LORA = _COL_GB + D_MODEL
_LORA_W = 512
_COL_KV = _COL_LORA + _LORA_W
_KV_W = 2 * B_KV
NP_COLS = _COL_KV + _KV_W
_LORA_G0 = LANES
_LORA_GW = 2 * LANES
PAIRS = A_HEADS // 2
_VEC_ROWS = 16


def _cparams(n_axes):
    return pltpu.CompilerParams(dimension_semantics=("arbitrary",) * n_axes,
                                vmem_limit_bytes=VMEM_LIMIT_BYTES)


def _dot(a, b):
    return jnp.dot(a.astype(BF16), b.astype(BF16), preferred_element_type=F32)


def _dot_nt(a, b):
    return lax.dot_general(a.astype(BF16), b.astype(BF16), (((1,), (1,)), ((), ())),
                           preferred_element_type=F32)


def _dot_tn(a, b):
    return lax.dot_general(a.astype(BF16), b.astype(BF16), (((0,), (0,)), ((), ())),
                           preferred_element_type=F32)


def _split3(x):
    hi = x.astype(BF16)
    r1 = x - hi.astype(F32)
    mid = r1.astype(BF16)
    lo = (r1 - mid.astype(F32)).astype(BF16)
    return hi, mid, lo


def _rms_rows(x, gain):
    ms = jnp.mean(x * x, axis=-1, keepdims=True)
    return x * lax.rsqrt(ms + RMS_EPS) * gain


def _ffn_kernel(x_ref, g_ref, wg_ref, wu_ref, wd_ref, o_ref, h_ref):
    f = pl.program_id(1)

    @pl.when(f == 0)
    def _():
        x = x_ref[...]
        h_ref[...] = _rms_rows(x, g_ref[...]).astype(BF16)
        o_ref[...] = x

    h = h_ref[...]
    gate = jnp.dot(h, wg_ref[...], preferred_element_type=F32)
    up = jnp.dot(h, wu_ref[...], preferred_element_type=F32)
    act = (0.5 * gate * jax.nn.sigmoid(gate) * up).astype(BF16)
    o_ref[...] += jnp.dot(act, wd_ref[...], preferred_element_type=F32)


def _ffn(x, gain, wg, wu, wd, *, tm, tf):
    t, d = x.shape
    fp = wg.shape[1]
    return pl.pallas_call(
        _ffn_kernel,
        grid=(t // tm, fp // tf),
        in_specs=[pl.BlockSpec((tm, d), lambda i, f: (i, 0)),
                  pl.BlockSpec((1, d), lambda i, f: (0, 0)),
                  pl.BlockSpec((d, tf), lambda i, f: (0, f)),
                  pl.BlockSpec((d, tf), lambda i, f: (0, f)),
                  pl.BlockSpec((tf, d), lambda i, f: (f, 0))],
        out_specs=pl.BlockSpec((tm, d), lambda i, f: (i, 0)),
        out_shape=jax.ShapeDtypeStruct((t, d), F32),
        scratch_shapes=[pltpu.VMEM((tm, d), BF16)],
        compiler_params=_cparams(2),
        name="ffn",
    )(x, gain, wg, wu, wd)


def _inproj_kernel(x_ref, g_ref, w_ref, o_ref, h_ref):
    @pl.when(pl.program_id(1) == 0)
    def _():
        h_ref[...] = _rms_rows(x_ref[...], g_ref[...]).astype(BF16)

    o_ref[...] = jnp.dot(h_ref[...], w_ref[...], preferred_element_type=F32)


def _inproj(x, gain, w, *, tm, tn):
    t, d = x.shape
    n = w.shape[1]
    return pl.pallas_call(
        _inproj_kernel,
        grid=(t // tm, n // tn),
        in_specs=[pl.BlockSpec((tm, d), lambda i, j: (i, 0)),
                  pl.BlockSpec((1, d), lambda i, j: (0, 0)),
                  pl.BlockSpec((d, tn), lambda i, j: (0, j))],
        out_specs=pl.BlockSpec((tm, tn), lambda i, j: (i, j)),
        out_shape=jax.ShapeDtypeStruct((t, n), F32),
        scratch_shapes=[pltpu.VMEM((tm, d), BF16)],
        compiler_params=_cparams(2),
        name="inproj",
    )(x, gain, w)


def _seg_sum(x, first):
    s0 = jnp.sum(jnp.where(first, x, 0.0), axis=1, keepdims=True)
    s1 = jnp.sum(jnp.where(first, 0.0, x), axis=1, keepdims=True)
    return jnp.where(first, s0, s1)


def _bd(x):
    w = x.shape[1]
    first = lax.broadcasted_iota(jnp.int32, (1, w), 1) < (w // 2)
    zero = jnp.zeros_like(x)
    return jnp.concatenate([jnp.where(first, x, zero), jnp.where(first, zero, x)], axis=0)


def _diag_blocks(m):
    n = m.shape[0] // 2
    first = lax.broadcasted_iota(jnp.int32, (1, 2 * n), 1) < n
    return jnp.where(first, m[0:n], m[n:2 * n])


def _rwkv_kernel(r_ref, k_ref, v_ref, l_ref, rp_ref, kp_ref, vp_ref, lp_ref,
                 sr_ref, sk_ref, sv_ref, sl_ref, vec_ref, mul_ref, wl1_ref, wl2_ref, s0_ref,
                 y_ref, sout_ref, z_ref, *, sb, tb, c):
    b = pl.program_id(2)
    nc = tb // c
    hd = HEAD_DIM

    vec = vec_ref[...]
    mu_r, mu_k, mu_v = vec[0:1], vec[1:2], vec[2:3]
    w0, a0, k_k, k_a, r_k = vec[3:4], vec[4:5], vec[5:6], vec[6:7], vec[7:8]
    gn_g, gn_b = vec[8:9], vec[9:10]
    mu_l = mul_ref[...]

    lane = lax.broadcasted_iota(jnp.int32, (1, LANES), 1)
    first = lane < hd
    row_t = lax.broadcasted_iota(jnp.int32, (tb, 1), 0)
    ri = lax.broadcasted_iota(jnp.int32, (c, 2 * c), 0)
    ci = lax.broadcasted_iota(jnp.int32, (c, 2 * c), 1) % c
    lower_incl = ri >= ci
    lower_strict = ri > ci
    eye_p = jnp.where(ri == ci, 1.0, 0.0).astype(F32)
    rt_ = lax.broadcasted_iota(jnp.int32, (c, c), 0)
    ct_ = lax.broadcasted_iota(jnp.int32, (c, c), 1)
    tri = jnp.where(rt_ >= ct_, 1.0, 0.0).astype(BF16)
    rj = lax.broadcasted_iota(jnp.int32, (hd, LANES), 0)
    cj = lax.broadcasted_iota(jnp.int32, (hd, LANES), 1) % hd
    eye_h = rj == cj

    @pl.when(b == 0)
    def _():
        for s in range(sb):
            z_ref[s] = jnp.concatenate([s0_ref[s, 0].T, s0_ref[s, 1].T], axis=1)

    def shifted(ref, pref, sref, s, mu):
        raw = ref[s]
        before = jnp.where(b == 0, sref[s], pref[s][SUBLANES - 1:SUBLANES, :])
        prev = jnp.where(row_t == 0, before, pltpu.roll(raw, 1, 0))
        return raw + (prev - raw) * mu

    pre = []
    for s in range(sb):
        rx = shifted(r_ref, rp_ref, sr_ref, s, mu_r)
        kx = shifted(k_ref, kp_ref, sk_ref, s, mu_k)
        vx = shifted(v_ref, vp_ref, sv_ref, s, mu_v)
        lx = shifted(l_ref, lp_ref, sl_ref, s, mu_l)
        l01 = lx[:, 0:LANES]
        z01 = jnp.where(first, jnp.tanh(l01), l01)
        wa = _dot(z01, wl1_ref[0])
        w_pre = wa[:, 0:LANES] + w0
        a_pre = wa[:, LANES:2 * LANES] + a0
        gate = _dot(jax.nn.sigmoid(lx[:, _LORA_G0:_LORA_G0 + _LORA_GW]), wl2_ref[0])
        softplus = jnp.maximum(-w_pre, 0.0) + jnp.log1p(jnp.exp(-jnp.abs(w_pre)))
        logd = -jnp.exp(-softplus - 0.5)
        a = jax.nn.sigmoid(a_pre)
        kxk = kx * k_k
        kk = kxk / jnp.maximum(jnp.sqrt(_seg_sum(kxk * kxk, first)), 1e-12)
        kp = kx * (1.0 + (a - 1.0) * k_a)
        pre.append(dict(r=rx, kk=kk, kp=kp, bb=kk * a, v=vx, logd=logd, gate=gate,
                        bonus=_seg_sum(rx * kp * r_k, first) * vx))

    units = [(s, ch) for s in range(sb) for ch in range(nc)]

    def rows(name, u):
        s, ch = units[u]
        return pre[s][name][ch * c:(ch + 1) * c]

    nu = len(units)
    cin = []
    for u in range(nu):
        hi, mid, lo = _split3(rows("logd", u))
        cs = jnp.dot(tri, jnp.concatenate([hi, mid, lo], axis=1), preferred_element_type=F32)
        cin.append(cs[:, 0:LANES] + cs[:, LANES:2 * LANES] + cs[:, 2 * LANES:3 * LANES])
    kk_t, r_t, k_d, b_d, p_end, g = [], [], [], [], [], []
    for u in range(nu):
        ld = rows("logd", u)
        c_end = cin[u][c - 1:c, :]
        e_inv = jnp.exp(-cin[u])
        e_dec = jnp.exp(c_end - cin[u])
        kk_t.append((rows("kk", u) * jnp.exp(cin[u] - ld)).astype(BF16))
        r_t.append(rows("r", u) * jnp.exp(cin[u]))
        k_d.append((rows("kp", u) * e_dec).astype(BF16))
        b_d.append((rows("bb", u) * e_dec).astype(BF16))
        p_end.append(jnp.exp(c_end))
        lhs = jnp.concatenate([kk_t[u], r_t[u].astype(BF16)], axis=0)
        rhs = jnp.concatenate([_bd((rows("bb", u) * e_inv).astype(BF16)),
                               _bd((rows("kp", u) * e_inv).astype(BF16))], axis=0)
        g.append(_dot_nt(lhs, rhs))
    m_ab = [jnp.where(lower_strict, g[u][0:c, 0:2 * c], 0.0).astype(BF16) for u in range(nu)]
    a_rb = [jnp.where(lower_incl, g[u][c:2 * c, 0:2 * c], 0.0).astype(BF16) for u in range(nu)]
    m_ak = [jnp.where(lower_strict, g[u][0:c, 2 * c:4 * c], 0.0).astype(BF16) for u in range(nu)]
    a_rk = [jnp.where(lower_incl, g[u][c:2 * c, 2 * c:4 * c], 0.0).astype(BF16) for u in range(nu)]
    v_bd = [_bd(rows("v", u).astype(BF16)) for u in range(nu)]
    x = [eye_p - m_ab[u].astype(F32) for u in range(nu)]
    p = [_dot(m_ab[u], _bd(m_ab[u])) for u in range(nu)]
    k = 2
    while 2 * k < c:
        both = [_dot(jnp.concatenate([x[u].astype(BF16), p[u].astype(BF16)], axis=0), _bd(p[u].astype(BF16)))
                for u in range(nu)]
        x = [x[u] + both[u][0:c] for u in range(nu)]
        p = [both[u][c:2 * c] for u in range(nu)]
        k *= 2
    t_inv = [(x[u] + _dot(x[u], _bd(p[u].astype(BF16)))).astype(BF16) for u in range(nu)]
    mv = [_dot(m_ak[u], v_bd[u]) for u in range(nu)]
    tw = [_dot(t_inv[u], jnp.concatenate([_bd(kk_t[u]), _bd(mv[u].astype(BF16))], axis=1)).astype(BF16)
          for u in range(nu)]
    bt = [_dot_tn(b_d[u], tw[u]) for u in range(nu)]
    kv = [_dot_tn(k_d[u], rows("v", u)) for u in range(nu)]
    ab = [_dot(a_rb[u], jnp.concatenate([_bd(tw[u][:, 0:LANES]), _bd(tw[u][:, LANES:2 * LANES])], axis=1))
          for u in range(nu)]
    av = [_dot(a_rk[u], v_bd[u]) for u in range(nu)]
    lhs_z = []
    u_c, y_i = [], []
    for u in range(nu):
        a_c = jnp.where(eye_h, p_end[u], 0.0) - _diag_blocks(bt[u][:, 0:LANES])
        q_e = r_t[u] - ab[u][:, 0:LANES]
        lhs_z.append(jnp.concatenate([q_e, a_c], axis=0).astype(BF16))
        u_c.append(_diag_blocks(kv[u]) - _diag_blocks(bt[u][:, LANES:2 * LANES]))
        y_i.append(av[u] - ab[u][:, LANES:2 * LANES])
    z = [z_ref[s] for s in range(sb)]
    for u in range(nu):
        s, ch = units[u]
        both = _dot(lhs_z[u], _bd(z[s].astype(BF16)))
        y = both[0:c] + y_i[u]
        z[s] = both[c:c + hd] + u_c[u]
        mean = _seg_sum(y, first) * (1.0 / hd)
        yc = y - mean
        var = _seg_sum(yc * yc, first) * (1.0 / hd)
        yn = yc * lax.rsqrt(var + GN_EPS)
        out = (yn * gn_g + gn_b + rows("bonus", u)) * rows("gate", u)
        y_ref[s, ch * c:(ch + 1) * c, :] = out.astype(y_ref.dtype)
    for s in range(sb):
        z_ref[s] = z[s]
        sout_ref[s, 0] = z[s][:, 0:hd].T
        sout_ref[s, 1] = z[s][:, hd:2 * hd].T


def _rwkv(p3, shiftp, s0, vec, mu_l, wl1, wl2, *, sb, tb, c):
    s, t, _ = p3.shape
    kb, vb = _COL_K // LANES, _COL_V // LANES
    lb = _COL_LORA // _LORA_W
    tpb = tb // SUBLANES

    def cur(col0):
        return pl.BlockSpec((sb, tb, LANES), lambda i, p, b: (i, b, col0 + p))

    def prv(col0):
        return pl.BlockSpec((sb, SUBLANES, LANES), lambda i, p, b: (i, jnp.maximum(b * tpb - 1, 0), col0 + p))

    def shf(col0):
        return pl.BlockSpec((sb, 1, LANES), lambda i, p, b: (i, 0, col0 + p))

    in_specs = [
        cur(0), cur(kb), cur(vb),
        pl.BlockSpec((sb, tb, _LORA_W), lambda i, p, b: (i, b, lb)),
        prv(0), prv(kb), prv(vb),
        pl.BlockSpec((sb, SUBLANES, _LORA_W), lambda i, p, b: (i, jnp.maximum(b * tpb - 1, 0), lb)),
        shf(0), shf(kb), shf(vb),
        pl.BlockSpec((sb, 1, _LORA_W), lambda i, p, b: (i, 0, 3 * A_WIDTH // _LORA_W)),
        pl.BlockSpec((_VEC_ROWS, LANES), lambda i, p, b: (0, p)),
        pl.BlockSpec((1, _LORA_W), lambda i, p, b: (0, 0)),
        pl.BlockSpec((1, LANES, 2 * LANES), lambda i, p, b: (p, 0, 0)),
        pl.BlockSpec((1, _LORA_GW, LANES), lambda i, p, b: (p, 0, 0)),
        pl.BlockSpec((sb, 2, HEAD_DIM, HEAD_DIM), lambda i, p, b: (i, p, 0, 0)),
    ]
    out_specs = [
        pl.BlockSpec((sb, tb, LANES), lambda i, p, b: (i, b, p)),
        pl.BlockSpec((sb, 2, HEAD_DIM, HEAD_DIM), lambda i, p, b: (i, p, 0, 0)),
    ]
    return pl.pallas_call(
        functools.partial(_rwkv_kernel, sb=sb, tb=tb, c=c),
        grid=(s // sb, PAIRS, t // tb),
        in_specs=in_specs,
        out_specs=out_specs,
        out_shape=[jax.ShapeDtypeStruct((s, t, A_WIDTH), BF16),
                   jax.ShapeDtypeStruct((s, A_HEADS, HEAD_DIM, HEAD_DIM), F32)],
        scratch_shapes=[pltpu.VMEM((sb, HEAD_DIM, LANES), F32)],
        compiler_params=_cparams(3),
        name="rwkv",
    )(p3, p3, p3, p3, p3, p3, p3, p3, shiftp, shiftp, shiftp, shiftp, vec, mu_l, wl1, wl2, s0)


def _norm_rope(x, gain, cos, sin_signed):
    lane = lax.broadcasted_iota(jnp.int32, (1, LANES), 1)
    first = lane < HEAD_DIM
    low_half = (lane % HEAD_DIM) < (HEAD_DIM // 2)
    parts = []
    for j in range(x.shape[1] // LANES):
        xs = x[:, j * LANES:(j + 1) * LANES]
        ms = _seg_sum(xs * xs, first) * (1.0 / HEAD_DIM)
        xn = xs * lax.rsqrt(ms + RMS_EPS) * gain
        partner = jnp.where(low_half, pltpu.roll(xn, LANES - HEAD_DIM // 2, 1),
                            pltpu.roll(xn, HEAD_DIM // 2, 1))
        parts.append(xn * cos + partner * sin_signed)
    return jnp.concatenate(parts, axis=1) if len(parts) > 1 else parts[0]


def _sink_attend(q, k_all, v_all, sink, mask):
    s = _dot_nt(q, k_all) * (HEAD_DIM ** -0.5)
    if mask is not None:
        s = jnp.where(mask, s, -1e30)
    m = jnp.maximum(jnp.max(s, axis=-1, keepdims=True), sink)
    p = jnp.exp(s - m)
    denom = jnp.sum(p, axis=-1, keepdims=True) + jnp.exp(sink - m)
    return _dot(p / denom, v_all)


def _band_kernel(sink_ref, q_ref, kv_ref, kvp_ref, cq_ref, sq_ref, cp_ref, sp_ref, gq_ref, gk_ref,
                 y_ref, ko_ref, *, tq):
    i = pl.program_id(0)
    hd = HEAD_DIM
    q = _norm_rope(q_ref[...], gq_ref[...], cq_ref[...], sq_ref[...])
    kv = kv_ref[...]
    kvp = kvp_ref[...]
    k_cur = _norm_rope(kv[:, 0:B_KV], gk_ref[...], cq_ref[...], sq_ref[...])
    k_prev = _norm_rope(kvp[:, 0:B_KV], gk_ref[...], cp_ref[...], sp_ref[...])
    ko_ref[...] = k_cur
    v_cur = kv[:, B_KV:2 * B_KV]
    v_prev = kvp[:, B_KV:2 * B_KV]

    nk = WINDOW + tq
    qc = lax.broadcasted_iota(jnp.int32, (tq, nk), 0) // CHUNK
    col = lax.broadcasted_iota(jnp.int32, (tq, nk), 1)
    rel = col // CHUNK - qc
    mask = (rel >= 0) & (rel <= WINDOW // CHUNK) & ((col >= WINDOW) | (i > 0))

    outs = []
    for g in range(B_KV_HEADS):
        gs = slice(g * hd, (g + 1) * hd)
        k_all = jnp.concatenate([k_prev[:, gs], k_cur[:, gs]], axis=0)
        v_all = jnp.concatenate([v_prev[:, gs], v_cur[:, gs]], axis=0)
        for j in range(B_GROUP):
            h = g * B_GROUP + j
            outs.append(_sink_attend(q[:, h * hd:(h + 1) * hd], k_all, v_all, sink_ref[h], mask))
    y_ref[...] = jnp.concatenate(outs, axis=1).astype(y_ref.dtype)


def _attn_band(p2, cos, sin_signed, gq, gk, sinks, *, tq):
    t = p2.shape[0]
    qb, kvb = _COL_Q // B_Q, _COL_KV // _KV_W
    wpb = tq // WINDOW

    def prev_idx(i):
        return jnp.maximum(i * wpb - 1, 0)

    return pl.pallas_call(
        functools.partial(_band_kernel, tq=tq),
        grid=(t // tq,),
        in_specs=[pl.BlockSpec(memory_space=pltpu.SMEM),
                  pl.BlockSpec((tq, B_Q), lambda i: (i, qb)),
                  pl.BlockSpec((tq, _KV_W), lambda i: (i, kvb)),
                  pl.BlockSpec((WINDOW, _KV_W), lambda i: (prev_idx(i), kvb)),
                  pl.BlockSpec((tq, LANES), lambda i: (i, 0)),
                  pl.BlockSpec((tq, LANES), lambda i: (i, 0)),
                  pl.BlockSpec((WINDOW, LANES), lambda i: (prev_idx(i), 0)),
                  pl.BlockSpec((WINDOW, LANES), lambda i: (prev_idx(i), 0)),
                  pl.BlockSpec((1, LANES), lambda i: (0, 0)),
                  pl.BlockSpec((1, LANES), lambda i: (0, 0))],
        out_specs=[pl.BlockSpec((tq, B_Q), lambda i: (i, 0)),
                   pl.BlockSpec((tq, B_KV), lambda i: (i, 0))],
        out_shape=[jax.ShapeDtypeStruct((t, B_Q), BF16),
                   jax.ShapeDtypeStruct((t, B_KV), F32)],
        compiler_params=_cparams(1),
        name="attn_band",
    )(sinks, p2, p2, p2, cos, sin_signed, cos, sin_signed, gq, gk)


def _cached_kernel(sink_ref, q_ref, kv_ref, ck_ref, cv_ref, c_ref, s_ref, gq_ref, gk_ref,
                   y_ref, ko_ref, *, sb):
    hd = HEAD_DIM
    for s in range(sb):
        q = _norm_rope(q_ref[s], gq_ref[...], c_ref[...], s_ref[...])
        kv = kv_ref[s]
        k_new = _norm_rope(kv[:, 0:B_KV], gk_ref[...], c_ref[...], s_ref[...])
        ko_ref[s] = k_new
        v_new = kv[:, B_KV:2 * B_KV]
        ck = ck_ref[s]
        cv = cv_ref[s]
        outs = []
        for g in range(B_KV_HEADS):
            gs = slice(g * hd, (g + 1) * hd)
            k_all = jnp.concatenate([ck[:, gs], k_new[:, gs]], axis=0)
            v_all = jnp.concatenate([cv[:, gs], v_new[:, gs]], axis=0)
            for j in range(B_GROUP):
                h = g * B_GROUP + j
                outs.append(_sink_attend(q[:, h * hd:(h + 1) * hd], k_all, v_all, sink_ref[h], None))
        y_ref[s] =jnp.concatenate(outs, axis=1).astype(y_ref.dtype)


def _attn_cached(p3, ck, cv, cos, sin_signed, gq, gk, sinks, *, sb):
    s, t, _ = p3.shape
    w = ck.shape[1]
    qb, kvb = _COL_Q // B_Q, _COL_KV // _KV_W
    return pl.pallas_call(
        functools.partial(_cached_kernel, sb=sb),
        grid=(s // sb,),
        in_specs=[pl.BlockSpec(memory_space=pltpu.SMEM),
                  pl.BlockSpec((sb, t, B_Q), lambda i: (i, 0, qb)),
                  pl.BlockSpec((sb, t, _KV_W), lambda i: (i, 0, kvb)),
                  pl.BlockSpec((sb, w, B_KV), lambda i: (i, 0, 0)),
                  pl.BlockSpec((sb, w, B_KV), lambda i: (i, 0, 0)),
                  pl.BlockSpec((t, LANES), lambda i: (0, 0)),
                  pl.BlockSpec((t, LANES), lambda i: (0, 0)),
                  pl.BlockSpec((1, LANES), lambda i: (0, 0)),
                  pl.BlockSpec((1, LANES), lambda i: (0, 0))],
        out_specs=[pl.BlockSpec((sb, t, B_Q), lambda i: (i, 0, 0)),
                   pl.BlockSpec((sb, t, B_KV), lambda i: (i, 0, 0))],
        out_shape=[jax.ShapeDtypeStruct((s, t, B_Q), BF16),
                   jax.ShapeDtypeStruct((s, t, B_KV), F32)],
        compiler_params=_cparams(1),
        name="attn_cached",
    )(sinks, p3, p3, ck, cv, cos, sin_signed, gq, gk)


def _outproj_kernel(x_ref, ya_ref, yb_ref, ga_ref, gb_ref, wa_ref, wb_ref, wo_ref, o_ref, m_ref):
    @pl.when(pl.program_id(1) == 0)
    def _():
        ua = jnp.dot(ya_ref[...], wa_ref[...], preferred_element_type=F32)
        ub = jnp.dot(yb_ref[...], wb_ref[...], preferred_element_type=F32)
        merged = jax.nn.sigmoid(ga_ref[...]) * ua + jax.nn.sigmoid(gb_ref[...]) * ub
        m_ref[...] = merged.astype(BF16)

    o_ref[...] = x_ref[...] + jnp.dot(m_ref[...], wo_ref[...], preferred_element_type=F32)


def _outproj(x, ya, yb, p2, wa, wb, wo, *, tm, tn):
    t, d = x.shape
    gab, gbb = _COL_GA // d, _COL_GB // d
    return pl.pallas_call(
        _outproj_kernel,
        grid=(t // tm, d // tn),
        in_specs=[pl.BlockSpec((tm, tn), lambda i, j: (i, j)),
                  pl.BlockSpec((tm, A_WIDTH), lambda i, j: (i, 0)),
                  pl.BlockSpec((tm, B_Q), lambda i, j: (i, 0)),
                  pl.BlockSpec((tm, d), lambda i, j: (i, gab)),
                  pl.BlockSpec((tm, d), lambda i, j: (i, gbb)),
                  pl.BlockSpec((A_WIDTH, d), lambda i, j: (0, 0)),
                  pl.BlockSpec((B_Q, d), lambda i, j: (0, 0)),
                  pl.BlockSpec((d, tn), lambda i, j: (0, j))],
        out_specs=pl.BlockSpec((tm, tn), lambda i, j: (i, j)),
        out_shape=jax.ShapeDtypeStruct((t, d), F32),
        scratch_shapes=[pltpu.VMEM((tm, d), BF16)],
        compiler_params=_cparams(2),
        name="outproj",
    )(x, ya, yb, p2, p2, wa, wb, wo)


def _a_pieces(a):
    o = 0
    out = []
    for w in (A_WIDTH, DECAY_LORA, A_WIDTH, A_WIDTH, ICLR_LORA, GATE_LORA):
        out.append(a[..., o:o + w])
        o += w
    return out


def _regroup_a(a, pad_to):
    r, w_lo, k, v, a_lo, g_lo = _a_pieces(a)
    lora = jnp.concatenate([w_lo, a_lo, g_lo], axis=-1)
    lora = jnp.pad(lora, [(0, 0)] * (a.ndim - 1) + [(0, pad_to - lora.shape[-1])])
    return r, k, v, lora


def _regroup_w_in(w):
    wa = w[:, :A_COLS]
    o = A_COLS
    wq = w[:, o:o + B_Q]
    wk = w[:, o + B_Q:o + B_Q + B_KV]
    wv = w[:, o + B_Q + B_KV:o + B_Q + 2 * B_KV]
    o += B_Q + 2 * B_KV
    wga = w[:, o:o + D_MODEL]
    wgb = w[:, o + D_MODEL:o + 2 * D_MODEL]
    r, k, v, lora = _regroup_a(wa, _LORA_W)
    return jnp.concatenate([r, k, v, wq, wga, wgb, lora, wk, wv], axis=1).astype(BF16)


def _shift_row_to_a(p_last):
    r = p_last[..., _COL_R:_COL_R + A_WIDTH]
    k = p_last[..., _COL_K:_COL_K + A_WIDTH]
    v = p_last[..., _COL_V:_COL_V + A_WIDTH]
    o = _COL_LORA
    w_lo = p_last[..., o:o + DECAY_LORA]
    a_lo = p_last[..., o + DECAY_LORA:o + DECAY_LORA + ICLR_LORA]
    g_lo = p_last[..., o + DECAY_LORA + ICLR_LORA:o + DECAY_LORA + ICLR_LORA + GATE_LORA]
    return jnp.concatenate([r, w_lo, k, v, a_lo, g_lo], axis=-1)


def _lora_weights(decay_w2, iclr_a2, gate_g2):
    dw = decay_w2.reshape(DECAY_LORA, PAIRS, LANES)
    ia = iclr_a2.reshape(ICLR_LORA, PAIRS, LANES)
    top = jnp.concatenate([dw, jnp.zeros_like(dw)], axis=2)
    bot = jnp.concatenate([jnp.zeros_like(ia), ia], axis=2)
    wl1 = jnp.concatenate([top, bot], axis=0).transpose(1, 0, 2)
    gg = gate_g2.reshape(GATE_LORA, PAIRS, LANES)
    gg = jnp.pad(gg, ((0, _LORA_GW - GATE_LORA), (0, 0), (0, 0)))
    wl2 = gg.transpose(1, 0, 2)
    return wl1.astype(BF16), wl2.astype(BF16)


def _rope_tables(pos):
    half = HEAD_DIM // 2
    inv = ROPE_THETA ** (-jnp.arange(half, dtype=F32) / half)
    ang = pos.astype(F32)[:, None] * inv[None, :]
    cos = jnp.cos(ang)
    sin = jnp.sin(ang)
    cos_t = jnp.tile(cos, (1, LANES // half))
    sin_t = jnp.tile(jnp.concatenate([-sin, sin], axis=1), (1, LANES // HEAD_DIM))
    return cos_t, sin_t


def _pad_cols(w, n):
    return jnp.pad(w, [(0, 0)] * (w.ndim - 1) + [(0, n - w.shape[-1])])


_FF_TILE = 512


def kernel(x_prompt, x_sample, cache_k, cache_v, state_wkv, state_shift, norm_ff1, ff1_gate, ff1_up, ff1_down,
           norm_mix, w_in, shift_mu, decay_w0, decay_w2, iclr_a0, iclr_a2, gate_g2, k_k, k_a, r_k, gn_gain,
           gn_bias, q_norm, k_norm, sinks, w_up_a, w_up_b, w_o, norm_ff2, ff2_gate, ff2_up, ff2_down):
    depth = norm_ff1.shape[0]
    bp, tp, d = x_prompt.shape
    bs, ts, _ = x_sample.shape
    assert bp == 1 and d == D_MODEL
    d_ff = ff1_gate.shape[2]
    ffp = -(-d_ff // _FF_TILE) * _FF_TILE

    xp = x_prompt.reshape(tp, d)
    xs = x_sample.reshape(bs * ts, d)

    cos_p, sin_p = _rope_tables(jnp.arange(tp))
    cos_s, sin_s = _rope_tables(PAST_LEN + jnp.arange(ts))
    zero_shift = jnp.zeros((bp, 1, 3 * A_WIDTH + _LORA_W), F32)
    zero_state = jnp.zeros((bp, A_HEADS, HEAD_DIM, HEAD_DIM), F32)

    outs = {k: [] for k in ("p_wkv", "p_shift", "p_k", "p_v", "s_wkv", "s_shift", "s_k", "s_v")}
    for l in range(depth):
        wg1 = _pad_cols(ff1_gate[l].astype(BF16), ffp)
        wu1 = _pad_cols(ff1_up[l].astype(BF16), ffp)
        wd1 = jnp.pad(ff1_down[l].astype(BF16), ((0, ffp - d_ff), (0, 0)))
        wg2 = _pad_cols(ff2_gate[l].astype(BF16), ffp)
        wu2 = _pad_cols(ff2_up[l].astype(BF16), ffp)
        wd2 = jnp.pad(ff2_down[l].astype(BF16), ((0, ffp - d_ff), (0, 0)))
        w_in_l = _regroup_w_in(w_in[l])
        mu_r, mu_k, mu_v, mu_l = _regroup_a(shift_mu[l][None, :], _LORA_W)
        rows = [mu_r, mu_k, mu_v, decay_w0[l][None], iclr_a0[l][None], k_k[l][None], k_a[l][None],
                r_k[l].reshape(1, A_WIDTH), gn_gain[l][None], gn_bias[l][None]]
        vec = jnp.concatenate(rows + [jnp.zeros((_VEC_ROWS - len(rows), A_WIDTH), F32)], axis=0)
        wl1, wl2 = _lora_weights(decay_w2[l], iclr_a2[l], gate_g2[l])
        gq = jnp.tile(q_norm[l], LANES // HEAD_DIM)[None, :]
        gk = jnp.tile(k_norm[l], LANES // HEAD_DIM)[None, :]
        wa = w_up_a[l].astype(BF16)
        wb = w_up_b[l].astype(BF16)
        wo = w_o[l].astype(BF16)
        sr, sk, sv, sl = _regroup_a(state_shift[l], _LORA_W)
        shift_s = jnp.concatenate([sr, sk, sv, sl], axis=-1)[:, None, :]

        g1 = norm_ff1[l][None, :]
        xp = _ffn(xp, g1, wg1, wu1, wd1, tm=512, tf=_FF_TILE)
        xs = _ffn(xs, g1, wg1, wu1, wd1, tm=512, tf=_FF_TILE)

        gm = norm_mix[l][None, :]
        pp = _inproj(xp, gm, w_in_l, tm=512, tn=512)
        ps = _inproj(xs, gm, w_in_l, tm=512, tn=512)
        pp3 = pp.reshape(bp, tp, NP_COLS)
        ps3 = ps.reshape(bs, ts, NP_COLS)

        ya_p, wkv_p = _rwkv(pp3, zero_shift, zero_state, vec, mu_l, wl1, wl2, sb=1, tb=512, c=CHUNK)
        ya_s, wkv_s = _rwkv(ps3, shift_s, state_wkv[l], vec, mu_l, wl1, wl2, sb=8, tb=ts, c=ts)

        yb_p, kr_p = _attn_band(pp, cos_p, sin_p, gq, gk, sinks[l], tq=256)
        ck = cache_k[l].reshape(bs, -1, B_KV)
        cv = cache_v[l].reshape(bs, -1, B_KV)
        yb_s, kr_s = _attn_cached(ps3, ck, cv, cos_s, sin_s, gq, gk, sinks[l], sb=8)

        xp = _outproj(xp, ya_p.reshape(tp, A_WIDTH), yb_p, pp, wa, wb, wo, tm=256, tn=512)
        xs = _outproj(xs, ya_s.reshape(bs * ts, A_WIDTH), yb_s.reshape(bs * ts, B_Q), ps, wa, wb, wo,
                      tm=256, tn=512)

        g2 = norm_ff2[l][None, :]
        xp = _ffn(xp, g2, wg2, wu2, wd2, tm=512, tf=_FF_TILE)
        xs = _ffn(xs, g2, wg2, wu2, wd2, tm=512, tf=_FF_TILE)

        vcol = _COL_KV + B_KV
        outs["p_wkv"].append(wkv_p)
        outs["p_shift"].append(_shift_row_to_a(pp3[:, -1, :]))
        outs["p_k"].append(kr_p[-WINDOW:].reshape(bp, WINDOW, B_KV_HEADS, HEAD_DIM))
        outs["p_v"].append(pp3[:, -WINDOW:, vcol:vcol + B_KV].reshape(bp, WINDOW, B_KV_HEADS, HEAD_DIM))
        outs["s_wkv"].append(wkv_s)
        outs["s_shift"].append(_shift_row_to_a(ps3[:, -1, :]))
        outs["s_k"].append(kr_s.reshape(bs, ts, B_KV_HEADS, HEAD_DIM))
        outs["s_v"].append(ps3[:, :, vcol:vcol + B_KV].reshape(bs, ts, B_KV_HEADS, HEAD_DIM))

    return (xp.reshape(bp, tp, d), xs.reshape(bs, ts, d),
            jnp.stack(outs["p_wkv"]), jnp.stack(outs["p_shift"]), jnp.stack(outs["p_k"]), jnp.stack(outs["p_v"]),
            jnp.stack(outs["s_wkv"]), jnp.stack(outs["s_shift"]), jnp.stack(outs["s_k"]), jnp.stack(outs["s_v"]))
```

```python
import functools

import jax
import jax.numpy as jnp
from jax import lax
from jax.experimental import pallas as pl
from jax.experimental.pallas import tpu as pltpu

F32 = jnp.float32
BF16 = jnp.bfloat16

HEAD_DIM = 64
A_WIDTH = 1024
A_HEADS = A_WIDTH // HEAD_DIM
DECAY_LORA = 64
ICLR_LORA = 64
GATE_LORA = 160
GN_EPS = 64e-5
RMS_EPS = 1e-6
B_HEADS = 16
B_KV_HEADS = 4
B_GROUP = B_HEADS // B_KV_HEADS
B_Q = B_HEADS * HEAD_DIM
B_KV = B_KV_HEADS * HEAD_DIM
CHUNK = 64
WINDOW = 128
ROPE_THETA = 10000.0
PAST_LEN = 1024
D_MODEL = 2048
A_COLS = 3 * A_WIDTH + DECAY_LORA + ICLR_LORA + GATE_LORA

LANES = 128
BF16_ROWS = 16
VMEM_LIMIT_BYTES = 56 * 1024 * 1024

_COL_R = 0
_COL_K = _COL_R + A_WIDTH
_COL_V = _COL_K + A_WIDTH
_RKV_W = 3 * A_WIDTH
_COL_Q = _COL_V + A_WIDTH
_COL_GA = _COL_Q + B_Q
_COL_GB = _COL_GA + D_MODEL
_COL_LORA = _COL_GB + D_MODEL
_LORA_W = 512
_COL_KV = _COL_LORA + _LORA_W
_KV_W = 2 * B_KV
NP_COLS = _COL_KV + _KV_W
_LORA_G0 = LANES
_LORA_GW = 2 * LANES
PAIRS = A_HEADS // 2
_VEC_ROWS = 16


def _cparams(n_axes):
    return pltpu.CompilerParams(dimension_semantics=("arbitrary",) * n_axes,
                                vmem_limit_bytes=VMEM_LIMIT_BYTES)


def _dot(a, b):
    return jnp.dot(a.astype(BF16), b.astype(BF16), preferred_element_type=F32)


def _dot_nt(a, b):
    return lax.dot_general(a.astype(BF16), b.astype(BF16), (((1,), (1,)), ((), ())),
                           preferred_element_type=F32)


def _dot_tn(a, b):
    return lax.dot_general(a.astype(BF16), b.astype(BF16), (((0,), (0,)), ((), ())),
                           preferred_element_type=F32)


def _split3(x):
    hi = x.astype(BF16)
    r1 = x - hi.astype(F32)
    mid = r1.astype(BF16)
    lo = (r1 - mid.astype(F32)).astype(BF16)
    return hi, mid, lo


def _rms_rows(x, gain):
    ms = jnp.mean(x * x, axis=-1, keepdims=True)
    return x * lax.rsqrt(ms + RMS_EPS) * gain


def _resident(shape, index_map):
    return pl.BlockSpec(shape, index_map, pipeline_mode=pl.Buffered(1))


def _ffn_kernel(x_ref, g_ref, wg_ref, wu_ref, wd_ref, o_ref, h_ref):
    f = pl.program_id(1)

    @pl.when(f == 0)
    def _():
        x = x_ref[...]
        h_ref[...] = _rms_rows(x, g_ref[...]).astype(BF16)
        o_ref[...] = x

    h = h_ref[...]
    gate = jnp.dot(h, wg_ref[...], preferred_element_type=F32)
    up = jnp.dot(h, wu_ref[...], preferred_element_type=F32)
    act = (0.5 * gate * jax.nn.sigmoid(gate) * up).astype(BF16)
    o_ref[...] += jnp.dot(act, wd_ref[...], preferred_element_type=F32)


def _ffn(x, gain, wg, wu, wd, layer, *, tm, tf):
    t, d = x.shape
    fp = wg.shape[2]
    return pl.pallas_call(
        _ffn_kernel,
        grid=(t // tm, fp // tf),
        in_specs=[pl.BlockSpec((tm, d), lambda i, f: (i, 0)),
                  pl.BlockSpec((None, 1, d), lambda i, f: (layer, 0, 0)),
                  pl.BlockSpec((None, d, tf), lambda i, f: (layer, 0, f)),
                  pl.BlockSpec((None, d, tf), lambda i, f: (layer, 0, f)),
                  pl.BlockSpec((None, tf, d), lambda i, f: (layer, f, 0))],
        out_specs=pl.BlockSpec((tm, d), lambda i, f: (i, 0)),
        out_shape=jax.ShapeDtypeStruct((t, d), F32),
        scratch_shapes=[pltpu.VMEM((tm, d), BF16)],
        compiler_params=_cparams(2),
        name="ffn",
    )(x, gain, wg, wu, wd)


def _inproj_kernel(x_ref, g_ref, w_ref, o_ref, h_ref):
    @pl.when(pl.program_id(1) == 0)
    def _():
        h_ref[...] = _rms_rows(x_ref[...], g_ref[...]).astype(BF16)

    o_ref[...] = jnp.dot(h_ref[...], w_ref[...], preferred_element_type=F32).astype(o_ref.dtype)


def _inproj(x, gain, w, layer, *, tm, tn):
    t, d = x.shape
    n = w.shape[2]
    return pl.pallas_call(
        _inproj_kernel,
        grid=(t // tm, n // tn),
        in_specs=[pl.BlockSpec((tm, d), lambda i, j: (i, 0)),
                  pl.BlockSpec((None, 1, d), lambda i, j: (layer, 0, 0)),
                  pl.BlockSpec((None, d, tn), lambda i, j: (layer, 0, j))],
        out_specs=pl.BlockSpec((tm, tn), lambda i, j: (i, j)),
        out_shape=jax.ShapeDtypeStruct((t, n), BF16),
        scratch_shapes=[pltpu.VMEM((tm, d), BF16)],
        compiler_params=_cparams(2),
        name="inproj",
    )(x, gain, w)


def _seg_sum(x, first):
    s0 = jnp.sum(jnp.where(first, x, 0.0), axis=1, keepdims=True)
    s1 = jnp.sum(jnp.where(first, 0.0, x), axis=1, keepdims=True)
    return jnp.where(first, s0, s1)


def _bd(x):
    w = x.shape[1]
    first = lax.broadcasted_iota(jnp.int32, (1, w), 1) < (w // 2)
    zero = jnp.zeros_like(x)
    return jnp.concatenate([jnp.where(first, x, zero), jnp.where(first, zero, x)], axis=0)


def _diag_blocks(m):
    n = m.shape[0] // 2
    first = lax.broadcasted_iota(jnp.int32, (1, 2 * n), 1) < n
    return jnp.where(first, m[0:n], m[n:2 * n])


def _rwkv_kernel(rkv_ref, l_ref, rkvp_ref, lp_ref, srkv_ref, sl_ref, vec_ref, mul_ref, wl1_ref, wl2_ref,
                 s0_ref, y_ref, sout_ref, z_ref, *, sb, tb, c, nb):
    b = pl.program_id(1)
    nc = tb // c
    hd = HEAD_DIM

    vec = vec_ref[...]
    mu_rkv = jnp.concatenate([vec[0:1], vec[1:2], vec[2:3]], axis=1)
    mu_l = mul_ref[...]

    lane = lax.broadcasted_iota(jnp.int32, (1, LANES), 1)
    first = lane < hd
    row_t = lax.broadcasted_iota(jnp.int32, (tb, 1), 0)
    ri = lax.broadcasted_iota(jnp.int32, (c, 2 * c), 0)
    ci = lax.broadcasted_iota(jnp.int32, (c, 2 * c), 1) % c
    lower_incl = ri >= ci
    lower_strict = ri > ci
    eye_p = jnp.where(ri == ci, 1.0, 0.0).astype(F32)
    rt_ = lax.broadcasted_iota(jnp.int32, (c, c), 0)
    ct_ = lax.broadcasted_iota(jnp.int32, (c, c), 1)
    tri = jnp.where(rt_ >= ct_, 1.0, 0.0).astype(BF16)
    rj = lax.broadcasted_iota(jnp.int32, (hd, LANES), 0)
    cj = lax.broadcasted_iota(jnp.int32, (hd, LANES), 1) % hd
    eye_h = rj == cj

    @pl.when(b == 0)
    def _():
        for s in range(sb):
            for p in range(PAIRS):
                z_ref[s, p] = jnp.concatenate([s0_ref[s, 2 * p].T, s0_ref[s, 2 * p + 1].T], axis=1)

    def shifted(ref, pref, sref, s, mu):
        raw = ref[s].astype(F32)
        before = jnp.where(b == 0, sref[s], pref[s][BF16_ROWS - 1:BF16_ROWS, :].astype(F32))
        prev = jnp.where(row_t == 0, before, pltpu.roll(raw, 1, 0))
        return raw + (prev - raw) * mu

    pre = {}
    for s in range(sb):
        x3 = shifted(rkv_ref, rkvp_ref, srkv_ref, s, mu_rkv)
        lx = shifted(l_ref, lp_ref, sl_ref, s, mu_l)
        l01 = lx[:, 0:LANES]
        z01 = jnp.where(first, jnp.tanh(l01), l01)
        wa = _dot(z01, wl1_ref[...])
        gate = _dot(jax.nn.sigmoid(lx[:, _LORA_G0:_LORA_G0 + _LORA_GW]), wl2_ref[...])
        for p in range(PAIRS):
            ps = slice(p * LANES, (p + 1) * LANES)
            w0, a0, k_k, k_a, r_k = vec[3:4, ps], vec[4:5, ps], vec[5:6, ps], vec[6:7, ps], vec[7:8, ps]
            rx = x3[:, _COL_R + p * LANES:_COL_R + (p + 1) * LANES]
            kx = x3[:, _COL_K + p * LANES:_COL_K + (p + 1) * LANES]
            vx = x3[:, _COL_V + p * LANES:_COL_V + (p + 1) * LANES]
            w_pre = wa[:, ps] + w0
            a_pre = wa[:, A_WIDTH + p * LANES:A_WIDTH + (p + 1) * LANES] + a0
            softplus = jnp.maximum(-w_pre, 0.0) + jnp.log1p(jnp.exp(-jnp.abs(w_pre)))
            logd = -jnp.exp(-softplus - 0.5)
            a = jax.nn.sigmoid(a_pre)
            kxk = kx * k_k
            kk = kxk / jnp.maximum(jnp.sqrt(_seg_sum(kxk * kxk, first)), 1e-12)
            kp = kx * (1.0 + (a - 1.0) * k_a)
            pre[(s, p)] = dict(r=rx, kk=kk, kp=kp, bb=kk * a, v=vx, logd=logd, gate=gate[:, ps],
                               bonus=_seg_sum(rx * kp * r_k, first) * vx)

    units = [(s, ch, p) for ch in range(nc) for s in range(sb) for p in range(PAIRS)]

    def rows(name, u):
        s, ch, p = units[u]
        return pre[(s, p)][name][ch * c:(ch + 1) * c]

    nu = len(units)
    cin = []
    for u in range(nu):
        hi, mid, lo = _split3(rows("logd", u))
        cs = jnp.dot(tri, jnp.concatenate([hi, mid, lo], axis=1), preferred_element_type=F32)
        cin.append(cs[:, 0:LANES] + cs[:, LANES:2 * LANES] + cs[:, 2 * LANES:3 * LANES])
    kk_t, r_t, k_d, b_d, p_end, g = [], [], [], [], [], []
    for u in range(nu):
        ld = rows("logd", u)
        c_end = cin[u][c - 1:c, :]
        e_inv = jnp.exp(-cin[u])
        e_dec = jnp.exp(c_end - cin[u])
        kk_t.append((rows("kk", u) * jnp.exp(cin[u] - ld)).astype(BF16))
        r_t.append(rows("r", u) * jnp.exp(cin[u]))
        k_d.append((rows("kp", u) * e_dec).astype(BF16))
        b_d.append((rows("bb", u) * e_dec).astype(BF16))
        p_end.append(jnp.exp(c_end))
        lhs = jnp.concatenate([kk_t[u], r_t[u].astype(BF16)], axis=0)
        rhs = jnp.concatenate([_bd((rows("bb", u) * e_inv).astype(BF16)),
                               _bd((rows("kp", u) * e_inv).astype(BF16))], axis=0)
        g.append(_dot_nt(lhs, rhs))
    m_ab = [jnp.where(lower_strict, g[u][0:c, 0:2 * c], 0.0).astype(BF16) for u in range(nu)]
    a_rb = [jnp.where(lower_incl, g[u][c:2 * c, 0:2 * c], 0.0).astype(BF16) for u in range(nu)]
    m_ak = [jnp.where(lower_strict, g[u][0:c, 2 * c:4 * c], 0.0).astype(BF16) for u in range(nu)]
    a_rk = [jnp.where(lower_incl, g[u][c:2 * c, 2 * c:4 * c], 0.0).astype(BF16) for u in range(nu)]
    v_bd = [_bd(rows("v", u).astype(BF16)) for u in range(nu)]
    x = [eye_p - m_ab[u].astype(F32) for u in range(nu)]
    pw = [_dot(m_ab[u], _bd(m_ab[u])) for u in range(nu)]
    k = 2
    while 2 * k < c:
        both = [_dot(jnp.concatenate([x[u].astype(BF16), pw[u].astype(BF16)], axis=0), _bd(pw[u].astype(BF16)))
                for u in range(nu)]
        x = [x[u] + both[u][0:c] for u in range(nu)]
        pw = [both[u][c:2 * c] for u in range(nu)]
        k *= 2
    t_inv = [(x[u] + _dot(x[u], _bd(pw[u].astype(BF16)))).astype(BF16) for u in range(nu)]
    mv = [_dot(m_ak[u], v_bd[u]) for u in range(nu)]
    tw = [_dot(t_inv[u], jnp.concatenate([_bd(kk_t[u]), _bd(mv[u].astype(BF16))], axis=1)).astype(BF16)
          for u in range(nu)]
    bt = [_dot_tn(b_d[u], tw[u]) for u in range(nu)]
    kv = [_dot_tn(k_d[u], rows("v", u)) for u in range(nu)]
    ab = [_dot(a_rb[u], jnp.concatenate([_bd(tw[u][:, 0:LANES]), _bd(tw[u][:, LANES:2 * LANES])], axis=1))
          for u in range(nu)]
    av = [_dot(a_rk[u], v_bd[u]) for u in range(nu)]
    lhs_z, u_c, y_i = [], [], []
    for u in range(nu):
        a_c = jnp.where(eye_h, p_end[u], 0.0) - _diag_blocks(bt[u][:, 0:LANES])
        q_e = r_t[u] - ab[u][:, 0:LANES]
        lhs_z.append(jnp.concatenate([a_c, q_e], axis=0).astype(BF16))
        u_c.append(_diag_blocks(kv[u]) - _diag_blocks(bt[u][:, LANES:2 * LANES]))
        y_i.append(av[u] - ab[u][:, LANES:2 * LANES])
    z = {(s, p): z_ref[s, p] for s in range(sb) for p in range(PAIRS)}
    for u in range(nu):
        s, ch, p = units[u]
        both = _dot(lhs_z[u], _bd(z[(s, p)].astype(BF16)))
        y = both[hd:hd + c] + y_i[u]
        z[(s, p)] = both[0:hd] + u_c[u]
        mean = _seg_sum(y, first) * (1.0 / hd)
        yc = y - mean
        var = _seg_sum(yc * yc, first) * (1.0 / hd)
        yn = yc * lax.rsqrt(var + GN_EPS)
        ps = slice(p * LANES, (p + 1) * LANES)
        out = (yn * vec[8:9, ps] + vec[9:10, ps] + rows("bonus", u)) * rows("gate", u)
        y_ref[s, ch * c:(ch + 1) * c, ps] = out.astype(y_ref.dtype)
    for s in range(sb):
        for p in range(PAIRS):
            z_ref[s, p] = z[(s, p)]

    @pl.when(b == nb - 1)
    def _():
        for s in range(sb):
            for p in range(PAIRS):
                sout_ref[s, 2 * p] = z[(s, p)][:, 0:hd].T
                sout_ref[s, 2 * p + 1] = z[(s, p)][:, hd:2 * hd].T


def _rwkv(p3, shiftp, s0, vec, mu_l, wl1, wl2, layer, *, sb, tb, c):
    s, t, _ = p3.shape
    lb = _COL_LORA // _LORA_W
    tpb = tb // BF16_ROWS
    nb = t // tb

    def prev_tile(b):
        return jnp.maximum(b * tpb - 1, 0)

    in_specs = [
        pl.BlockSpec((sb, tb, _RKV_W), lambda i, b: (i, b, 0)),
        pl.BlockSpec((sb, tb, _LORA_W), lambda i, b: (i, b, lb)),
        pl.BlockSpec((sb, BF16_ROWS, _RKV_W), lambda i, b: (i, prev_tile(b), 0)),
        pl.BlockSpec((sb, BF16_ROWS, _LORA_W), lambda i, b: (i, prev_tile(b), lb)),
        pl.BlockSpec((sb, 1, _RKV_W), lambda i, b: (i, 0, 0)),
        pl.BlockSpec((sb, 1, _LORA_W), lambda i, b: (i, 0, _RKV_W // _LORA_W)),
        pl.BlockSpec((None, _VEC_ROWS, A_WIDTH), lambda i, b: (layer, 0, 0)),
        pl.BlockSpec((None, 1, _LORA_W), lambda i, b: (layer, 0, 0)),
        pl.BlockSpec((None, LANES, 2 * A_WIDTH), lambda i, b: (layer, 0, 0)),
        pl.BlockSpec((None, _LORA_GW, A_WIDTH), lambda i, b: (layer, 0, 0)),
        pl.BlockSpec((sb, A_HEADS, HEAD_DIM, HEAD_DIM), lambda i, b: (i, 0, 0, 0)),
    ]
    out_specs = [
        pl.BlockSpec((sb, tb, A_WIDTH), lambda i, b: (i, b, 0)),
        pl.BlockSpec((sb, A_HEADS, HEAD_DIM, HEAD_DIM), lambda i, b: (i, 0, 0, 0)),
    ]
    return pl.pallas_call(
        functools.partial(_rwkv_kernel, sb=sb, tb=tb, c=c, nb=nb),
        grid=(s // sb, nb),
        in_specs=in_specs,
        out_specs=out_specs,
        out_shape=[jax.ShapeDtypeStruct((s, t, A_WIDTH), BF16),
                   jax.ShapeDtypeStruct((s, A_HEADS, HEAD_DIM, HEAD_DIM), F32)],
        scratch_shapes=[pltpu.VMEM((sb, PAIRS, HEAD_DIM, LANES), F32)],
        compiler_params=_cparams(2),
        name="rwkv",
    )(p3, p3, p3, p3, shiftp, shiftp, vec, mu_l, wl1, wl2, s0)


def _norm_rope(x, gain, cos, sin_signed):
    lane = lax.broadcasted_iota(jnp.int32, (1, LANES), 1)
    first = lane < HEAD_DIM
    low_half = (lane % HEAD_DIM) < (HEAD_DIM // 2)
    parts = []
    for j in range(x.shape[1] // LANES):
        xs = x[:, j * LANES:(j + 1) * LANES]
        ms = _seg_sum(xs * xs, first) * (1.0 / HEAD_DIM)
        xn = xs * lax.rsqrt(ms + RMS_EPS) * gain
        partner = jnp.where(low_half, pltpu.roll(xn, LANES - HEAD_DIM // 2, 1),
                            pltpu.roll(xn, HEAD_DIM // 2, 1))
        parts.append(xn * cos + partner * sin_signed)
    return jnp.concatenate(parts, axis=1) if len(parts) > 1 else parts[0]


def _sink_softmax(s, sink):
    m = jnp.maximum(jnp.max(s, axis=-1, keepdims=True), sink)
    p = jnp.exp(s - m)
    denom = jnp.sum(p, axis=-1, keepdims=True) + jnp.exp(sink - m)
    return (p / denom).astype(BF16)


_QK_SCALE = HEAD_DIM ** -0.5


def _band_kernel(sink_ref, q_ref, kv_ref, kvp_ref, cq_ref, sq_ref, cp_ref, sp_ref, gq_ref, gk_ref,
                 y_ref, ko_ref, *, tq):
    i = pl.program_id(0)
    hd = HEAD_DIM
    band = WINDOW + CHUNK
    q = (_norm_rope(q_ref[...].astype(F32), gq_ref[...], cq_ref[...], sq_ref[...]) * _QK_SCALE).astype(BF16)
    kv = kv_ref[...]
    kvp = kvp_ref[...]
    k_cur = _norm_rope(kv[:, 0:B_KV].astype(F32), gk_ref[...], cq_ref[...], sq_ref[...])
    k_prev = _norm_rope(kvp[:, 0:B_KV].astype(F32), gk_ref[...], cp_ref[...], sp_ref[...])
    ko_ref[...] = k_cur
    k_all = jnp.concatenate([k_prev.astype(BF16), k_cur.astype(BF16)], axis=0)
    v_all = jnp.concatenate([kvp[:, B_KV:2 * B_KV], kv[:, B_KV:2 * B_KV]], axis=0)
    k_g = [k_all[:, g * hd:(g + 1) * hd] for g in range(B_KV_HEADS)]
    v_g = [v_all[:, g * hd:(g + 1) * hd] for g in range(B_KV_HEADS)]
    q_h = [q[:, h * hd:(h + 1) * hd] for h in range(B_HEADS)]
    col = lax.broadcasted_iota(jnp.int32, (1, band), 1)

    jobs = [(k0, h) for k0 in range(0, tq, CHUNK) for h in range(B_HEADS)]
    scores = [_dot_nt(q_h[h][k0:k0 + CHUNK], k_g[h // B_GROUP][k0:k0 + band]) for k0, h in jobs]
    probs = []
    for (k0, h), s in zip(jobs, scores):
        valid = (col >= WINDOW - k0) | (i > 0)
        probs.append(_sink_softmax(jnp.where(valid, s, -1e30), sink_ref[h]))
    outs = [_dot(p, v_g[h // B_GROUP][k0:k0 + band]) for (k0, h), p in zip(jobs, probs)]
    for n, k0 in enumerate(range(0, tq, CHUNK)):
        y_ref[k0:k0 + CHUNK, :] = jnp.concatenate(outs[n * B_HEADS:(n + 1) * B_HEADS], axis=1).astype(y_ref.dtype)


def _attn_band(p2, cos, sin_signed, gq, gk, sinks, layer, *, tq):
    t = p2.shape[0]
    qb, kvb = _COL_Q // B_Q, _COL_KV // _KV_W
    wpb = tq // WINDOW

    def prev_idx(i):
        return jnp.maximum(i * wpb - 1, 0)

    return pl.pallas_call(
        functools.partial(_band_kernel, tq=tq),
        grid=(t // tq,),
        in_specs=[pl.BlockSpec(memory_space=pltpu.SMEM),
                  pl.BlockSpec((tq, B_Q), lambda i: (i, qb)),
                  pl.BlockSpec((tq, _KV_W), lambda i: (i, kvb)),
                  pl.BlockSpec((WINDOW, _KV_W), lambda i: (prev_idx(i), kvb)),
                  pl.BlockSpec((tq, LANES), lambda i: (i, 0)),
                  pl.BlockSpec((tq, LANES), lambda i: (i, 0)),
                  pl.BlockSpec((WINDOW, LANES), lambda i: (prev_idx(i), 0)),
                  pl.BlockSpec((WINDOW, LANES), lambda i: (prev_idx(i), 0)),
                  pl.BlockSpec((None, 1, LANES), lambda i: (layer, 0, 0)),
                  pl.BlockSpec((None, 1, LANES), lambda i: (layer, 0, 0))],
        out_specs=[pl.BlockSpec((tq, B_Q), lambda i: (i, 0)),
                   pl.BlockSpec((tq, B_KV), lambda i: (i, 0))],
        out_shape=[jax.ShapeDtypeStruct((t, B_Q), BF16),
                   jax.ShapeDtypeStruct((t, B_KV), F32)],
        compiler_params=_cparams(1),
        name="attn_band",
    )(sinks[layer], p2, p2, p2, cos, sin_signed, cos, sin_signed, gq, gk)


def _cached_kernel(sink_ref, q_ref, kv_ref, ck_ref, cv_ref, c_ref, s_ref, gq_ref, gk_ref,
                   y_ref, ko_ref, *, sb):
    hd = HEAD_DIM
    tn = q_ref.shape[1]
    jobs = [(s, g) for s in range(sb) for g in range(B_KV_HEADS)]
    q_rows, k_all, v_all = {}, {}, {}
    for s in range(sb):
        q = (_norm_rope(q_ref[s].astype(F32), gq_ref[...], c_ref[...], s_ref[...]) * _QK_SCALE).astype(BF16)
        kv = kv_ref[s]
        k_new = _norm_rope(kv[:, 0:B_KV].astype(F32), gk_ref[...], c_ref[...], s_ref[...])
        ko_ref[s] = k_new
        k_cat = jnp.concatenate([ck_ref[s].astype(BF16), k_new.astype(BF16)], axis=0)
        v_cat = jnp.concatenate([cv_ref[s].astype(BF16), kv[:, B_KV:2 * B_KV]], axis=0)
        for g in range(B_KV_HEADS):
            q_rows[(s, g)] = jnp.concatenate(
                [q[:, (g * B_GROUP + j) * hd:(g * B_GROUP + j + 1) * hd] for j in range(B_GROUP)], axis=0)
            k_all[(s, g)] = k_cat[:, g * hd:(g + 1) * hd]
            v_all[(s, g)] = v_cat[:, g * hd:(g + 1) * hd]
    scores = [_dot_nt(q_rows[j], k_all[j]) for j in jobs]
    probs = []
    for (s, g), sc in zip(jobs, scores):
        sink = jnp.concatenate([jnp.full((tn, 1), sink_ref[g * B_GROUP + j], F32) for j in range(B_GROUP)], axis=0)
        probs.append(_sink_softmax(sc, sink))
    outs = {j: _dot(p, v_all[j]) for j, p in zip(jobs, probs)}
    for s in range(sb):
        heads = [outs[(s, g)][j * tn:(j + 1) * tn] for g in range(B_KV_HEADS) for j in range(B_GROUP)]
        y_ref[s] = jnp.concatenate(heads, axis=1).astype(y_ref.dtype)


def _attn_cached(p3, ck, cv, cos, sin_signed, gq, gk, sinks, layer, *, sb):
    s, t, _ = p3.shape
    w = ck.shape[2]
    qb, kvb = _COL_Q // B_Q, _COL_KV // _KV_W
    return pl.pallas_call(
        functools.partial(_cached_kernel, sb=sb),
        grid=(s // sb,),
        in_specs=[pl.BlockSpec(memory_space=pltpu.SMEM),
                  pl.BlockSpec((sb, t, B_Q), lambda i: (i, 0, qb)),
                  pl.BlockSpec((sb, t, _KV_W), lambda i: (i, 0, kvb)),
                  pl.BlockSpec((None, sb, w, B_KV), lambda i: (layer, i, 0, 0)),
                  pl.BlockSpec((None, sb, w, B_KV), lambda i: (layer, i, 0, 0)),
                  pl.BlockSpec((t, LANES), lambda i: (0, 0)),
                  pl.BlockSpec((t, LANES), lambda i: (0, 0)),
                  pl.BlockSpec((None, 1, LANES), lambda i: (layer, 0, 0)),
                  pl.BlockSpec((None, 1, LANES), lambda i: (layer, 0, 0))],
        out_specs=[pl.BlockSpec((sb, t, B_Q), lambda i: (i, 0, 0)),
                   pl.BlockSpec((sb, t, B_KV), lambda i: (i, 0, 0))],
        out_shape=[jax.ShapeDtypeStruct((s, t, B_Q), BF16),
                   jax.ShapeDtypeStruct((s, t, B_KV), F32)],
        compiler_params=_cparams(1),
        name="attn_cached",
    )(sinks[layer], p3, p3, ck, cv, cos, sin_signed, gq, gk)


def _outproj_kernel(x_ref, ya_ref, yb_ref, ga_ref, gb_ref, wa_ref, wb_ref, wo_ref, o_ref):
    ua = jnp.dot(ya_ref[...], wa_ref[...], preferred_element_type=F32)
    ub = jnp.dot(yb_ref[...], wb_ref[...], preferred_element_type=F32)
    merged = (jax.nn.sigmoid(ga_ref[...].astype(F32)) * ua + jax.nn.sigmoid(gb_ref[...].astype(F32)) * ub)
    o_ref[...] = x_ref[...] + jnp.dot(merged.astype(BF16), wo_ref[...], preferred_element_type=F32)


def _outproj(x, ya, yb, p2, wa, wb, wo, layer, *, tm):
    t, d = x.shape
    gab, gbb = _COL_GA // d, _COL_GB // d
    return pl.pallas_call(
        _outproj_kernel,
        grid=(t // tm,),
        in_specs=[pl.BlockSpec((tm, d), lambda i: (i, 0)),
                  pl.BlockSpec((tm, A_WIDTH), lambda i: (i, 0)),
                  pl.BlockSpec((tm, B_Q), lambda i: (i, 0)),
                  pl.BlockSpec((tm, d), lambda i: (i, gab)),
                  pl.BlockSpec((tm, d), lambda i: (i, gbb)),
                  _resident((None, A_WIDTH, d), lambda i: (layer, 0, 0)),
                  _resident((None, B_Q, d), lambda i: (layer, 0, 0)),
                  _resident((None, d, d), lambda i: (layer, 0, 0))],
        out_specs=pl.BlockSpec((tm, d), lambda i: (i, 0)),
        out_shape=jax.ShapeDtypeStruct((t, d), F32),
        compiler_params=_cparams(1),
        name="outproj",
    )(x, ya, yb, p2, p2, wa, wb, wo)


def _a_pieces(a):
    o = 0
    out = []
    for w in (A_WIDTH, DECAY_LORA, A_WIDTH, A_WIDTH, ICLR_LORA, GATE_LORA):
        out.append(a[..., o:o + w])
        o += w
    return out


def _pad_last(w, n):
    return jnp.pad(w, [(0, 0)] * (w.ndim - 1) + [(0, n - w.shape[-1])])


def _regroup_a(a):
    r, w_lo, k, v, a_lo, g_lo = _a_pieces(a)
    lora = _pad_last(jnp.concatenate([w_lo, a_lo, g_lo], axis=-1), _LORA_W)
    return jnp.concatenate([r, k, v], axis=-1), lora


def _regroup_w_in(w):
    o = A_COLS
    wq = w[..., o:o + B_Q]
    wk = w[..., o + B_Q:o + B_Q + B_KV]
    wv = w[..., o + B_Q + B_KV:o + B_Q + 2 * B_KV]
    o += B_Q + 2 * B_KV
    wga = w[..., o:o + D_MODEL]
    wgb = w[..., o + D_MODEL:o + 2 * D_MODEL]
    rkv, lora = _regroup_a(w[..., :A_COLS])
    return jnp.concatenate([rkv, wq, wga, wgb, lora, wk, wv], axis=-1).astype(BF16)


def _shift_row_to_a(p_last):
    p_last = p_last.astype(F32)
    r = p_last[..., _COL_R:_COL_R + A_WIDTH]
    k = p_last[..., _COL_K:_COL_K + A_WIDTH]
    v = p_last[..., _COL_V:_COL_V + A_WIDTH]
    o = _COL_LORA
    w_lo = p_last[..., o:o + DECAY_LORA]
    a_lo = p_last[..., o + DECAY_LORA:o + DECAY_LORA + ICLR_LORA]
    g_lo = p_last[..., o + DECAY_LORA + ICLR_LORA:o + DECAY_LORA + ICLR_LORA + GATE_LORA]
    return jnp.concatenate([r, w_lo, k, v, a_lo, g_lo], axis=-1)


def _lora_weights(decay_w2, iclr_a2, gate_g2):
    top = jnp.concatenate([decay_w2, jnp.zeros_like(decay_w2)], axis=-1)
    bot = jnp.concatenate([jnp.zeros_like(iclr_a2), iclr_a2], axis=-1)
    wl1 = jnp.concatenate([top, bot], axis=-2)
    wl2 = jnp.pad(gate_g2, ((0, 0), (0, _LORA_GW - GATE_LORA), (0, 0)))
    return wl1.astype(BF16), wl2.astype(BF16)


def _rope_tables(pos):
    half = HEAD_DIM // 2
    inv = ROPE_THETA ** (-jnp.arange(half, dtype=F32) / half)
    ang = pos.astype(F32)[:, None] * inv[None, :]
    cos = jnp.cos(ang)
    sin = jnp.sin(ang)
    cos_t = jnp.tile(cos, (1, LANES // half))
    sin_t = jnp.tile(jnp.concatenate([-sin, sin], axis=1), (1, LANES // HEAD_DIM))
    return cos_t, sin_t


_FF_TILE = 512


def kernel(x_prompt, x_sample, cache_k, cache_v, state_wkv, state_shift, norm_ff1, ff1_gate, ff1_up, ff1_down,
           norm_mix, w_in, shift_mu, decay_w0, decay_w2, iclr_a0, iclr_a2, gate_g2, k_k, k_a, r_k, gn_gain,
           gn_bias, q_norm, k_norm, sinks, w_up_a, w_up_b, w_o, norm_ff2, ff2_gate, ff2_up, ff2_down):
    depth = norm_ff1.shape[0]
    bp, tp, d = x_prompt.shape
    bs, ts, _ = x_sample.shape
    assert bp == 1 and d == D_MODEL
    d_ff = ff1_gate.shape[2]
    ffp = -(-d_ff // _FF_TILE) * _FF_TILE

    def ff_weights(gate, up, down):
        return (_pad_last(gate.astype(BF16), ffp), _pad_last(up.astype(BF16), ffp),
                jnp.pad(down.astype(BF16), ((0, 0), (0, ffp - d_ff), (0, 0))))

    ff1 = ff_weights(ff1_gate, ff1_up, ff1_down)
    ff2 = ff_weights(ff2_gate, ff2_up, ff2_down)
    g_ff1 = norm_ff1[:, None, :]
    g_ff2 = norm_ff2[:, None, :]
    g_mix = norm_mix[:, None, :]
    w_in_b = _regroup_w_in(w_in)
    mu_rkv, mu_l = _regroup_a(shift_mu)
    mu_l = mu_l[:, None, :]
    rows = [mu_rkv[:, 0:A_WIDTH], mu_rkv[:, A_WIDTH:2 * A_WIDTH], mu_rkv[:, 2 * A_WIDTH:], decay_w0, iclr_a0,
            k_k, k_a, r_k.reshape(depth, A_WIDTH), gn_gain, gn_bias]
    vec = jnp.stack(rows + [jnp.zeros((depth, A_WIDTH), F32)] * (_VEC_ROWS - len(rows)), axis=1)
    wl1, wl2 = _lora_weights(decay_w2, iclr_a2, gate_g2)
    gq = jnp.tile(q_norm, (1, LANES // HEAD_DIM))[:, None, :]
    gk = jnp.tile(k_norm, (1, LANES // HEAD_DIM))[:, None, :]
    wa = w_up_a.astype(BF16)
    wb = w_up_b.astype(BF16)
    wo = w_o.astype(BF16)
    s_rkv, s_lora = _regroup_a(state_shift)
    shift_s = jnp.concatenate([s_rkv, s_lora], axis=-1)[:, :, None, :]
    zero_shift = jnp.zeros((bp, 1, _RKV_W + _LORA_W), F32)
    zero_state = jnp.zeros((bp, A_HEADS, HEAD_DIM, HEAD_DIM), F32)
    ck = cache_k.reshape(depth, bs, -1, B_KV)
    cv = cache_v.reshape(depth, bs, -1, B_KV)
    cos_p, sin_p = _rope_tables(jnp.arange(tp))
    cos_s, sin_s = _rope_tables(PAST_LEN + jnp.arange(ts))

    xp = x_prompt.reshape(tp, d)
    xs = x_sample.reshape(bs * ts, d)
    outs = {k: [] for k in ("p_wkv", "p_shift", "p_k", "p_v", "s_wkv", "s_shift", "s_k", "s_v")}
    for l in range(depth):
        xp = _ffn(xp, g_ff1, *ff1, l, tm=512, tf=_FF_TILE)
        xs = _ffn(xs, g_ff1, *ff1, l, tm=512, tf=_FF_TILE)

        pp = _inproj(xp, g_mix, w_in_b, l, tm=1024, tn=1024)
        ps = _inproj(xs, g_mix, w_in_b, l, tm=512, tn=1024)
        pp3 = pp.reshape(bp, tp, NP_COLS)
        ps3 = ps.reshape(bs, ts, NP_COLS)

        ya_p, wkv_p = _rwkv(pp3, zero_shift, zero_state, vec, mu_l, wl1, wl2, l, sb=1, tb=2 * CHUNK, c=CHUNK)
        ya_s, wkv_s = _rwkv(ps3, shift_s[l], state_wkv[l], vec, mu_l, wl1, wl2, l, sb=2, tb=ts, c=ts)

        yb_p, kr_p = _attn_band(pp, cos_p, sin_p, gq, gk, sinks, l, tq=WINDOW)
        yb_s, kr_s = _attn_cached(ps3, ck, cv, cos_s, sin_s, gq, gk, sinks, l, sb=8)

        xp = _outproj(xp, ya_p.reshape(tp, A_WIDTH), yb_p, pp, wa, wb, wo, l, tm=256)
        xs = _outproj(xs, ya_s.reshape(bs * ts, A_WIDTH), yb_s.reshape(bs * ts, B_Q), ps, wa, wb, wo, l, tm=256)

        xp = _ffn(xp, g_ff2, *ff2, l, tm=512, tf=_FF_TILE)
        xs = _ffn(xs, g_ff2, *ff2, l, tm=512, tf=_FF_TILE)

        vcol = _COL_KV + B_KV
        outs["p_wkv"].append(wkv_p)
        outs["p_shift"].append(_shift_row_to_a(pp3[:, -1, :]))
        outs["p_k"].append(kr_p[-WINDOW:].reshape(bp, WINDOW, B_KV_HEADS, HEAD_DIM))
        outs["p_v"].append(pp3[:, -WINDOW:, vcol:vcol + B_KV].astype(F32).reshape(bp, WINDOW, B_KV_HEADS, HEAD_DIM))
        outs["s_wkv"].append(wkv_s)
        outs["s_shift"].append(_shift_row_to_a(ps3[:, -1, :]))
        outs["s_k"].append(kr_s.reshape(bs, ts, B_KV_HEADS, HEAD_DIM))
        outs["s_v"].append(ps3[:, :, vcol:vcol + B_KV].astype(F32).reshape(bs, ts, B_KV_HEADS, HEAD_DIM))

    return (xp.reshape(bp, tp, d), xs.reshape(bs, ts, d),
            jnp.stack(outs["p_wkv"]), jnp.stack(outs["p_shift"]), jnp.stack(outs["p_k"]), jnp.stack(outs["p_v"]),
            jnp.stack(outs["s_wkv"]), jnp.stack(outs["s_shift"]), jnp.stack(outs["s_k"]), jnp.stack(outs["s_v"]))
```

```python
import functools

import jax
import jax.numpy as jnp
from jax import lax
from jax.experimental import pallas as pl
from jax.experimental.pallas import tpu as pltpu

F32 = jnp.float32
BF16 = jnp.bfloat16

HEAD_DIM = 64
A_WIDTH = 1024
A_HEADS = A_WIDTH // HEAD_DIM
DECAY_LORA = 64
ICLR_LORA = 64
GATE_LORA = 160
GN_EPS = 64e-5
RMS_EPS = 1e-6
B_HEADS = 16
B_KV_HEADS = 4
B_GROUP = B_HEADS // B_KV_HEADS
B_Q = B_HEADS * HEAD_DIM
B_KV = B_KV_HEADS * HEAD_DIM
CHUNK = 64
WINDOW = 128
ROPE_THETA = 10000.0
PAST_LEN = 1024
D_MODEL = 2048
A_COLS = 3 * A_WIDTH + DECAY_LORA + ICLR_LORA + GATE_LORA

LANES = 128
BF16_ROWS = 16
VMEM_LIMIT_BYTES = 56 * 1024 * 1024

_COL_R = 0
_COL_K = _COL_R + A_WIDTH
_COL_V = _COL_K + A_WIDTH
_RKV_W = 3 * A_WIDTH
_COL_Q = _COL_V + A_WIDTH
_COL_GA = _COL_Q + B_Q
_COL_GB = _COL_GA + D_MODEL
_COL_LORA = _COL_GB + D_MODEL
_LORA_W = 512
_COL_KV = _COL_LORA + _LORA_W
_KV_W = 2 * B_KV
NP_COLS = _COL_KV + _KV_W
_LORA_G0 = LANES
_LORA_GW = 2 * LANES
PAIRS = A_HEADS // 2
_VEC_ROWS = 16


def _cparams(n_axes):
    return pltpu.CompilerParams(dimension_semantics=("arbitrary",) * n_axes,
                                vmem_limit_bytes=VMEM_LIMIT_BYTES)


def _dot(a, b):
    return jnp.dot(a.astype(BF16), b.astype(BF16), preferred_element_type=F32)


def _dot_nt(a, b):
    return lax.dot_general(a.astype(BF16), b.astype(BF16), (((1,), (1,)), ((), ())),
                           preferred_element_type=F32)


def _dot_tn(a, b):
    return lax.dot_general(a.astype(BF16), b.astype(BF16), (((0,), (0,)), ((), ())),
                           preferred_element_type=F32)


def _split3(x):
    hi = x.astype(BF16)
    r1 = x - hi.astype(F32)
    mid = r1.astype(BF16)
    lo = (r1 - mid.astype(F32)).astype(BF16)
    return hi, mid, lo


def _rms_rows(x, gain):
    ms = jnp.mean(x * x, axis=-1, keepdims=True)
    return x * lax.rsqrt(ms + RMS_EPS) * gain


def _resident(shape, index_map):
    return pl.BlockSpec(shape, index_map, pipeline_mode=pl.Buffered(1))


def _ffn_kernel(x_ref, g_ref, wg_ref, wu_ref, wd_ref, wgt_ref, wut_ref, wdt_ref, o_ref, h_ref, *, n_full):
    f = pl.program_id(1)

    @pl.when(f == 0)
    def _():
        x = x_ref[...]
        h_ref[...] = _rms_rows(x, g_ref[...]).astype(BF16)
        o_ref[...] = x

    def accumulate(wg, wu, wd):
        h = h_ref[...]
        gate = jnp.dot(h, wg[...], preferred_element_type=F32)
        up = jnp.dot(h, wu[...], preferred_element_type=F32)
        act = (0.5 * gate * jax.nn.sigmoid(gate) * up).astype(BF16)
        o_ref[...] += jnp.dot(act, wd[...], preferred_element_type=F32)

    @pl.when(f < n_full)
    def _():
        accumulate(wg_ref, wu_ref, wd_ref)

    @pl.when(f == n_full)
    def _():
        accumulate(wgt_ref, wut_ref, wdt_ref)


def _ffn(x, gain, w, layer, *, tm, tf):
    wg, wu, wd, wgt, wut, wdt = w
    t, d = x.shape
    n_full = wg.shape[2] // tf
    tail = wgt.shape[2]
    last = n_full - 1
    return pl.pallas_call(
        functools.partial(_ffn_kernel, n_full=n_full),
        grid=(t // tm, n_full + 1),
        in_specs=[pl.BlockSpec((tm, d), lambda i, f: (i, 0)),
                  pl.BlockSpec((None, 1, d), lambda i, f: (layer, 0, 0)),
                  pl.BlockSpec((None, d, tf), lambda i, f: (layer, 0, jnp.minimum(f, last))),
                  pl.BlockSpec((None, d, tf), lambda i, f: (layer, 0, jnp.minimum(f, last))),
                  pl.BlockSpec((None, tf, d), lambda i, f: (layer, jnp.minimum(f, last), 0)),
                  _resident((None, d, tail), lambda i, f: (layer, 0, 0)),
                  _resident((None, d, tail), lambda i, f: (layer, 0, 0)),
                  _resident((None, tail, d), lambda i, f: (layer, 0, 0))],
        out_specs=pl.BlockSpec((tm, d), lambda i, f: (i, 0)),
        out_shape=jax.ShapeDtypeStruct((t, d), F32),
        scratch_shapes=[pltpu.VMEM((tm, d), BF16)],
        compiler_params=_cparams(2),
        name="ffn",
    )(x, gain, wg, wu, wd, wgt, wut, wdt)


def _ffn_weights(gate, up, down, tf):
    d_ff = gate.shape[2]
    n_full = d_ff // tf
    assert d_ff % tf != 0 and (d_ff - n_full * tf) % LANES == 0
    g, u, dn = gate.astype(BF16), up.astype(BF16), down.astype(BF16)
    return g, u, dn, g[:, :, n_full * tf:], u[:, :, n_full * tf:], dn[:, n_full * tf:, :]


def _inproj_kernel(x_ref, g_ref, w_ref, o_ref, h_ref):
    @pl.when(pl.program_id(1) == 0)
    def _():
        h_ref[...] = _rms_rows(x_ref[...], g_ref[...]).astype(BF16)

    o_ref[...] = jnp.dot(h_ref[...], w_ref[...], preferred_element_type=F32).astype(o_ref.dtype)


def _inproj(x, gain, w, layer, *, tm, tn):
    t, d = x.shape
    n = w.shape[2]
    return pl.pallas_call(
        _inproj_kernel,
        grid=(t // tm, n // tn),
        in_specs=[pl.BlockSpec((tm, d), lambda i, j: (i, 0)),
                  pl.BlockSpec((None, 1, d), lambda i, j: (layer, 0, 0)),
                  pl.BlockSpec((None, d, tn), lambda i, j: (layer, 0, j))],
        out_specs=pl.BlockSpec((tm, tn), lambda i, j: (i, j)),
        out_shape=jax.ShapeDtypeStruct((t, n), BF16),
        scratch_shapes=[pltpu.VMEM((tm, d), BF16)],
        compiler_params=_cparams(2),
        name="inproj",
    )(x, gain, w)


def _seg_sum(x, first):
    s0 = jnp.sum(jnp.where(first, x, 0.0), axis=1, keepdims=True)
    s1 = jnp.sum(jnp.where(first, 0.0, x), axis=1, keepdims=True)
    return jnp.where(first, s0, s1)


def _bd(x):
    w = x.shape[1]
    first = lax.broadcasted_iota(jnp.int32, (1, w), 1) < (w // 2)
    zero = jnp.zeros_like(x)
    return jnp.concatenate([jnp.where(first, x, zero), jnp.where(first, zero, x)], axis=0)


def _diag_blocks(m):
    n = m.shape[0] // 2
    first = lax.broadcasted_iota(jnp.int32, (1, 2 * n), 1) < n
    return jnp.where(first, m[0:n], m[n:2 * n])


def _rwkv_kernel(rkv_ref, l_ref, rkvp_ref, lp_ref, srkv_ref, sl_ref, vec_ref, mul_ref, wl1_ref, wl2_ref,
                 s0_ref, y_ref, sout_ref, z_ref, *, sb, tb, c, nb):
    b = pl.program_id(1)
    nc = tb // c
    hd = HEAD_DIM

    vec = vec_ref[...]
    mu_rkv = jnp.concatenate([vec[0:1], vec[1:2], vec[2:3]], axis=1)
    mu_l = mul_ref[...]

    lane = lax.broadcasted_iota(jnp.int32, (1, LANES), 1)
    first = lane < hd
    row_t = lax.broadcasted_iota(jnp.int32, (tb, 1), 0)
    ri = lax.broadcasted_iota(jnp.int32, (c, 2 * c), 0)
    ci = lax.broadcasted_iota(jnp.int32, (c, 2 * c), 1) % c
    lower_incl = ri >= ci
    lower_strict = ri > ci
    eye_p = jnp.where(ri == ci, 1.0, 0.0).astype(F32)
    rt_ = lax.broadcasted_iota(jnp.int32, (c, c), 0)
    ct_ = lax.broadcasted_iota(jnp.int32, (c, c), 1)
    tri = jnp.where(rt_ >= ct_, 1.0, 0.0).astype(BF16)
    rj = lax.broadcasted_iota(jnp.int32, (hd, LANES), 0)
    cj = lax.broadcasted_iota(jnp.int32, (hd, LANES), 1) % hd
    eye_h = rj == cj

    @pl.when(b == 0)
    def _():
        for s in range(sb):
            for p in range(PAIRS):
                z_ref[s, p] = jnp.concatenate([s0_ref[s, 2 * p].T, s0_ref[s, 2 * p + 1].T], axis=1)

    def shifted(ref, pref, sref, s, mu):
        raw = ref[s].astype(F32)
        before = jnp.where(b == 0, sref[s], pref[s][BF16_ROWS - 1:BF16_ROWS, :].astype(F32))
        prev = jnp.where(row_t == 0, before, pltpu.roll(raw, 1, 0))
        return raw + (prev - raw) * mu

    pre = {}
    for s in range(sb):
        x3 = shifted(rkv_ref, rkvp_ref, srkv_ref, s, mu_rkv)
        lx = shifted(l_ref, lp_ref, sl_ref, s, mu_l)
        l01 = lx[:, 0:LANES]
        z01 = jnp.where(first, jnp.tanh(l01), l01)
        wa = _dot(z01, wl1_ref[...])
        gate = _dot(jax.nn.sigmoid(lx[:, _LORA_G0:_LORA_G0 + _LORA_GW]), wl2_ref[...])
        for p in range(PAIRS):
            ps = slice(p * LANES, (p + 1) * LANES)
            w0, a0, k_k, k_a, r_k = vec[3:4, ps], vec[4:5, ps], vec[5:6, ps], vec[6:7, ps], vec[7:8, ps]
            rx = x3[:, _COL_R + p * LANES:_COL_R + (p + 1) * LANES]
            kx = x3[:, _COL_K + p * LANES:_COL_K + (p + 1) * LANES]
            vx = x3[:, _COL_V + p * LANES:_COL_V + (p + 1) * LANES]
            w_pre = wa[:, ps] + w0
            a_pre = wa[:, A_WIDTH + p * LANES:A_WIDTH + (p + 1) * LANES] + a0
            softplus = jnp.maximum(-w_pre, 0.0) + jnp.log1p(jnp.exp(-jnp.abs(w_pre)))
            logd = -jnp.exp(-softplus - 0.5)
            a = jax.nn.sigmoid(a_pre)
            kxk = kx * k_k
            kk = kxk / jnp.maximum(jnp.sqrt(_seg_sum(kxk * kxk, first)), 1e-12)
            kp = kx * (1.0 + (a - 1.0) * k_a)
            pre[(s, p)] = dict(r=rx, kk=kk, kp=kp, bb=kk * a, v=vx, logd=logd, gate=gate[:, ps],
                               bonus=_seg_sum(rx * kp * r_k, first) * vx)

    units = [(s, ch, p) for ch in range(nc) for s in range(sb) for p in range(PAIRS)]

    def rows(name, u):
        s, ch, p = units[u]
        return pre[(s, p)][name][ch * c:(ch + 1) * c]

    nu = len(units)
    cin = []
    for u in range(nu):
        hi, mid, lo = _split3(rows("logd", u))
        cs = jnp.dot(tri, jnp.concatenate([hi, mid, lo], axis=1), preferred_element_type=F32)
        cin.append(cs[:, 0:LANES] + cs[:, LANES:2 * LANES] + cs[:, 2 * LANES:3 * LANES])
    kk_t, r_t, k_d, b_d, p_end, g = [], [], [], [], [], []
    for u in range(nu):
        ld = rows("logd", u)
        c_end = cin[u][c - 1:c, :]
        e_inv = jnp.exp(-cin[u])
        e_dec = jnp.exp(c_end - cin[u])
        kk_t.append((rows("kk", u) * jnp.exp(cin[u] - ld)).astype(BF16))
        r_t.append(rows("r", u) * jnp.exp(cin[u]))
        k_d.append((rows("kp", u) * e_dec).astype(BF16))
        b_d.append((rows("bb", u) * e_dec).astype(BF16))
        p_end.append(jnp.exp(c_end))
        lhs = jnp.concatenate([kk_t[u], r_t[u].astype(BF16)], axis=0)
        rhs = jnp.concatenate([_bd((rows("bb", u) * e_inv).astype(BF16)),
                               _bd((rows("kp", u) * e_inv).astype(BF16))], axis=0)
        g.append(_dot_nt(lhs, rhs))
    m_ab = [jnp.where(lower_strict, g[u][0:c, 0:2 * c], 0.0).astype(BF16) for u in range(nu)]
    a_rb = [jnp.where(lower_incl, g[u][c:2 * c, 0:2 * c], 0.0).astype(BF16) for u in range(nu)]
    m_ak = [jnp.where(lower_strict, g[u][0:c, 2 * c:4 * c], 0.0).astype(BF16) for u in range(nu)]
    a_rk = [jnp.where(lower_incl, g[u][c:2 * c, 2 * c:4 * c], 0.0).astype(BF16) for u in range(nu)]
    v_bd = [_bd(rows("v", u).astype(BF16)) for u in range(nu)]
    x = [eye_p - m_ab[u].astype(F32) for u in range(nu)]
    pw = [_dot(m_ab[u], _bd(m_ab[u])) for u in range(nu)]
    k = 2
    while 2 * k < c:
        both = [_dot(jnp.concatenate([x[u].astype(BF16), pw[u].astype(BF16)], axis=0), _bd(pw[u].astype(BF16)))
                for u in range(nu)]
        x = [x[u] + both[u][0:c] for u in range(nu)]
        pw = [both[u][c:2 * c] for u in range(nu)]
        k *= 2
    t_inv = [(x[u] + _dot(x[u], _bd(pw[u].astype(BF16)))).astype(BF16) for u in range(nu)]
    mv = [_dot(m_ak[u], v_bd[u]) for u in range(nu)]
    tw = [_dot(t_inv[u], jnp.concatenate([_bd(kk_t[u]), _bd(mv[u].astype(BF16))], axis=1)).astype(BF16)
          for u in range(nu)]
    bt = [_dot_tn(b_d[u], tw[u]) for u in range(nu)]
    kv = [_dot_tn(k_d[u], rows("v", u)) for u in range(nu)]
    ab = [_dot(a_rb[u], jnp.concatenate([_bd(tw[u][:, 0:LANES]), _bd(tw[u][:, LANES:2 * LANES])], axis=1))
          for u in range(nu)]
    av = [_dot(a_rk[u], v_bd[u]) for u in range(nu)]
    lhs_z, u_c, y_i = [], [], []
    for u in range(nu):
        a_c = jnp.where(eye_h, p_end[u], 0.0) - _diag_blocks(bt[u][:, 0:LANES])
        q_e = r_t[u] - ab[u][:, 0:LANES]
        lhs_z.append(jnp.concatenate([a_c, q_e], axis=0).astype(BF16))
        u_c.append(_diag_blocks(kv[u]) - _diag_blocks(bt[u][:, LANES:2 * LANES]))
        y_i.append(av[u] - ab[u][:, LANES:2 * LANES])
    z = {(s, p): z_ref[s, p] for s in range(sb) for p in range(PAIRS)}
    per_level = sb * PAIRS
    for ch in range(nc):
        level = range(ch * per_level, (ch + 1) * per_level)
        both = {u: _dot(lhs_z[u], _bd(z[units[u][0], units[u][2]].astype(BF16))) for u in level}
        ys = {u: both[u][hd:hd + c] + y_i[u] for u in level}
        for u in level:
            z[units[u][0], units[u][2]] = both[u][0:hd] + u_c[u]
        means = {u: _seg_sum(ys[u], first) * (1.0 / hd) for u in level}
        ycs = {u: ys[u] - means[u] for u in level}
        variances = {u: _seg_sum(ycs[u] * ycs[u], first) * (1.0 / hd) for u in level}
        for u in level:
            s, _, p = units[u]
            ps = slice(p * LANES, (p + 1) * LANES)
            yn = ycs[u] * lax.rsqrt(variances[u] + GN_EPS)
            out = (yn * vec[8:9, ps] + vec[9:10, ps] + rows("bonus", u)) * rows("gate", u)
            y_ref[s, ch * c:(ch + 1) * c, ps] = out.astype(y_ref.dtype)
    for s in range(sb):
        for p in range(PAIRS):
            z_ref[s, p] = z[(s, p)]

    @pl.when(b == nb - 1)
    def _():
        for s in range(sb):
            for p in range(PAIRS):
                sout_ref[s, 2 * p] = z[(s, p)][:, 0:hd].T
                sout_ref[s, 2 * p + 1] = z[(s, p)][:, hd:2 * hd].T


def _rwkv(p3, shiftp, s0, vec, mu_l, wl1, wl2, layer, *, sb, tb, c):
    s, t, _ = p3.shape
    lb = _COL_LORA // _LORA_W
    tpb = tb // BF16_ROWS
    nb = t // tb

    def prev_tile(b):
        return jnp.maximum(b * tpb - 1, 0)

    in_specs = [
        pl.BlockSpec((sb, tb, _RKV_W), lambda i, b: (i, b, 0)),
        pl.BlockSpec((sb, tb, _LORA_W), lambda i, b: (i, b, lb)),
        pl.BlockSpec((sb, BF16_ROWS, _RKV_W), lambda i, b: (i, prev_tile(b), 0)),
        pl.BlockSpec((sb, BF16_ROWS, _LORA_W), lambda i, b: (i, prev_tile(b), lb)),
        pl.BlockSpec((sb, 1, _RKV_W), lambda i, b: (i, 0, 0)),
        pl.BlockSpec((sb, 1, _LORA_W), lambda i, b: (i, 0, _RKV_W // _LORA_W)),
        pl.BlockSpec((None, _VEC_ROWS, A_WIDTH), lambda i, b: (layer, 0, 0)),
        pl.BlockSpec((None, 1, _LORA_W), lambda i, b: (layer, 0, 0)),
        pl.BlockSpec((None, LANES, 2 * A_WIDTH), lambda i, b: (layer, 0, 0)),
        pl.BlockSpec((None, _LORA_GW, A_WIDTH), lambda i, b: (layer, 0, 0)),
        pl.BlockSpec((sb, A_HEADS, HEAD_DIM, HEAD_DIM), lambda i, b: (i, 0, 0, 0)),
    ]
    out_specs = [
        pl.BlockSpec((sb, tb, A_WIDTH), lambda i, b: (i, b, 0)),
        pl.BlockSpec((sb, A_HEADS, HEAD_DIM, HEAD_DIM), lambda i, b: (i, 0, 0, 0)),
    ]
    return pl.pallas_call(
        functools.partial(_rwkv_kernel, sb=sb, tb=tb, c=c, nb=nb),
        grid=(s // sb, nb),
        in_specs=in_specs,
        out_specs=out_specs,
        out_shape=[jax.ShapeDtypeStruct((s, t, A_WIDTH), BF16),
                   jax.ShapeDtypeStruct((s, A_HEADS, HEAD_DIM, HEAD_DIM), F32)],
        scratch_shapes=[pltpu.VMEM((sb, PAIRS, HEAD_DIM, LANES), F32)],
        compiler_params=_cparams(2),
        name="rwkv",
    )(p3, p3, p3, p3, shiftp, shiftp, vec, mu_l, wl1, wl2, s0)


def _norm_rope(slabs):
    lane = lax.broadcasted_iota(jnp.int32, (1, LANES), 1)
    first = lane < HEAD_DIM
    low_half = (lane % HEAD_DIM) < (HEAD_DIM // 2)
    tiles = [(n, x[:, j * LANES:(j + 1) * LANES]) for n, (x, _, _, _) in enumerate(slabs)
             for j in range(x.shape[1] // LANES)]
    sq = [xs * xs for _, xs in tiles]
    s0 = [jnp.sum(jnp.where(first, s, 0.0), axis=1, keepdims=True) for s in sq]
    s1 = [jnp.sum(jnp.where(first, 0.0, s), axis=1, keepdims=True) for s in sq]
    xn = [xs * lax.rsqrt(jnp.where(first, a, b) * (1.0 / HEAD_DIM) + RMS_EPS) * slabs[n][1]
          for (n, xs), a, b in zip(tiles, s0, s1)]
    up = [pltpu.roll(v, LANES - HEAD_DIM // 2, 1) for v in xn]
    down = [pltpu.roll(v, HEAD_DIM // 2, 1) for v in xn]
    out = [v * slabs[n][2] + jnp.where(low_half, u, w) * slabs[n][3]
           for (n, _), v, u, w in zip(tiles, xn, up, down)]
    res, o = [], 0
    for x, _, _, _ in slabs:
        k = x.shape[1] // LANES
        res.append(jnp.concatenate(out[o:o + k], axis=1) if k > 1 else out[o])
        o += k
    return res


def _sink_softmax_all(scores, sinks):
    ms = [jnp.maximum(jnp.max(s, axis=-1, keepdims=True), k) for s, k in zip(scores, sinks)]
    ps = [jnp.exp(s - m) for s, m in zip(scores, ms)]
    ds = [jnp.sum(p, axis=-1, keepdims=True) + jnp.exp(k - m) for p, k, m in zip(ps, sinks, ms)]
    return [(p / d).astype(BF16) for p, d in zip(ps, ds)]


def _sink_column(sink_ref, heads, rows):
    return jnp.concatenate([jnp.full((rows, 1), sink_ref[h], F32) for h in heads], axis=0)


_QK_SCALE = HEAD_DIM ** -0.5


def _band_kernel(sink_ref, q_ref, kv_ref, kvp_ref, cq_ref, sq_ref, cp_ref, sp_ref, gq_ref, gk_ref,
                 y_ref, ko_ref, *, tq):
    i = pl.program_id(0)
    hd = HEAD_DIM
    band = WINDOW + CHUNK
    kv = kv_ref[...]
    kvp = kvp_ref[...]
    q, k_cur, k_prev = _norm_rope([
        (q_ref[...].astype(F32), gq_ref[...], cq_ref[...], sq_ref[...]),
        (kv[:, 0:B_KV].astype(F32), gk_ref[...], cq_ref[...], sq_ref[...]),
        (kvp[:, 0:B_KV].astype(F32), gk_ref[...], cp_ref[...], sp_ref[...])])
    q = (q * _QK_SCALE).astype(BF16)
    ko_ref[...] = k_cur
    k_all = jnp.concatenate([k_prev.astype(BF16), k_cur.astype(BF16)], axis=0)
    v_all = jnp.concatenate([kvp[:, B_KV:2 * B_KV], kv[:, B_KV:2 * B_KV]], axis=0)
    k_g = [k_all[:, g * hd:(g + 1) * hd] for g in range(B_KV_HEADS)]
    v_g = [v_all[:, g * hd:(g + 1) * hd] for g in range(B_KV_HEADS)]
    col = lax.broadcasted_iota(jnp.int32, (1, band), 1)
    sink_g = [_sink_column(sink_ref, range(g * B_GROUP, (g + 1) * B_GROUP), CHUNK) for g in range(B_KV_HEADS)]

    jobs = [(k0, g) for k0 in range(0, tq, CHUNK) for g in range(B_KV_HEADS)]
    q_rows = [jnp.concatenate([q[k0:k0 + CHUNK, (g * B_GROUP + j) * hd:(g * B_GROUP + j + 1) * hd]
                               for j in range(B_GROUP)], axis=0) for k0, g in jobs]
    scores = [_dot_nt(qr, k_g[g][k0:k0 + band]) for (k0, g), qr in zip(jobs, q_rows)]
    scores = [jnp.where((col >= WINDOW - k0) | (i > 0), s, -1e30) for (k0, g), s in zip(jobs, scores)]
    probs = _sink_softmax_all(scores, [sink_g[g] for k0, g in jobs])
    outs = {job: _dot(p, v_g[job[1]][job[0]:job[0] + band]) for job, p in zip(jobs, probs)}
    for k0 in range(0, tq, CHUNK):
        heads = [outs[(k0, g)][j * CHUNK:(j + 1) * CHUNK] for g in range(B_KV_HEADS) for j in range(B_GROUP)]
        y_ref[k0:k0 + CHUNK, :] = jnp.concatenate(heads, axis=1).astype(y_ref.dtype)


def _attn_band(p2, cos, sin_signed, gq, gk, sinks, layer, *, tq):
    t = p2.shape[0]
    qb, kvb = _COL_Q // B_Q, _COL_KV // _KV_W
    wpb = tq // WINDOW

    def prev_idx(i):
        return jnp.maximum(i * wpb - 1, 0)

    return pl.pallas_call(
        functools.partial(_band_kernel, tq=tq),
        grid=(t // tq,),
        in_specs=[pl.BlockSpec(memory_space=pltpu.SMEM),
                  pl.BlockSpec((tq, B_Q), lambda i: (i, qb)),
                  pl.BlockSpec((tq, _KV_W), lambda i: (i, kvb)),
                  pl.BlockSpec((WINDOW, _KV_W), lambda i: (prev_idx(i), kvb)),
                  pl.BlockSpec((tq, LANES), lambda i: (i, 0)),
                  pl.BlockSpec((tq, LANES), lambda i: (i, 0)),
                  pl.BlockSpec((WINDOW, LANES), lambda i: (prev_idx(i), 0)),
                  pl.BlockSpec((WINDOW, LANES), lambda i: (prev_idx(i), 0)),
                  pl.BlockSpec((None, 1, LANES), lambda i: (layer, 0, 0)),
                  pl.BlockSpec((None, 1, LANES), lambda i: (layer, 0, 0))],
        out_specs=[pl.BlockSpec((tq, B_Q), lambda i: (i, 0)),
                   pl.BlockSpec((tq, B_KV), lambda i: (i, 0))],
        out_shape=[jax.ShapeDtypeStruct((t, B_Q), BF16),
                   jax.ShapeDtypeStruct((t, B_KV), F32)],
        compiler_params=_cparams(1),
        name="attn_band",
    )(sinks[layer], p2, p2, p2, cos, sin_signed, cos, sin_signed, gq, gk)


def _cached_kernel(sink_ref, q_ref, kv_ref, ck_ref, cv_ref, c_ref, s_ref, gq_ref, gk_ref,
                   y_ref, ko_ref, *, sb):
    hd = HEAD_DIM
    tn = q_ref.shape[1]
    jobs = [(s, g) for s in range(sb) for g in range(B_KV_HEADS)]
    q_rows, k_all, v_all = {}, {}, {}
    cos = jnp.concatenate([c_ref[...]] * sb, axis=0)
    sin = jnp.concatenate([s_ref[...]] * sb, axis=0)
    q_cat = jnp.concatenate([q_ref[s] for s in range(sb)], axis=0).astype(F32)
    kv_cat = jnp.concatenate([kv_ref[s] for s in range(sb)], axis=0)
    q_cat, k_cat_new = _norm_rope([(q_cat, gq_ref[...], cos, sin),
                                   (kv_cat[:, 0:B_KV].astype(F32), gk_ref[...], cos, sin)])
    q_cat = (q_cat * _QK_SCALE).astype(BF16)
    for s in range(sb):
        q = q_cat[s * tn:(s + 1) * tn]
        k_new = k_cat_new[s * tn:(s + 1) * tn]
        ko_ref[s] = k_new
        k_cat = jnp.concatenate([ck_ref[s].astype(BF16), k_new.astype(BF16)], axis=0)
        v_cat = jnp.concatenate([cv_ref[s].astype(BF16), kv_cat[s * tn:(s + 1) * tn, B_KV:2 * B_KV]], axis=0)
        for g in range(B_KV_HEADS):
            q_rows[(s, g)] = jnp.concatenate(
                [q[:, (g * B_GROUP + j) * hd:(g * B_GROUP + j + 1) * hd] for j in range(B_GROUP)], axis=0)
            k_all[(s, g)] = k_cat[:, g * hd:(g + 1) * hd]
            v_all[(s, g)] = v_cat[:, g * hd:(g + 1) * hd]
    scores = [_dot_nt(q_rows[j], k_all[j]) for j in jobs]
    sink_g = [_sink_column(sink_ref, range(g * B_GROUP, (g + 1) * B_GROUP), tn) for g in range(B_KV_HEADS)]
    probs = _sink_softmax_all(scores, [sink_g[g] for s, g in jobs])
    outs = {j: _dot(p, v_all[j]) for j, p in zip(jobs, probs)}
    for s in range(sb):
        heads = [outs[(s, g)][j * tn:(j + 1) * tn] for g in range(B_KV_HEADS) for j in range(B_GROUP)]
        y_ref[s] = jnp.concatenate(heads, axis=1).astype(y_ref.dtype)


def _attn_cached(p3, ck, cv, cos, sin_signed, gq, gk, sinks, layer, *, sb):
    s, t, _ = p3.shape
    w = ck.shape[2]
    qb, kvb = _COL_Q // B_Q, _COL_KV // _KV_W
    return pl.pallas_call(
        functools.partial(_cached_kernel, sb=sb),
        grid=(s // sb,),
        in_specs=[pl.BlockSpec(memory_space=pltpu.SMEM),
                  pl.BlockSpec((sb, t, B_Q), lambda i: (i, 0, qb)),
                  pl.BlockSpec((sb, t, _KV_W), lambda i: (i, 0, kvb)),
                  pl.BlockSpec((None, sb, w, B_KV), lambda i: (layer, i, 0, 0)),
                  pl.BlockSpec((None, sb, w, B_KV), lambda i: (layer, i, 0, 0)),
                  pl.BlockSpec((t, LANES), lambda i: (0, 0)),
                  pl.BlockSpec((t, LANES), lambda i: (0, 0)),
                  pl.BlockSpec((None, 1, LANES), lambda i: (layer, 0, 0)),
                  pl.BlockSpec((None, 1, LANES), lambda i: (layer, 0, 0))],
        out_specs=[pl.BlockSpec((sb, t, B_Q), lambda i: (i, 0, 0)),
                   pl.BlockSpec((sb, t, B_KV), lambda i: (i, 0, 0))],
        out_shape=[jax.ShapeDtypeStruct((s, t, B_Q), BF16),
                   jax.ShapeDtypeStruct((s, t, B_KV), F32)],
        compiler_params=_cparams(1),
        name="attn_cached",
    )(sinks[layer], p3, p3, ck, cv, cos, sin_signed, gq, gk)


def _outproj_kernel(x_ref, ya_ref, yb_ref, ga_ref, gb_ref, wa_ref, wb_ref, wo_ref, o_ref):
    ua = jnp.dot(ya_ref[...], wa_ref[...], preferred_element_type=F32)
    ub = jnp.dot(yb_ref[...], wb_ref[...], preferred_element_type=F32)
    merged = (jax.nn.sigmoid(ga_ref[...].astype(F32)) * ua + jax.nn.sigmoid(gb_ref[...].astype(F32)) * ub)
    o_ref[...] = x_ref[...] + jnp.dot(merged.astype(BF16), wo_ref[...], preferred_element_type=F32)


def _outproj(x, ya, yb, p2, wa, wb, wo, layer, *, tm):
    t, d = x.shape
    gab, gbb = _COL_GA // d, _COL_GB // d
    return pl.pallas_call(
        _outproj_kernel,
        grid=(t // tm,),
        in_specs=[pl.BlockSpec((tm, d), lambda i: (i, 0)),
                  pl.BlockSpec((tm, A_WIDTH), lambda i: (i, 0)),
                  pl.BlockSpec((tm, B_Q), lambda i: (i, 0)),
                  pl.BlockSpec((tm, d), lambda i: (i, gab)),
                  pl.BlockSpec((tm, d), lambda i: (i, gbb)),
                  _resident((None, A_WIDTH, d), lambda i: (layer, 0, 0)),
                  _resident((None, B_Q, d), lambda i: (layer, 0, 0)),
                  _resident((None, d, d), lambda i: (layer, 0, 0))],
        out_specs=pl.BlockSpec((tm, d), lambda i: (i, 0)),
        out_shape=jax.ShapeDtypeStruct((t, d), F32),
        compiler_params=_cparams(1),
        name="outproj",
    )(x, ya, yb, p2, p2, wa, wb, wo)


def _a_pieces(a):
    o = 0
    out = []
    for w in (A_WIDTH, DECAY_LORA, A_WIDTH, A_WIDTH, ICLR_LORA, GATE_LORA):
        out.append(a[..., o:o + w])
        o += w
    return out


def _pad_last(w, n):
    return jnp.pad(w, [(0, 0)] * (w.ndim - 1) + [(0, n - w.shape[-1])])


def _regroup_a(a):
    r, w_lo, k, v, a_lo, g_lo = _a_pieces(a)
    lora = _pad_last(jnp.concatenate([w_lo, a_lo, g_lo], axis=-1), _LORA_W)
    return jnp.concatenate([r, k, v], axis=-1), lora


def _regroup_w_in(w):
    o = A_COLS
    wq = w[..., o:o + B_Q]
    wk = w[..., o + B_Q:o + B_Q + B_KV]
    wv = w[..., o + B_Q + B_KV:o + B_Q + 2 * B_KV]
    o += B_Q + 2 * B_KV
    wga = w[..., o:o + D_MODEL]
    wgb = w[..., o + D_MODEL:o + 2 * D_MODEL]
    rkv, lora = _regroup_a(w[..., :A_COLS])
    return jnp.concatenate([rkv, wq, wga, wgb, lora, wk, wv], axis=-1).astype(BF16)


def _shift_row_to_a(p_last):
    p_last = p_last.astype(F32)
    r = p_last[..., _COL_R:_COL_R + A_WIDTH]
    k = p_last[..., _COL_K:_COL_K + A_WIDTH]
    v = p_last[..., _COL_V:_COL_V + A_WIDTH]
    o = _COL_LORA
    w_lo = p_last[..., o:o + DECAY_LORA]
    a_lo = p_last[..., o + DECAY_LORA:o + DECAY_LORA + ICLR_LORA]
    g_lo = p_last[..., o + DECAY_LORA + ICLR_LORA:o + DECAY_LORA + ICLR_LORA + GATE_LORA]
    return jnp.concatenate([r, w_lo, k, v, a_lo, g_lo], axis=-1)


def _lora_weights(decay_w2, iclr_a2, gate_g2):
    top = jnp.concatenate([decay_w2, jnp.zeros_like(decay_w2)], axis=-1)
    bot = jnp.concatenate([jnp.zeros_like(iclr_a2), iclr_a2], axis=-1)
    wl1 = jnp.concatenate([top, bot], axis=-2)
    wl2 = jnp.pad(gate_g2, ((0, 0), (0, _LORA_GW - GATE_LORA), (0, 0)))
    return wl1.astype(BF16), wl2.astype(BF16)


def _rope_tables(pos):
    half = HEAD_DIM // 2
    inv = ROPE_THETA ** (-jnp.arange(half, dtype=F32) / half)
    ang = pos.astype(F32)[:, None] * inv[None, :]
    cos = jnp.cos(ang)
    sin = jnp.sin(ang)
    cos_t = jnp.tile(cos, (1, LANES // half))
    sin_t = jnp.tile(jnp.concatenate([-sin, sin], axis=1), (1, LANES // HEAD_DIM))
    return cos_t, sin_t


_FF_TILE = 512


def kernel(x_prompt, x_sample, cache_k, cache_v, state_wkv, state_shift, norm_ff1, ff1_gate, ff1_up, ff1_down,
           norm_mix, w_in, shift_mu, decay_w0, decay_w2, iclr_a0, iclr_a2, gate_g2, k_k, k_a, r_k, gn_gain,
           gn_bias, q_norm, k_norm, sinks, w_up_a, w_up_b, w_o, norm_ff2, ff2_gate, ff2_up, ff2_down):
    depth = norm_ff1.shape[0]
    bp, tp, d = x_prompt.shape
    bs, ts, _ = x_sample.shape
    assert bp == 1 and d == D_MODEL

    ff1 = _ffn_weights(ff1_gate, ff1_up, ff1_down, _FF_TILE)
    ff2 = _ffn_weights(ff2_gate, ff2_up, ff2_down, _FF_TILE)
    g_ff1 = norm_ff1[:, None, :]
    g_ff2 = norm_ff2[:, None, :]
    g_mix = norm_mix[:, None, :]
    w_in_b = _regroup_w_in(w_in)
    mu_rkv, mu_l = _regroup_a(shift_mu)
    mu_l = mu_l[:, None, :]
    rows = [mu_rkv[:, 0:A_WIDTH], mu_rkv[:, A_WIDTH:2 * A_WIDTH], mu_rkv[:, 2 * A_WIDTH:], decay_w0, iclr_a0,
            k_k, k_a, r_k.reshape(depth, A_WIDTH), gn_gain, gn_bias]
    vec = jnp.stack(rows + [jnp.zeros((depth, A_WIDTH), F32)] * (_VEC_ROWS - len(rows)), axis=1)
    wl1, wl2 = _lora_weights(decay_w2, iclr_a2, gate_g2)
    gq = jnp.tile(q_norm, (1, LANES // HEAD_DIM))[:, None, :]
    gk = jnp.tile(k_norm, (1, LANES // HEAD_DIM))[:, None, :]
    wa = w_up_a.astype(BF16)
    wb = w_up_b.astype(BF16)
    wo = w_o.astype(BF16)
    s_rkv, s_lora = _regroup_a(state_shift)
    shift_s = jnp.concatenate([s_rkv, s_lora], axis=-1)[:, :, None, :]
    zero_shift = jnp.zeros((bp, 1, _RKV_W + _LORA_W), F32)
    zero_state = jnp.zeros((bp, A_HEADS, HEAD_DIM, HEAD_DIM), F32)
    ck = cache_k.reshape(depth, bs, -1, B_KV)
    cv = cache_v.reshape(depth, bs, -1, B_KV)
    cos_p, sin_p = _rope_tables(jnp.arange(tp))
    cos_s, sin_s = _rope_tables(PAST_LEN + jnp.arange(ts))

    xp = x_prompt.reshape(tp, d)
    xs = x_sample.reshape(bs * ts, d)
    outs = {k: [] for k in ("p_wkv", "p_shift", "p_k", "p_v", "s_wkv", "s_shift", "s_k", "s_v")}
    for l in range(depth):
        xp = _ffn(xp, g_ff1, ff1, l, tm=512, tf=_FF_TILE)
        xs = _ffn(xs, g_ff1, ff1, l, tm=512, tf=_FF_TILE)

        pp = _inproj(xp, g_mix, w_in_b, l, tm=1024, tn=1024)
        ps = _inproj(xs, g_mix, w_in_b, l, tm=512, tn=1024)
        pp3 = pp.reshape(bp, tp, NP_COLS)
        ps3 = ps.reshape(bs, ts, NP_COLS)

        ya_p, wkv_p = _rwkv(pp3, zero_shift, zero_state, vec, mu_l, wl1, wl2, l, sb=1, tb=2 * CHUNK, c=CHUNK)
        ya_s, wkv_s = _rwkv(ps3, shift_s[l], state_wkv[l], vec, mu_l, wl1, wl2, l, sb=2, tb=ts, c=ts)

        yb_p, kr_p = _attn_band(pp, cos_p, sin_p, gq, gk, sinks, l, tq=WINDOW)
        yb_s, kr_s = _attn_cached(ps3, ck, cv, cos_s, sin_s, gq, gk, sinks, l, sb=8)

        xp = _outproj(xp, ya_p.reshape(tp, A_WIDTH), yb_p, pp, wa, wb, wo, l, tm=256)
        xs = _outproj(xs, ya_s.reshape(bs * ts, A_WIDTH), yb_s.reshape(bs * ts, B_Q), ps, wa, wb, wo, l, tm=256)

        xp = _ffn(xp, g_ff2, ff2, l, tm=512, tf=_FF_TILE)
        xs = _ffn(xs, g_ff2, ff2, l, tm=512, tf=_FF_TILE)

        vcol = _COL_KV + B_KV
        outs["p_wkv"].append(wkv_p)
        outs["p_shift"].append(_shift_row_to_a(pp3[:, -1, :]))
        outs["p_k"].append(kr_p[-WINDOW:].reshape(bp, WINDOW, B_KV_HEADS, HEAD_DIM))
        outs["p_v"].append(pp3[:, -WINDOW:, vcol:vcol + B_KV].astype(F32).reshape(bp, WINDOW, B_KV_HEADS, HEAD_DIM))
        outs["s_wkv"].append(wkv_s)
        outs["s_shift"].append(_shift_row_to_a(ps3[:, -1, :]))
        outs["s_k"].append(kr_s.reshape(bs, ts, B_KV_HEADS, HEAD_DIM))
        outs["s_v"].append(ps3[:, :, vcol:vcol + B_KV].astype(F32).reshape(bs, ts, B_KV_HEADS, HEAD_DIM))

    return (xp.reshape(bp, tp, d), xs.reshape(bs, ts, d),
            jnp.stack(outs["p_wkv"]), jnp.stack(outs["p_shift"]), jnp.stack(outs["p_k"]), jnp.stack(outs["p_v"]),
            jnp.stack(outs["s_wkv"]), jnp.stack(outs["s_shift"]), jnp.stack(outs["s_k"]), jnp.stack(outs["s_v"]))
```

```python
import functools

import jax
import jax.numpy as jnp
from jax import lax
from jax.experimental import pallas as pl
from jax.experimental.pallas import tpu as pltpu

F32 = jnp.float32
BF16 = jnp.bfloat16

HEAD_DIM = 64
A_WIDTH = 1024
A_HEADS = A_WIDTH // HEAD_DIM
DECAY_LORA = 64
ICLR_LORA = 64
GATE_LORA = 160
GN_EPS = 64e-5
RMS_EPS = 1e-6
B_HEADS = 16
B_KV_HEADS = 4
B_GROUP = B_HEADS // B_KV_HEADS
B_Q = B_HEADS * HEAD_DIM
B_KV = B_KV_HEADS * HEAD_DIM
CHUNK = 64
WINDOW = 128
ROPE_THETA = 10000.0
PAST_LEN = 1024
D_MODEL = 2048
A_COLS = 3 * A_WIDTH + DECAY_LORA + ICLR_LORA + GATE_LORA

LANES = 128
BF16_ROWS = 16
VMEM_LIMIT_BYTES = 56 * 1024 * 1024

_COL_R = 0
_COL_K = _COL_R + A_WIDTH
_COL_V = _COL_K + A_WIDTH
_RKV_W = 3 * A_WIDTH
_COL_Q = _COL_V + A_WIDTH
_COL_GA = _COL_Q + B_Q
_COL_GB = _COL_GA + D_MODEL
_COL_LORA = _COL_GB + D_MODEL
_LORA_W = 512
_COL_KV = _COL_LORA + _LORA_W
_KV_W = 2 * B_KV
NP_COLS = _COL_KV + _KV_W
_LORA_G0 = LANES
_LORA_GW = 2 * LANES
PAIRS = A_HEADS // 2
_VEC_ROWS = 16


def _cparams(n_axes):
    return pltpu.CompilerParams(dimension_semantics=("arbitrary",) * n_axes,
                                vmem_limit_bytes=VMEM_LIMIT_BYTES)


def _dot(a, b):
    return jnp.dot(a.astype(BF16), b.astype(BF16), preferred_element_type=F32)


def _dot_nt(a, b):
    return lax.dot_general(a.astype(BF16), b.astype(BF16), (((1,), (1,)), ((), ())),
                           preferred_element_type=F32)


def _dot_tn(a, b):
    return lax.dot_general(a.astype(BF16), b.astype(BF16), (((0,), (0,)), ((), ())),
                           preferred_element_type=F32)


def _split3(x):
    hi = x.astype(BF16)
    r1 = x - hi.astype(F32)
    mid = r1.astype(BF16)
    lo = (r1 - mid.astype(F32)).astype(BF16)
    return hi, mid, lo


def _rms_rows(x, gain):
    ms = jnp.mean(x * x, axis=-1, keepdims=True)
    return x * lax.rsqrt(ms + RMS_EPS) * gain


def _single_buffered(shape, index_map):
    return pl.BlockSpec(shape, index_map, pipeline_mode=pl.Buffered(1))


def _ffn_kernel(x_ref, g_ref, wg_ref, wu_ref, wd_ref, wgt_ref, wut_ref, wdt_ref, o_ref, h_ref, *, n_full):
    f = pl.program_id(1)

    @pl.when(f == 0)
    def _():
        x = x_ref[...]
        h_ref[...] = _rms_rows(x, g_ref[...]).astype(BF16)
        o_ref[...] = x

    def accumulate(wg, wu, wd):
        h = h_ref[...]
        gate = jnp.dot(h, wg[...], preferred_element_type=F32)
        up = jnp.dot(h, wu[...], preferred_element_type=F32)
        act = (0.5 * gate * jax.nn.sigmoid(gate) * up).astype(BF16)
        o_ref[...] += jnp.dot(act, wd[...], preferred_element_type=F32)

    @pl.when(f < n_full)
    def _():
        accumulate(wg_ref, wu_ref, wd_ref)

    @pl.when(f == n_full)
    def _():
        accumulate(wgt_ref, wut_ref, wdt_ref)


def _ffn(x, gain, w, layer, *, tm, tf):
    wg, wu, wd, wgt, wut, wdt = w
    t, d = x.shape
    n_full = wg.shape[2] // tf
    tail = wgt.shape[2]
    last = n_full - 1
    return pl.pallas_call(
        functools.partial(_ffn_kernel, n_full=n_full),
        grid=(t // tm, n_full + 1),
        in_specs=[_single_buffered((tm, d), lambda i, f: (i, 0)),
                  pl.BlockSpec((None, 1, d), lambda i, f: (layer, 0, 0)),
                  pl.BlockSpec((None, d, tf), lambda i, f: (layer, 0, jnp.minimum(f, last))),
                  pl.BlockSpec((None, d, tf), lambda i, f: (layer, 0, jnp.minimum(f, last))),
                  pl.BlockSpec((None, tf, d), lambda i, f: (layer, jnp.minimum(f, last), 0)),
                  _single_buffered((None, d, tail), lambda i, f: (layer, 0, 0)),
                  _single_buffered((None, d, tail), lambda i, f: (layer, 0, 0)),
                  _single_buffered((None, tail, d), lambda i, f: (layer, 0, 0))],
        out_specs=pl.BlockSpec((tm, d), lambda i, f: (i, 0)),
        out_shape=jax.ShapeDtypeStruct((t, d), F32),
        scratch_shapes=[pltpu.VMEM((tm, d), BF16)],
        compiler_params=_cparams(2),
        name="ffn",
    )(x, gain, wg, wu, wd, wgt, wut, wdt)


def _ffn_weights(gate, up, down, tf):
    d_ff = gate.shape[2]
    n_full = d_ff // tf
    assert d_ff % tf != 0 and (d_ff - n_full * tf) % LANES == 0
    g, u, dn = gate.astype(BF16), up.astype(BF16), down.astype(BF16)
    return g, u, dn, g[:, :, n_full * tf:], u[:, :, n_full * tf:], dn[:, n_full * tf:, :]


def _inproj_kernel(x_ref, g_ref, w_ref, o_ref, h_ref):
    @pl.when(pl.program_id(1) == 0)
    def _():
        h_ref[...] = _rms_rows(x_ref[...], g_ref[...]).astype(BF16)

    o_ref[...] = jnp.dot(h_ref[...], w_ref[...], preferred_element_type=F32).astype(o_ref.dtype)


def _inproj(x, gain, w, layer, *, tm, tn):
    t, d = x.shape
    n = w.shape[2]
    return pl.pallas_call(
        _inproj_kernel,
        grid=(t // tm, n // tn),
        in_specs=[pl.BlockSpec((tm, d), lambda i, j: (i, 0)),
                  pl.BlockSpec((None, 1, d), lambda i, j: (layer, 0, 0)),
                  pl.BlockSpec((None, d, tn), lambda i, j: (layer, 0, j))],
        out_specs=pl.BlockSpec((tm, tn), lambda i, j: (i, j)),
        out_shape=jax.ShapeDtypeStruct((t, n), BF16),
        scratch_shapes=[pltpu.VMEM((tm, d), BF16)],
        compiler_params=_cparams(2),
        name="inproj",
    )(x, gain, w)


def _seg_sum(x, first):
    s0 = jnp.sum(jnp.where(first, x, 0.0), axis=1, keepdims=True)
    s1 = jnp.sum(jnp.where(first, 0.0, x), axis=1, keepdims=True)
    return jnp.where(first, s0, s1)


def _bd(x):
    w = x.shape[1]
    first = lax.broadcasted_iota(jnp.int32, (1, w), 1) < (w // 2)
    zero = jnp.zeros_like(x)
    return jnp.concatenate([jnp.where(first, x, zero), jnp.where(first, zero, x)], axis=0)


def _diag_blocks(m):
    n = m.shape[0] // 2
    first = lax.broadcasted_iota(jnp.int32, (1, 2 * n), 1) < n
    return jnp.where(first, m[0:n], m[n:2 * n])


def _rwkv_kernel(rkv_ref, l_ref, rkvp_ref, lp_ref, srkv_ref, sl_ref, vec_ref, mul_ref, wl1_ref, wl2_ref,
                 s0_ref, y_ref, sout_ref, z_ref, *, sb, tb, c, nb):
    b = pl.program_id(1)
    nc = tb // c
    hd = HEAD_DIM

    vec = vec_ref[...]
    mu_rkv = jnp.concatenate([vec[0:1], vec[1:2], vec[2:3]], axis=1)
    mu_l = mul_ref[...]

    lane = lax.broadcasted_iota(jnp.int32, (1, LANES), 1)
    first = lane < hd
    row_t = lax.broadcasted_iota(jnp.int32, (tb, 1), 0)
    ri = lax.broadcasted_iota(jnp.int32, (c, 2 * c), 0)
    ci = lax.broadcasted_iota(jnp.int32, (c, 2 * c), 1) % c
    lower_incl = ri >= ci
    lower_strict = ri > ci
    eye_p = jnp.where(ri == ci, 1.0, 0.0).astype(F32)
    rt_ = lax.broadcasted_iota(jnp.int32, (c, c), 0)
    ct_ = lax.broadcasted_iota(jnp.int32, (c, c), 1)
    tri = jnp.where(rt_ >= ct_, 1.0, 0.0).astype(BF16)
    rj = lax.broadcasted_iota(jnp.int32, (hd, LANES), 0)
    cj = lax.broadcasted_iota(jnp.int32, (hd, LANES), 1) % hd
    eye_h = rj == cj

    @pl.when(b == 0)
    def _():
        for s in range(sb):
            for p in range(PAIRS):
                z_ref[s, p] = jnp.concatenate([s0_ref[s, 2 * p].T, s0_ref[s, 2 * p + 1].T], axis=1)

    def shifted(ref, pref, sref, s, mu):
        raw = ref[s].astype(F32)
        before = jnp.where(b == 0, sref[s], pref[s][BF16_ROWS - 1:BF16_ROWS, :].astype(F32))
        prev = jnp.where(row_t == 0, before, pltpu.roll(raw, 1, 0))
        return raw + (prev - raw) * mu

    pre = {}
    for s in range(sb):
        x3 = shifted(rkv_ref, rkvp_ref, srkv_ref, s, mu_rkv)
        lx = shifted(l_ref, lp_ref, sl_ref, s, mu_l)
        l01 = lx[:, 0:LANES]
        z01 = jnp.where(first, jnp.tanh(l01), l01)
        wa = _dot(z01, wl1_ref[...])
        gate = _dot(jax.nn.sigmoid(lx[:, _LORA_G0:_LORA_G0 + _LORA_GW]), wl2_ref[...])
        for p in range(PAIRS):
            ps = slice(p * LANES, (p + 1) * LANES)
            w0, a0, k_k, k_a, r_k = vec[3:4, ps], vec[4:5, ps], vec[5:6, ps], vec[6:7, ps], vec[7:8, ps]
            rx = x3[:, _COL_R + p * LANES:_COL_R + (p + 1) * LANES]
            kx = x3[:, _COL_K + p * LANES:_COL_K + (p + 1) * LANES]
            vx = x3[:, _COL_V + p * LANES:_COL_V + (p + 1) * LANES]
            w_pre = wa[:, ps] + w0
            a_pre = wa[:, A_WIDTH + p * LANES:A_WIDTH + (p + 1) * LANES] + a0
            softplus = jnp.maximum(-w_pre, 0.0) + jnp.log1p(jnp.exp(-jnp.abs(w_pre)))
            logd = -jnp.exp(-softplus - 0.5)
            a = jax.nn.sigmoid(a_pre)
            kxk = kx * k_k
            kk = kxk / jnp.maximum(jnp.sqrt(_seg_sum(kxk * kxk, first)), 1e-12)
            kp = kx * (1.0 + (a - 1.0) * k_a)
            pre[(s, p)] = dict(r=rx, kk=kk, kp=kp, bb=kk * a, v=vx, logd=logd, gate=gate[:, ps],
                               bonus=_seg_sum(rx * kp * r_k, first) * vx)

    units = [(s, ch, p) for ch in range(nc) for s in range(sb) for p in range(PAIRS)]

    def rows(name, u):
        s, ch, p = units[u]
        return pre[(s, p)][name][ch * c:(ch + 1) * c]

    nu = len(units)
    cin = []
    for u in range(nu):
        hi, mid, lo = _split3(rows("logd", u))
        cs = jnp.dot(tri, jnp.concatenate([hi, mid, lo], axis=1), preferred_element_type=F32)
        cin.append(cs[:, 0:LANES] + cs[:, LANES:2 * LANES] + cs[:, 2 * LANES:3 * LANES])
    kk_t, r_t, k_d, b_d, p_end, g = [], [], [], [], [], []
    for u in range(nu):
        ld = rows("logd", u)
        c_end = cin[u][c - 1:c, :]
        e_inv = jnp.exp(-cin[u])
        e_dec = jnp.exp(c_end - cin[u])
        kk_t.append((rows("kk", u) * jnp.exp(cin[u] - ld)).astype(BF16))
        r_t.append(rows("r", u) * jnp.exp(cin[u]))
        k_d.append((rows("kp", u) * e_dec).astype(BF16))
        b_d.append((rows("bb", u) * e_dec).astype(BF16))
        p_end.append(jnp.exp(c_end))
        lhs = jnp.concatenate([kk_t[u], r_t[u].astype(BF16)], axis=0)
        rhs = jnp.concatenate([_bd((rows("bb", u) * e_inv).astype(BF16)),
                               _bd((rows("kp", u) * e_inv).astype(BF16))], axis=0)
        g.append(_dot_nt(lhs, rhs))
    m_ab = [jnp.where(lower_strict, g[u][0:c, 0:2 * c], 0.0).astype(BF16) for u in range(nu)]
    a_rb = [jnp.where(lower_incl, g[u][c:2 * c, 0:2 * c], 0.0).astype(BF16) for u in range(nu)]
    m_ak = [jnp.where(lower_strict, g[u][0:c, 2 * c:4 * c], 0.0).astype(BF16) for u in range(nu)]
    a_rk = [jnp.where(lower_incl, g[u][c:2 * c, 2 * c:4 * c], 0.0).astype(BF16) for u in range(nu)]
    v_bd = [_bd(rows("v", u).astype(BF16)) for u in range(nu)]
    x = [eye_p - m_ab[u].astype(F32) for u in range(nu)]
    pw = [_dot(m_ab[u], _bd(m_ab[u])) for u in range(nu)]
    k = 2
    while 2 * k < c:
        both = [_dot(jnp.concatenate([x[u].astype(BF16), pw[u].astype(BF16)], axis=0), _bd(pw[u].astype(BF16)))
                for u in range(nu)]
        x = [x[u] + both[u][0:c] for u in range(nu)]
        pw = [both[u][c:2 * c] for u in range(nu)]
        k *= 2
    t_inv = [(x[u] + _dot(x[u], _bd(pw[u].astype(BF16)))).astype(BF16) for u in range(nu)]
    mv = [_dot(m_ak[u], v_bd[u]) for u in range(nu)]
    tw = [_dot(t_inv[u], jnp.concatenate([_bd(kk_t[u]), _bd(mv[u].astype(BF16))], axis=1)).astype(BF16)
          for u in range(nu)]
    bt = [_dot_tn(b_d[u], tw[u]) for u in range(nu)]
    kv = [_dot_tn(k_d[u], rows("v", u)) for u in range(nu)]
    ab = [_dot(a_rb[u], jnp.concatenate([_bd(tw[u][:, 0:LANES]), _bd(tw[u][:, LANES:2 * LANES])], axis=1))
          for u in range(nu)]
    av = [_dot(a_rk[u], v_bd[u]) for u in range(nu)]
    lhs_z, u_c, y_i = [], [], []
    for u in range(nu):
        a_c = jnp.where(eye_h, p_end[u], 0.0) - _diag_blocks(bt[u][:, 0:LANES])
        q_e = r_t[u] - ab[u][:, 0:LANES]
        lhs_z.append(jnp.concatenate([a_c, q_e], axis=0).astype(BF16))
        u_c.append(_diag_blocks(kv[u]) - _diag_blocks(bt[u][:, LANES:2 * LANES]))
        y_i.append(av[u] - ab[u][:, LANES:2 * LANES])
    z = {(s, p): z_ref[s, p] for s in range(sb) for p in range(PAIRS)}
    per_level = sb * PAIRS
    for ch in range(nc):
        level = range(ch * per_level, (ch + 1) * per_level)
        both = {u: _dot(lhs_z[u], _bd(z[units[u][0], units[u][2]].astype(BF16))) for u in level}
        ys = {u: both[u][hd:hd + c] + y_i[u] for u in level}
        for u in level:
            z[units[u][0], units[u][2]] = both[u][0:hd] + u_c[u]
        means = {u: _seg_sum(ys[u], first) * (1.0 / hd) for u in level}
        ycs = {u: ys[u] - means[u] for u in level}
        variances = {u: _seg_sum(ycs[u] * ycs[u], first) * (1.0 / hd) for u in level}
        for u in level:
            s, _, p = units[u]
            ps = slice(p * LANES, (p + 1) * LANES)
            yn = ycs[u] * lax.rsqrt(variances[u] + GN_EPS)
            out = (yn * vec[8:9, ps] + vec[9:10, ps] + rows("bonus", u)) * rows("gate", u)
            y_ref[s, ch * c:(ch + 1) * c, ps] = out.astype(y_ref.dtype)
    for s in range(sb):
        for p in range(PAIRS):
            z_ref[s, p] = z[(s, p)]

    @pl.when(b == nb - 1)
    def _():
        for s in range(sb):
            for p in range(PAIRS):
                sout_ref[s, 2 * p] = z[(s, p)][:, 0:hd].T
                sout_ref[s, 2 * p + 1] = z[(s, p)][:, hd:2 * hd].T


def _rwkv(p3, shiftp, s0, vec, mu_l, wl1, wl2, layer, *, sb, tb, c):
    s, t, _ = p3.shape
    lb = _COL_LORA // _LORA_W
    tpb = tb // BF16_ROWS
    nb = t // tb

    def prev_tile(b):
        return jnp.maximum(b * tpb - 1, 0)

    in_specs = [
        pl.BlockSpec((sb, tb, _RKV_W), lambda i, b: (i, b, 0)),
        pl.BlockSpec((sb, tb, _LORA_W), lambda i, b: (i, b, lb)),
        pl.BlockSpec((sb, BF16_ROWS, _RKV_W), lambda i, b: (i, prev_tile(b), 0)),
        pl.BlockSpec((sb, BF16_ROWS, _LORA_W), lambda i, b: (i, prev_tile(b), lb)),
        pl.BlockSpec((sb, 1, _RKV_W), lambda i, b: (i, 0, 0)),
        pl.BlockSpec((sb, 1, _LORA_W), lambda i, b: (i, 0, _RKV_W // _LORA_W)),
        pl.BlockSpec((None, _VEC_ROWS, A_WIDTH), lambda i, b: (layer, 0, 0)),
        pl.BlockSpec((None, 1, _LORA_W), lambda i, b: (layer, 0, 0)),
        pl.BlockSpec((None, LANES, 2 * A_WIDTH), lambda i, b: (layer, 0, 0)),
        pl.BlockSpec((None, _LORA_GW, A_WIDTH), lambda i, b: (layer, 0, 0)),
        pl.BlockSpec((sb, A_HEADS, HEAD_DIM, HEAD_DIM), lambda i, b: (i, 0, 0, 0)),
    ]
    out_specs = [
        pl.BlockSpec((sb, tb, A_WIDTH), lambda i, b: (i, b, 0)),
        pl.BlockSpec((sb, A_HEADS, HEAD_DIM, HEAD_DIM), lambda i, b: (i, 0, 0, 0)),
    ]
    return pl.pallas_call(
        functools.partial(_rwkv_kernel, sb=sb, tb=tb, c=c, nb=nb),
        grid=(s // sb, nb),
        in_specs=in_specs,
        out_specs=out_specs,
        out_shape=[jax.ShapeDtypeStruct((s, t, A_WIDTH), BF16),
                   jax.ShapeDtypeStruct((s, A_HEADS, HEAD_DIM, HEAD_DIM), F32)],
        scratch_shapes=[pltpu.VMEM((sb, PAIRS, HEAD_DIM, LANES), F32)],
        compiler_params=_cparams(2),
        name="rwkv",
    )(p3, p3, p3, p3, shiftp, shiftp, vec, mu_l, wl1, wl2, s0)


def _norm_rope(slabs):
    lane = lax.broadcasted_iota(jnp.int32, (1, LANES), 1)
    first = lane < HEAD_DIM
    low_half = (lane % HEAD_DIM) < (HEAD_DIM // 2)
    tiles = [(n, x[:, j * LANES:(j + 1) * LANES]) for n, (x, _, _, _) in enumerate(slabs)
             for j in range(x.shape[1] // LANES)]
    sq = [xs * xs for _, xs in tiles]
    s0 = [jnp.sum(jnp.where(first, s, 0.0), axis=1, keepdims=True) for s in sq]
    s1 = [jnp.sum(jnp.where(first, 0.0, s), axis=1, keepdims=True) for s in sq]
    xn = [xs * lax.rsqrt(jnp.where(first, a, b) * (1.0 / HEAD_DIM) + RMS_EPS) * slabs[n][1]
          for (n, xs), a, b in zip(tiles, s0, s1)]
    up = [pltpu.roll(v, LANES - HEAD_DIM // 2, 1) for v in xn]
    down = [pltpu.roll(v, HEAD_DIM // 2, 1) for v in xn]
    out = [v * slabs[n][2] + jnp.where(low_half, u, w) * slabs[n][3]
           for (n, _), v, u, w in zip(tiles, xn, up, down)]
    res, o = [], 0
    for x, _, _, _ in slabs:
        k = x.shape[1] // LANES
        res.append(jnp.concatenate(out[o:o + k], axis=1) if k > 1 else out[o])
        o += k
    return res


def _sink_softmax_all(scores, sinks):
    ms = [jnp.maximum(jnp.max(s, axis=-1, keepdims=True), k) for s, k in zip(scores, sinks)]
    ps = [jnp.exp(s - m) for s, m in zip(scores, ms)]
    ds = [jnp.sum(p, axis=-1, keepdims=True) + jnp.exp(k - m) for p, k, m in zip(ps, sinks, ms)]
    return [(p / d).astype(BF16) for p, d in zip(ps, ds)]


def _sink_column(sink_ref, heads, rows):
    return jnp.concatenate([jnp.full((rows, 1), sink_ref[h], F32) for h in heads], axis=0)


_QK_SCALE = HEAD_DIM ** -0.5


def _band_kernel(sink_ref, q_ref, kv_ref, kvp_ref, cq_ref, sq_ref, cp_ref, sp_ref, gq_ref, gk_ref,
                 y_ref, ko_ref, *, tq):
    i = pl.program_id(0)
    hd = HEAD_DIM
    band = WINDOW + CHUNK
    kv = kv_ref[...]
    kvp = kvp_ref[...]
    q, k_cur, k_prev = _norm_rope([
        (q_ref[...].astype(F32), gq_ref[...], cq_ref[...], sq_ref[...]),
        (kv[:, 0:B_KV].astype(F32), gk_ref[...], cq_ref[...], sq_ref[...]),
        (kvp[:, 0:B_KV].astype(F32), gk_ref[...], cp_ref[...], sp_ref[...])])
    q = (q * _QK_SCALE).astype(BF16)
    ko_ref[...] = k_cur
    k_all = jnp.concatenate([k_prev.astype(BF16), k_cur.astype(BF16)], axis=0)
    v_all = jnp.concatenate([kvp[:, B_KV:2 * B_KV], kv[:, B_KV:2 * B_KV]], axis=0)
    k_g = [k_all[:, g * hd:(g + 1) * hd] for g in range(B_KV_HEADS)]
    v_g = [v_all[:, g * hd:(g + 1) * hd] for g in range(B_KV_HEADS)]
    col = lax.broadcasted_iota(jnp.int32, (1, band), 1)
    sink_g = [_sink_column(sink_ref, range(g * B_GROUP, (g + 1) * B_GROUP), CHUNK) for g in range(B_KV_HEADS)]

    jobs = [(k0, g) for k0 in range(0, tq, CHUNK) for g in range(B_KV_HEADS)]
    q_rows = [jnp.concatenate([q[k0:k0 + CHUNK, (g * B_GROUP + j) * hd:(g * B_GROUP + j + 1) * hd]
                               for j in range(B_GROUP)], axis=0) for k0, g in jobs]
    scores = [_dot_nt(qr, k_g[g][k0:k0 + band]) for (k0, g), qr in zip(jobs, q_rows)]
    scores = [jnp.where((col >= WINDOW - k0) | (i > 0), s, -1e30) for (k0, g), s in zip(jobs, scores)]
    probs = _sink_softmax_all(scores, [sink_g[g] for k0, g in jobs])
    outs = {job: _dot(p, v_g[job[1]][job[0]:job[0] + band]) for job, p in zip(jobs, probs)}
    for k0 in range(0, tq, CHUNK):
        heads = [outs[(k0, g)][j * CHUNK:(j + 1) * CHUNK] for g in range(B_KV_HEADS) for j in range(B_GROUP)]
        y_ref[k0:k0 + CHUNK, :] = jnp.concatenate(heads, axis=1).astype(y_ref.dtype)


def _attn_band(p2, cos, sin_signed, gq, gk, sinks, layer, *, tq):
    t = p2.shape[0]
    qb, kvb = _COL_Q // B_Q, _COL_KV // _KV_W
    wpb = tq // WINDOW

    def prev_idx(i):
        return jnp.maximum(i * wpb - 1, 0)

    return pl.pallas_call(
        functools.partial(_band_kernel, tq=tq),
        grid=(t // tq,),
        in_specs=[pl.BlockSpec(memory_space=pltpu.SMEM),
                  pl.BlockSpec((tq, B_Q), lambda i: (i, qb)),
                  pl.BlockSpec((tq, _KV_W), lambda i: (i, kvb)),
                  pl.BlockSpec((WINDOW, _KV_W), lambda i: (prev_idx(i), kvb)),
                  pl.BlockSpec((tq, LANES), lambda i: (i, 0)),
                  pl.BlockSpec((tq, LANES), lambda i: (i, 0)),
                  pl.BlockSpec((WINDOW, LANES), lambda i: (prev_idx(i), 0)),
                  pl.BlockSpec((WINDOW, LANES), lambda i: (prev_idx(i), 0)),
                  pl.BlockSpec((None, 1, LANES), lambda i: (layer, 0, 0)),
                  pl.BlockSpec((None, 1, LANES), lambda i: (layer, 0, 0))],
        out_specs=[pl.BlockSpec((tq, B_Q), lambda i: (i, 0)),
                   pl.BlockSpec((tq, B_KV), lambda i: (i, 0))],
        out_shape=[jax.ShapeDtypeStruct((t, B_Q), BF16),
                   jax.ShapeDtypeStruct((t, B_KV), F32)],
        compiler_params=_cparams(1),
        name="attn_band",
    )(sinks[layer], p2, p2, p2, cos, sin_signed, cos, sin_signed, gq, gk)


def _cached_kernel(sink_ref, q_ref, kv_ref, ck_ref, cv_ref, c_ref, s_ref, gq_ref, gk_ref,
                   y_ref, ko_ref, *, sb):
    hd = HEAD_DIM
    tn = q_ref.shape[1]
    jobs = [(s, g) for s in range(sb) for g in range(B_KV_HEADS)]
    q_rows, k_all, v_all = {}, {}, {}
    cos = jnp.concatenate([c_ref[...]] * sb, axis=0)
    sin = jnp.concatenate([s_ref[...]] * sb, axis=0)
    q_cat = jnp.concatenate([q_ref[s] for s in range(sb)], axis=0).astype(F32)
    kv_cat = jnp.concatenate([kv_ref[s] for s in range(sb)], axis=0)
    q_cat, k_cat_new = _norm_rope([(q_cat, gq_ref[...], cos, sin),
                                   (kv_cat[:, 0:B_KV].astype(F32), gk_ref[...], cos, sin)])
    q_cat = (q_cat * _QK_SCALE).astype(BF16)
    for s in range(sb):
        q = q_cat[s * tn:(s + 1) * tn]
        k_new = k_cat_new[s * tn:(s + 1) * tn]
        ko_ref[s] = k_new
        k_cat = jnp.concatenate([ck_ref[s].astype(BF16), k_new.astype(BF16)], axis=0)
        v_cat = jnp.concatenate([cv_ref[s].astype(BF16), kv_cat[s * tn:(s + 1) * tn, B_KV:2 * B_KV]], axis=0)
        for g in range(B_KV_HEADS):
            q_rows[(s, g)] = jnp.concatenate(
                [q[:, (g * B_GROUP + j) * hd:(g * B_GROUP + j + 1) * hd] for j in range(B_GROUP)], axis=0)
            k_all[(s, g)] = k_cat[:, g * hd:(g + 1) * hd]
            v_all[(s, g)] = v_cat[:, g * hd:(g + 1) * hd]
    scores = [_dot_nt(q_rows[j], k_all[j]) for j in jobs]
    sink_g = [_sink_column(sink_ref, range(g * B_GROUP, (g + 1) * B_GROUP), tn) for g in range(B_KV_HEADS)]
    probs = _sink_softmax_all(scores, [sink_g[g] for s, g in jobs])
    outs = {j: _dot(p, v_all[j]) for j, p in zip(jobs, probs)}
    for s in range(sb):
        heads = [outs[(s, g)][j * tn:(j + 1) * tn] for g in range(B_KV_HEADS) for j in range(B_GROUP)]
        y_ref[s] = jnp.concatenate(heads, axis=1).astype(y_ref.dtype)


def _attn_cached(p3, ck, cv, cos, sin_signed, gq, gk, sinks, layer, *, sb):
    s, t, _ = p3.shape
    w = ck.shape[2]
    qb, kvb = _COL_Q // B_Q, _COL_KV // _KV_W
    return pl.pallas_call(
        functools.partial(_cached_kernel, sb=sb),
        grid=(s // sb,),
        in_specs=[pl.BlockSpec(memory_space=pltpu.SMEM),
                  pl.BlockSpec((sb, t, B_Q), lambda i: (i, 0, qb)),
                  pl.BlockSpec((sb, t, _KV_W), lambda i: (i, 0, kvb)),
                  pl.BlockSpec((None, sb, w, B_KV), lambda i: (layer, i, 0, 0)),
                  pl.BlockSpec((None, sb, w, B_KV), lambda i: (layer, i, 0, 0)),
                  pl.BlockSpec((t, LANES), lambda i: (0, 0)),
                  pl.BlockSpec((t, LANES), lambda i: (0, 0)),
                  pl.BlockSpec((None, 1, LANES), lambda i: (layer, 0, 0)),
                  pl.BlockSpec((None, 1, LANES), lambda i: (layer, 0, 0))],
        out_specs=[pl.BlockSpec((sb, t, B_Q), lambda i: (i, 0, 0)),
                   pl.BlockSpec((sb, t, B_KV), lambda i: (i, 0, 0))],
        out_shape=[jax.ShapeDtypeStruct((s, t, B_Q), BF16),
                   jax.ShapeDtypeStruct((s, t, B_KV), F32)],
        compiler_params=_cparams(1),
        name="attn_cached",
    )(sinks[layer], p3, p3, ck, cv, cos, sin_signed, gq, gk)


def _outproj_kernel(x_ref, ya_ref, yb_ref, ga_ref, gb_ref, wa_ref, wb_ref, wo_ref, o_ref):
    ua = jnp.dot(ya_ref[...], wa_ref[...], preferred_element_type=F32)
    ub = jnp.dot(yb_ref[...], wb_ref[...], preferred_element_type=F32)
    merged = (jax.nn.sigmoid(ga_ref[...].astype(F32)) * ua + jax.nn.sigmoid(gb_ref[...].astype(F32)) * ub)
    o_ref[...] = x_ref[...] + jnp.dot(merged.astype(BF16), wo_ref[...], preferred_element_type=F32)


def _outproj(x, ya, yb, p2, wa, wb, wo, layer, *, tm):
    t, d = x.shape
    gab, gbb = _COL_GA // d, _COL_GB // d
    return pl.pallas_call(
        _outproj_kernel,
        grid=(t // tm,),
        in_specs=[pl.BlockSpec((tm, d), lambda i: (i, 0)),
                  pl.BlockSpec((tm, A_WIDTH), lambda i: (i, 0)),
                  pl.BlockSpec((tm, B_Q), lambda i: (i, 0)),
                  pl.BlockSpec((tm, d), lambda i: (i, gab)),
                  pl.BlockSpec((tm, d), lambda i: (i, gbb)),
                  _single_buffered((None, A_WIDTH, d), lambda i: (layer, 0, 0)),
                  _single_buffered((None, B_Q, d), lambda i: (layer, 0, 0)),
                  _single_buffered((None, d, d), lambda i: (layer, 0, 0))],
        out_specs=pl.BlockSpec((tm, d), lambda i: (i, 0)),
        out_shape=jax.ShapeDtypeStruct((t, d), F32),
        compiler_params=_cparams(1),
        name="outproj",
    )(x, ya, yb, p2, p2, wa, wb, wo)


def _a_pieces(a):
    o = 0
    out = []
    for w in (A_WIDTH, DECAY_LORA, A_WIDTH, A_WIDTH, ICLR_LORA, GATE_LORA):
        out.append(a[..., o:o + w])
        o += w
    return out


def _pad_last(w, n):
    return jnp.pad(w, [(0, 0)] * (w.ndim - 1) + [(0, n - w.shape[-1])])


def _regroup_a(a):
    r, w_lo, k, v, a_lo, g_lo = _a_pieces(a)
    lora = _pad_last(jnp.concatenate([w_lo, a_lo, g_lo], axis=-1), _LORA_W)
    return jnp.concatenate([r, k, v], axis=-1), lora


def _regroup_w_in(w):
    o = A_COLS
    wq = w[..., o:o + B_Q]
    wk = w[..., o + B_Q:o + B_Q + B_KV]
    wv = w[..., o + B_Q + B_KV:o + B_Q + 2 * B_KV]
    o += B_Q + 2 * B_KV
    wga = w[..., o:o + D_MODEL]
    wgb = w[..., o + D_MODEL:o + 2 * D_MODEL]
    rkv, lora = _regroup_a(w[..., :A_COLS])
    return jnp.concatenate([rkv, wq, wga, wgb, lora, wk, wv], axis=-1).astype(BF16)


def _shift_row_to_a(p_last):
    p_last = p_last.astype(F32)
    r = p_last[..., _COL_R:_COL_R + A_WIDTH]
    k = p_last[..., _COL_K:_COL_K + A_WIDTH]
    v = p_last[..., _COL_V:_COL_V + A_WIDTH]
    o = _COL_LORA
    w_lo = p_last[..., o:o + DECAY_LORA]
    a_lo = p_last[..., o + DECAY_LORA:o + DECAY_LORA + ICLR_LORA]
    g_lo = p_last[..., o + DECAY_LORA + ICLR_LORA:o + DECAY_LORA + ICLR_LORA + GATE_LORA]
    return jnp.concatenate([r, w_lo, k, v, a_lo, g_lo], axis=-1)


def _lora_weights(decay_w2, iclr_a2, gate_g2):
    top = jnp.concatenate([decay_w2, jnp.zeros_like(decay_w2)], axis=-1)
    bot = jnp.concatenate([jnp.zeros_like(iclr_a2), iclr_a2], axis=-1)
    wl1 = jnp.concatenate([top, bot], axis=-2)
    wl2 = jnp.pad(gate_g2, ((0, 0), (0, _LORA_GW - GATE_LORA), (0, 0)))
    return wl1.astype(BF16), wl2.astype(BF16)


def _rope_tables(pos):
    half = HEAD_DIM // 2
    inv = ROPE_THETA ** (-jnp.arange(half, dtype=F32) / half)
    ang = pos.astype(F32)[:, None] * inv[None, :]
    cos = jnp.cos(ang)
    sin = jnp.sin(ang)
    cos_t = jnp.tile(cos, (1, LANES // half))
    sin_t = jnp.tile(jnp.concatenate([-sin, sin], axis=1), (1, LANES // HEAD_DIM))
    return cos_t, sin_t


_FF_TILE = 512


def kernel(x_prompt, x_sample, cache_k, cache_v, state_wkv, state_shift, norm_ff1, ff1_gate, ff1_up, ff1_down,
           norm_mix, w_in, shift_mu, decay_w0, decay_w2, iclr_a0, iclr_a2, gate_g2, k_k, k_a, r_k, gn_gain,
           gn_bias, q_norm, k_norm, sinks, w_up_a, w_up_b, w_o, norm_ff2, ff2_gate, ff2_up, ff2_down):
    depth = norm_ff1.shape[0]
    bp, tp, d = x_prompt.shape
    bs, ts, _ = x_sample.shape
    assert bp == 1 and d == D_MODEL

    ff1 = _ffn_weights(ff1_gate, ff1_up, ff1_down, _FF_TILE)
    ff2 = _ffn_weights(ff2_gate, ff2_up, ff2_down, _FF_TILE)
    g_ff1 = norm_ff1[:, None, :]
    g_ff2 = norm_ff2[:, None, :]
    g_mix = norm_mix[:, None, :]
    w_in_b = _regroup_w_in(w_in)
    mu_rkv, mu_l = _regroup_a(shift_mu)
    mu_l = mu_l[:, None, :]
    rows = [mu_rkv[:, 0:A_WIDTH], mu_rkv[:, A_WIDTH:2 * A_WIDTH], mu_rkv[:, 2 * A_WIDTH:], decay_w0, iclr_a0,
            k_k, k_a, r_k.reshape(depth, A_WIDTH), gn_gain, gn_bias]
    vec = jnp.stack(rows + [jnp.zeros((depth, A_WIDTH), F32)] * (_VEC_ROWS - len(rows)), axis=1)
    wl1, wl2 = _lora_weights(decay_w2, iclr_a2, gate_g2)
    gq = jnp.tile(q_norm, (1, LANES // HEAD_DIM))[:, None, :]
    gk = jnp.tile(k_norm, (1, LANES // HEAD_DIM))[:, None, :]
    wa = w_up_a.astype(BF16)
    wb = w_up_b.astype(BF16)
    wo = w_o.astype(BF16)
    s_rkv, s_lora = _regroup_a(state_shift)
    shift_s = jnp.concatenate([s_rkv, s_lora], axis=-1)[:, :, None, :]
    zero_shift = jnp.zeros((bp, 1, _RKV_W + _LORA_W), F32)
    zero_state = jnp.zeros((bp, A_HEADS, HEAD_DIM, HEAD_DIM), F32)
    ck = cache_k.reshape(depth, bs, -1, B_KV)
    cv = cache_v.reshape(depth, bs, -1, B_KV)
    cos_p, sin_p = _rope_tables(jnp.arange(tp))
    cos_s, sin_s = _rope_tables(PAST_LEN + jnp.arange(ts))

    xp = x_prompt.reshape(tp, d)
    xs = x_sample.reshape(bs * ts, d)
    outs = {k: [] for k in ("p_wkv", "p_shift", "p_k", "p_v", "s_wkv", "s_shift", "s_k", "s_v")}
    for l in range(depth):
        xp = _ffn(xp, g_ff1, ff1, l, tm=1024, tf=_FF_TILE)
        xs = _ffn(xs, g_ff1, ff1, l, tm=512, tf=_FF_TILE)

        pp = _inproj(xp, g_mix, w_in_b, l, tm=1024, tn=1024)
        ps = _inproj(xs, g_mix, w_in_b, l, tm=512, tn=1024)
        pp3 = pp.reshape(bp, tp, NP_COLS)
        ps3 = ps.reshape(bs, ts, NP_COLS)

        ya_p, wkv_p = _rwkv(pp3, zero_shift, zero_state, vec, mu_l, wl1, wl2, l, sb=1, tb=4 * CHUNK, c=CHUNK)
        ya_s, wkv_s = _rwkv(ps3, shift_s[l], state_wkv[l], vec, mu_l, wl1, wl2, l, sb=2, tb=ts, c=ts)

        yb_p, kr_p = _attn_band(pp, cos_p, sin_p, gq, gk, sinks, l, tq=2 * WINDOW)
        yb_s, kr_s = _attn_cached(ps3, ck, cv, cos_s, sin_s, gq, gk, sinks, l, sb=8)

        xp = _outproj(xp, ya_p.reshape(tp, A_WIDTH), yb_p, pp, wa, wb, wo, l, tm=256)
        xs = _outproj(xs, ya_s.reshape(bs * ts, A_WIDTH), yb_s.reshape(bs * ts, B_Q), ps, wa, wb, wo, l, tm=256)

        xp = _ffn(xp, g_ff2, ff2, l, tm=1024, tf=_FF_TILE)
        xs = _ffn(xs, g_ff2, ff2, l, tm=512, tf=_FF_TILE)

        vcol = _COL_KV + B_KV
        outs["p_wkv"].append(wkv_p)
        outs["p_shift"].append(_shift_row_to_a(pp3[:, -1, :]))
        outs["p_k"].append(kr_p[-WINDOW:].reshape(bp, WINDOW, B_KV_HEADS, HEAD_DIM))
        outs["p_v"].append(pp3[:, -WINDOW:, vcol:vcol + B_KV].astype(F32).reshape(bp, WINDOW, B_KV_HEADS, HEAD_DIM))
        outs["s_wkv"].append(wkv_s)
        outs["s_shift"].append(_shift_row_to_a(ps3[:, -1, :]))
        outs["s_k"].append(kr_s.reshape(bs, ts, B_KV_HEADS, HEAD_DIM))
        outs["s_v"].append(ps3[:, :, vcol:vcol + B_KV].astype(F32).reshape(bs, ts, B_KV_HEADS, HEAD_DIM))

    return (xp.reshape(bp, tp, d), xs.reshape(bs, ts, d),
            jnp.stack(outs["p_wkv"]), jnp.stack(outs["p_shift"]), jnp.stack(outs["p_k"]), jnp.stack(outs["p_v"]),
            jnp.stack(outs["s_wkv"]), jnp.stack(outs["s_shift"]), jnp.stack(outs["s_k"]), jnp.stack(outs["s_v"]))
```

```python
import functools

import jax
import jax.numpy as jnp
from jax import lax
from jax.experimental import pallas as pl
from jax.experimental.pallas import tpu as pltpu

F32 = jnp.float32
BF16 = jnp.bfloat16

HEAD_DIM = 64
A_WIDTH = 1024
A_HEADS = A_WIDTH // HEAD_DIM
DECAY_LORA = 64
ICLR_LORA = 64
GATE_LORA = 160
GN_EPS = 64e-5
RMS_EPS = 1e-6
B_HEADS = 16
B_KV_HEADS = 4
B_GROUP = B_HEADS // B_KV_HEADS
B_Q = B_HEADS * HEAD_DIM
B_KV = B_KV_HEADS * HEAD_DIM
CHUNK = 64
WINDOW = 128
ROPE_THETA = 10000.0
PAST_LEN = 1024
D_MODEL = 2048
A_COLS = 3 * A_WIDTH + DECAY_LORA + ICLR_LORA + GATE_LORA

LANES = 128
BF16_ROWS = 16
VMEM_LIMIT_BYTES = 56 * 1024 * 1024

_COL_R = 0
_COL_K = _COL_R + A_WIDTH
_COL_V = _COL_K + A_WIDTH
_RKV_W = 3 * A_WIDTH
_COL_Q = _COL_V + A_WIDTH
_COL_GA = _COL_Q + B_Q
_COL_GB = _COL_GA + D_MODEL
_COL_LORA = _COL_GB + D_MODEL
_LORA_W = 512
_COL_KV = _COL_LORA + _LORA_W
_KV_W = 2 * B_KV
NP_COLS = _COL_KV + _KV_W
_LORA_G0 = LANES
_LORA_GW = 2 * LANES
PAIRS = A_HEADS // 2
_VEC_ROWS = 16


def _cparams(n_axes):
    return pltpu.CompilerParams(dimension_semantics=("arbitrary",) * n_axes,
                                vmem_limit_bytes=VMEM_LIMIT_BYTES)


def _dot(a, b):
    return jnp.dot(a.astype(BF16), b.astype(BF16), preferred_element_type=F32)


def _dot_nt(a, b):
    return lax.dot_general(a.astype(BF16), b.astype(BF16), (((1,), (1,)), ((), ())),
                           preferred_element_type=F32)


def _dot_tn(a, b):
    return lax.dot_general(a.astype(BF16), b.astype(BF16), (((0,), (0,)), ((), ())),
                           preferred_element_type=F32)


def _split3(x):
    hi = x.astype(BF16)
    r1 = x - hi.astype(F32)
    mid = r1.astype(BF16)
    lo = (r1 - mid.astype(F32)).astype(BF16)
    return hi, mid, lo


def _rms_rows(x, gain):
    ms = jnp.mean(x * x, axis=-1, keepdims=True)
    return x * lax.rsqrt(ms + RMS_EPS) * gain


def _single_buffered(shape, index_map):
    return pl.BlockSpec(shape, index_map, pipeline_mode=pl.Buffered(1))


def _ffn_kernel(x_ref, g_ref, wg_ref, wu_ref, wd_ref, wgt_ref, wut_ref, wdt_ref, o_ref, h_ref, *, n_full):
    f = pl.program_id(1)

    @pl.when(f == 0)
    def _():
        x = x_ref[...]
        h_ref[...] = _rms_rows(x, g_ref[...]).astype(BF16)
        o_ref[...] = x

    def accumulate(wg, wu, wd):
        h = h_ref[...]
        gate = jnp.dot(h, wg[...], preferred_element_type=F32)
        up = jnp.dot(h, wu[...], preferred_element_type=F32)
        act = (0.5 * gate * jax.nn.sigmoid(gate) * up).astype(BF16)
        o_ref[...] += jnp.dot(act, wd[...], preferred_element_type=F32)

    @pl.when(f < n_full)
    def _():
        accumulate(wg_ref, wu_ref, wd_ref)

    @pl.when(f == n_full)
    def _():
        accumulate(wgt_ref, wut_ref, wdt_ref)


def _ffn(x, gain, w, layer, *, tm, tf):
    wg, wu, wd, wgt, wut, wdt = w
    t, d = x.shape
    n_full = wg.shape[2] // tf
    tail = wgt.shape[2]
    last = n_full - 1
    return pl.pallas_call(
        functools.partial(_ffn_kernel, n_full=n_full),
        grid=(t // tm, n_full + 1),
        in_specs=[pl.BlockSpec((tm, d), lambda i, f: (i, 0)),
                  pl.BlockSpec((None, 1, d), lambda i, f: (layer, 0, 0)),
                  pl.BlockSpec((None, d, tf), lambda i, f: (layer, 0, jnp.minimum(f, last))),
                  pl.BlockSpec((None, d, tf), lambda i, f: (layer, 0, jnp.minimum(f, last))),
                  pl.BlockSpec((None, tf, d), lambda i, f: (layer, jnp.minimum(f, last), 0)),
                  _single_buffered((None, d, tail), lambda i, f: (layer, 0, 0)),
                  _single_buffered((None, d, tail), lambda i, f: (layer, 0, 0)),
                  _single_buffered((None, tail, d), lambda i, f: (layer, 0, 0))],
        out_specs=pl.BlockSpec((tm, d), lambda i, f: (i, 0)),
        out_shape=jax.ShapeDtypeStruct((t, d), F32),
        scratch_shapes=[pltpu.VMEM((tm, d), BF16)],
        compiler_params=_cparams(2),
        name="ffn",
    )(x, gain, wg, wu, wd, wgt, wut, wdt)


def _ffn_weights(gate, up, down, tf):
    d_ff = gate.shape[2]
    n_full = d_ff // tf
    assert d_ff % tf != 0 and (d_ff - n_full * tf) % LANES == 0
    g, u, dn = gate.astype(BF16), up.astype(BF16), down.astype(BF16)
    return g, u, dn, g[:, :, n_full * tf:], u[:, :, n_full * tf:], dn[:, n_full * tf:, :]


def _inproj_kernel(x_ref, g_ref, w_ref, o_ref, h_ref):
    @pl.when(pl.program_id(1) == 0)
    def _():
        h_ref[...] = _rms_rows(x_ref[...], g_ref[...]).astype(BF16)

    o_ref[...] = jnp.dot(h_ref[...], w_ref[...], preferred_element_type=F32).astype(o_ref.dtype)


def _inproj(x, gain, w, layer, *, tm, tn):
    t, d = x.shape
    n = w.shape[2]
    return pl.pallas_call(
        _inproj_kernel,
        grid=(t // tm, n // tn),
        in_specs=[pl.BlockSpec((tm, d), lambda i, j: (i, 0)),
                  pl.BlockSpec((None, 1, d), lambda i, j: (layer, 0, 0)),
                  pl.BlockSpec((None, d, tn), lambda i, j: (layer, 0, j))],
        out_specs=pl.BlockSpec((tm, tn), lambda i, j: (i, j)),
        out_shape=jax.ShapeDtypeStruct((t, n), BF16),
        scratch_shapes=[pltpu.VMEM((tm, d), BF16)],
        compiler_params=_cparams(2),
        name="inproj",
    )(x, gain, w)


def _seg_sum(x, first):
    s0 = jnp.sum(jnp.where(first, x, 0.0), axis=1, keepdims=True)
    s1 = jnp.sum(jnp.where(first, 0.0, x), axis=1, keepdims=True)
    return jnp.where(first, s0, s1)


def _bd(x):
    w = x.shape[1]
    first = lax.broadcasted_iota(jnp.int32, (1, w), 1) < (w // 2)
    zero = jnp.zeros_like(x)
    return jnp.concatenate([jnp.where(first, x, zero), jnp.where(first, zero, x)], axis=0)


def _diag_blocks(m):
    n = m.shape[0] // 2
    first = lax.broadcasted_iota(jnp.int32, (1, 2 * n), 1) < n
    return jnp.where(first, m[0:n], m[n:2 * n])


def _rwkv_kernel(rkv_ref, l_ref, rkvp_ref, lp_ref, srkv_ref, sl_ref, vec_ref, mul_ref, wl1_ref, wl2_ref,
                 s0_ref, y_ref, sout_ref, z_ref, *, sb, tb, c, nb):
    b = pl.program_id(1)
    nc = tb // c
    hd = HEAD_DIM

    vec = vec_ref[...]
    mu_rkv = jnp.concatenate([vec[0:1], vec[1:2], vec[2:3]], axis=1)
    mu_l = mul_ref[...]

    lane = lax.broadcasted_iota(jnp.int32, (1, LANES), 1)
    first = lane < hd
    row_t = lax.broadcasted_iota(jnp.int32, (tb, 1), 0)
    ri = lax.broadcasted_iota(jnp.int32, (c, 2 * c), 0)
    ci = lax.broadcasted_iota(jnp.int32, (c, 2 * c), 1) % c
    lower_incl = ri >= ci
    lower_strict = ri > ci
    eye_p = jnp.where(ri == ci, 1.0, 0.0).astype(F32)
    rt_ = lax.broadcasted_iota(jnp.int32, (c, c), 0)
    ct_ = lax.broadcasted_iota(jnp.int32, (c, c), 1)
    tri = jnp.where(rt_ >= ct_, 1.0, 0.0).astype(BF16)
    rj = lax.broadcasted_iota(jnp.int32, (hd, LANES), 0)
    cj = lax.broadcasted_iota(jnp.int32, (hd, LANES), 1) % hd
    eye_h = rj == cj

    @pl.when(b == 0)
    def _():
        for s in range(sb):
            for p in range(PAIRS):
                z_ref[s, p] = jnp.concatenate([s0_ref[s, 2 * p].T, s0_ref[s, 2 * p + 1].T], axis=1)

    def shifted(ref, pref, sref, s, mu):
        raw = ref[s].astype(F32)
        before = jnp.where(b == 0, sref[s], pref[s][BF16_ROWS - 1:BF16_ROWS, :].astype(F32))
        prev = jnp.where(row_t == 0, before, pltpu.roll(raw, 1, 0))
        return raw + (prev - raw) * mu

    pre = {}
    for s in range(sb):
        x3 = shifted(rkv_ref, rkvp_ref, srkv_ref, s, mu_rkv)
        lx = shifted(l_ref, lp_ref, sl_ref, s, mu_l)
        l01 = lx[:, 0:LANES]
        z01 = jnp.where(first, jnp.tanh(l01), l01)
        wa = _dot(z01, wl1_ref[...])
        gate = _dot(jax.nn.sigmoid(lx[:, _LORA_G0:_LORA_G0 + _LORA_GW]), wl2_ref[...])
        for p in range(PAIRS):
            ps = slice(p * LANES, (p + 1) * LANES)
            w0, a0, k_k, k_a, r_k = vec[3:4, ps], vec[4:5, ps], vec[5:6, ps], vec[6:7, ps], vec[7:8, ps]
            rx = x3[:, _COL_R + p * LANES:_COL_R + (p + 1) * LANES]
            kx = x3[:, _COL_K + p * LANES:_COL_K + (p + 1) * LANES]
            vx = x3[:, _COL_V + p * LANES:_COL_V + (p + 1) * LANES]
            w_pre = wa[:, ps] + w0
            a_pre = wa[:, A_WIDTH + p * LANES:A_WIDTH + (p + 1) * LANES] + a0
            softplus = jnp.maximum(-w_pre, 0.0) + jnp.log1p(jnp.exp(-jnp.abs(w_pre)))
            logd = -jnp.exp(-softplus - 0.5)
            a = jax.nn.sigmoid(a_pre)
            kxk = kx * k_k
            kk = kxk / jnp.maximum(jnp.sqrt(_seg_sum(kxk * kxk, first)), 1e-12)
            kp = kx * (1.0 + (a - 1.0) * k_a)
            pre[(s, p)] = dict(r=rx, kk=kk, kp=kp, bb=kk * a, v=vx, logd=logd, gate=gate[:, ps],
                               bonus=_seg_sum(rx * kp * r_k, first) * vx)

    units = [(s, ch, p) for ch in range(nc) for s in range(sb) for p in range(PAIRS)]

    def rows(name, u):
        s, ch, p = units[u]
        return pre[(s, p)][name][ch * c:(ch + 1) * c]

    nu = len(units)
    cin = []
    for u in range(nu):
        hi, mid, lo = _split3(rows("logd", u))
        cs = jnp.dot(tri, jnp.concatenate([hi, mid, lo], axis=1), preferred_element_type=F32)
        cin.append(cs[:, 0:LANES] + cs[:, LANES:2 * LANES] + cs[:, 2 * LANES:3 * LANES])
    kk_t, r_t, k_d, b_d, p_end, g = [], [], [], [], [], []
    for u in range(nu):
        ld = rows("logd", u)
        c_end = cin[u][c - 1:c, :]
        e_inv = jnp.exp(-cin[u])
        e_dec = jnp.exp(c_end - cin[u])
        kk_t.append((rows("kk", u) * jnp.exp(cin[u] - ld)).astype(BF16))
        r_t.append(rows("r", u) * jnp.exp(cin[u]))
        k_d.append((rows("kp", u) * e_dec).astype(BF16))
        b_d.append((rows("bb", u) * e_dec).astype(BF16))
        p_end.append(jnp.exp(c_end))
        lhs = jnp.concatenate([kk_t[u], r_t[u].astype(BF16)], axis=0)
        rhs = jnp.concatenate([_bd((rows("bb", u) * e_inv).astype(BF16)),
                               _bd((rows("kp", u) * e_inv).astype(BF16))], axis=0)
        g.append(_dot_nt(lhs, rhs))
    m_ab = [jnp.where(lower_strict, g[u][0:c, 0:2 * c], 0.0).astype(BF16) for u in range(nu)]
    a_rb = [jnp.where(lower_incl, g[u][c:2 * c, 0:2 * c], 0.0).astype(BF16) for u in range(nu)]
    m_ak = [jnp.where(lower_strict, g[u][0:c, 2 * c:4 * c], 0.0).astype(BF16) for u in range(nu)]
    a_rk = [jnp.where(lower_incl, g[u][c:2 * c, 2 * c:4 * c], 0.0).astype(BF16) for u in range(nu)]
    v_bd = [_bd(rows("v", u).astype(BF16)) for u in range(nu)]
    x = [eye_p - m_ab[u].astype(F32) for u in range(nu)]
    pw = [_dot(m_ab[u], _bd(m_ab[u])) for u in range(nu)]
    k = 2
    while 2 * k < c:
        both = [_dot(jnp.concatenate([x[u].astype(BF16), pw[u].astype(BF16)], axis=0), _bd(pw[u].astype(BF16)))
                for u in range(nu)]
        x = [x[u] + both[u][0:c] for u in range(nu)]
        pw = [both[u][c:2 * c] for u in range(nu)]
        k *= 2
    t_inv = [(x[u] + _dot(x[u], _bd(pw[u].astype(BF16)))).astype(BF16) for u in range(nu)]
    mv = [_dot(m_ak[u], v_bd[u]) for u in range(nu)]
    tw = [_dot(t_inv[u], jnp.concatenate([_bd(kk_t[u]), _bd(mv[u].astype(BF16))], axis=1)).astype(BF16)
          for u in range(nu)]
    bt = [_dot_tn(b_d[u], tw[u]) for u in range(nu)]
    kv = [_dot_tn(k_d[u], rows("v", u)) for u in range(nu)]
    ab = [_dot(a_rb[u], jnp.concatenate([_bd(tw[u][:, 0:LANES]), _bd(tw[u][:, LANES:2 * LANES])], axis=1))
          for u in range(nu)]
    av = [_dot(a_rk[u], v_bd[u]) for u in range(nu)]
    lhs_z, u_c, y_i = [], [], []
    for u in range(nu):
        a_c = jnp.where(eye_h, p_end[u], 0.0) - _diag_blocks(bt[u][:, 0:LANES])
        q_e = r_t[u] - ab[u][:, 0:LANES]
        lhs_z.append(jnp.concatenate([a_c, q_e], axis=0).astype(BF16))
        u_c.append(_diag_blocks(kv[u]) - _diag_blocks(bt[u][:, LANES:2 * LANES]))
        y_i.append(av[u] - ab[u][:, LANES:2 * LANES])
    z = {(s, p): z_ref[s, p] for s in range(sb) for p in range(PAIRS)}
    per_level = sb * PAIRS
    for ch in range(nc):
        level = range(ch * per_level, (ch + 1) * per_level)
        both = {u: _dot(lhs_z[u], _bd(z[units[u][0], units[u][2]].astype(BF16))) for u in level}
        ys = {u: both[u][hd:hd + c] + y_i[u] for u in level}
        for u in level:
            z[units[u][0], units[u][2]] = both[u][0:hd] + u_c[u]
        means = {u: _seg_sum(ys[u], first) * (1.0 / hd) for u in level}
        ycs = {u: ys[u] - means[u] for u in level}
        variances = {u: _seg_sum(ycs[u] * ycs[u], first) * (1.0 / hd) for u in level}
        for u in level:
            s, _, p = units[u]
            ps = slice(p * LANES, (p + 1) * LANES)
            yn = ycs[u] * lax.rsqrt(variances[u] + GN_EPS)
            out = (yn * vec[8:9, ps] + vec[9:10, ps] + rows("bonus", u)) * rows("gate", u)
            y_ref[s, ch * c:(ch + 1) * c, ps] = out.astype(y_ref.dtype)
    for s in range(sb):
        for p in range(PAIRS):
            z_ref[s, p] = z[(s, p)]

    @pl.when(b == nb - 1)
    def _():
        for s in range(sb):
            for p in range(PAIRS):
                sout_ref[s, 2 * p] = z[(s, p)][:, 0:hd].T
                sout_ref[s, 2 * p + 1] = z[(s, p)][:, hd:2 * hd].T


def _rwkv(p3, shiftp, s0, vec, mu_l, wl1, wl2, layer, *, sb, tb, c):
    s, t, _ = p3.shape
    lb = _COL_LORA // _LORA_W
    tpb = tb // BF16_ROWS
    nb = t // tb

    def prev_tile(b):
        return jnp.maximum(b * tpb - 1, 0)

    in_specs = [
        pl.BlockSpec((sb, tb, _RKV_W), lambda i, b: (i, b, 0)),
        pl.BlockSpec((sb, tb, _LORA_W), lambda i, b: (i, b, lb)),
        pl.BlockSpec((sb, BF16_ROWS, _RKV_W), lambda i, b: (i, prev_tile(b), 0)),
        pl.BlockSpec((sb, BF16_ROWS, _LORA_W), lambda i, b: (i, prev_tile(b), lb)),
        pl.BlockSpec((sb, 1, _RKV_W), lambda i, b: (i, 0, 0)),
        pl.BlockSpec((sb, 1, _LORA_W), lambda i, b: (i, 0, _RKV_W // _LORA_W)),
        pl.BlockSpec((None, _VEC_ROWS, A_WIDTH), lambda i, b: (layer, 0, 0)),
        pl.BlockSpec((None, 1, _LORA_W), lambda i, b: (layer, 0, 0)),
        pl.BlockSpec((None, LANES, 2 * A_WIDTH), lambda i, b: (layer, 0, 0)),
        pl.BlockSpec((None, _LORA_GW, A_WIDTH), lambda i, b: (layer, 0, 0)),
        pl.BlockSpec((sb, A_HEADS, HEAD_DIM, HEAD_DIM), lambda i, b: (i, 0, 0, 0)),
    ]
    out_specs = [
        pl.BlockSpec((sb, tb, A_WIDTH), lambda i, b: (i, b, 0)),
        pl.BlockSpec((sb, A_HEADS, HEAD_DIM, HEAD_DIM), lambda i, b: (i, 0, 0, 0)),
    ]
    return pl.pallas_call(
        functools.partial(_rwkv_kernel, sb=sb, tb=tb, c=c, nb=nb),
        grid=(s // sb, nb),
        in_specs=in_specs,
        out_specs=out_specs,
        out_shape=[jax.ShapeDtypeStruct((s, t, A_WIDTH), BF16),
                   jax.ShapeDtypeStruct((s, A_HEADS, HEAD_DIM, HEAD_DIM), F32)],
        scratch_shapes=[pltpu.VMEM((sb, PAIRS, HEAD_DIM, LANES), F32)],
        compiler_params=_cparams(2),
        name="rwkv",
    )(p3, p3, p3, p3, shiftp, shiftp, vec, mu_l, wl1, wl2, s0)


def _norm_rope(slabs):
    lane = lax.broadcasted_iota(jnp.int32, (1, LANES), 1)
    first = lane < HEAD_DIM
    low_half = (lane % HEAD_DIM) < (HEAD_DIM // 2)
    tiles = [(n, x[:, j * LANES:(j + 1) * LANES]) for n, (x, _, _, _) in enumerate(slabs)
             for j in range(x.shape[1] // LANES)]
    sq = [xs * xs for _, xs in tiles]
    s0 = [jnp.sum(jnp.where(first, s, 0.0), axis=1, keepdims=True) for s in sq]
    s1 = [jnp.sum(jnp.where(first, 0.0, s), axis=1, keepdims=True) for s in sq]
    xn = [xs * lax.rsqrt(jnp.where(first, a, b) * (1.0 / HEAD_DIM) + RMS_EPS) * slabs[n][1]
          for (n, xs), a, b in zip(tiles, s0, s1)]
    up = [pltpu.roll(v, LANES - HEAD_DIM // 2, 1) for v in xn]
    down = [pltpu.roll(v, HEAD_DIM // 2, 1) for v in xn]
    out = [v * slabs[n][2] + jnp.where(low_half, u, w) * slabs[n][3]
           for (n, _), v, u, w in zip(tiles, xn, up, down)]
    res, o = [], 0
    for x, _, _, _ in slabs:
        k = x.shape[1] // LANES
        res.append(jnp.concatenate(out[o:o + k], axis=1) if k > 1 else out[o])
        o += k
    return res


def _sink_softmax_all(scores, sinks):
    ms = [jnp.maximum(jnp.max(s, axis=-1, keepdims=True), k) for s, k in zip(scores, sinks)]
    ps = [jnp.exp(s - m) for s, m in zip(scores, ms)]
    ds = [jnp.sum(p, axis=-1, keepdims=True) + jnp.exp(k - m) for p, k, m in zip(ps, sinks, ms)]
    return [(p / d).astype(BF16) for p, d in zip(ps, ds)]


def _sink_column(sink_ref, heads, rows):
    return jnp.concatenate([jnp.full((rows, 1), sink_ref[h], F32) for h in heads], axis=0)


_QK_SCALE = HEAD_DIM ** -0.5


def _band_kernel(sink_ref, q_ref, kv_ref, kvp_ref, cq_ref, sq_ref, cp_ref, sp_ref, gq_ref, gk_ref,
                 y_ref, ko_ref, *, tq):
    i = pl.program_id(0)
    hd = HEAD_DIM
    band = WINDOW + CHUNK
    kv = kv_ref[...]
    kvp = kvp_ref[...]
    q, k_cur, k_prev = _norm_rope([
        (q_ref[...].astype(F32), gq_ref[...], cq_ref[...], sq_ref[...]),
        (kv[:, 0:B_KV].astype(F32), gk_ref[...], cq_ref[...], sq_ref[...]),
        (kvp[:, 0:B_KV].astype(F32), gk_ref[...], cp_ref[...], sp_ref[...])])
    q = (q * _QK_SCALE).astype(BF16)
    ko_ref[...] = k_cur
    k_all = jnp.concatenate([k_prev.astype(BF16), k_cur.astype(BF16)], axis=0)
    v_all = jnp.concatenate([kvp[:, B_KV:2 * B_KV], kv[:, B_KV:2 * B_KV]], axis=0)
    k_g = [k_all[:, g * hd:(g + 1) * hd] for g in range(B_KV_HEADS)]
    v_g = [v_all[:, g * hd:(g + 1) * hd] for g in range(B_KV_HEADS)]
    col = lax.broadcasted_iota(jnp.int32, (1, band), 1)
    sink_g = [_sink_column(sink_ref, range(g * B_GROUP, (g + 1) * B_GROUP), CHUNK) for g in range(B_KV_HEADS)]

    jobs = [(k0, g) for k0 in range(0, tq, CHUNK) for g in range(B_KV_HEADS)]
    q_rows = [jnp.concatenate([q[k0:k0 + CHUNK, (g * B_GROUP + j) * hd:(g * B_GROUP + j + 1) * hd]
                               for j in range(B_GROUP)], axis=0) for k0, g in jobs]
    scores = [_dot_nt(qr, k_g[g][k0:k0 + band]) for (k0, g), qr in zip(jobs, q_rows)]
    scores = [jnp.where((col >= WINDOW - k0) | (i > 0), s, -1e30) for (k0, g), s in zip(jobs, scores)]
    probs = _sink_softmax_all(scores, [sink_g[g] for k0, g in jobs])
    outs = {job: _dot(p, v_g[job[1]][job[0]:job[0] + band]) for job, p in zip(jobs, probs)}
    for k0 in range(0, tq, CHUNK):
        heads = [outs[(k0, g)][j * CHUNK:(j + 1) * CHUNK] for g in range(B_KV_HEADS) for j in range(B_GROUP)]
        y_ref[k0:k0 + CHUNK, :] = jnp.concatenate(heads, axis=1).astype(y_ref.dtype)


def _attn_band(p2, cos, sin_signed, gq, gk, sinks, layer, *, tq):
    t = p2.shape[0]
    qb, kvb = _COL_Q // B_Q, _COL_KV // _KV_W
    wpb = tq // WINDOW

    def prev_idx(i):
        return jnp.maximum(i * wpb - 1, 0)

    return pl.pallas_call(
        functools.partial(_band_kernel, tq=tq),
        grid=(t // tq,),
        in_specs=[pl.BlockSpec(memory_space=pltpu.SMEM),
                  pl.BlockSpec((tq, B_Q), lambda i: (i, qb)),
                  pl.BlockSpec((tq, _KV_W), lambda i: (i, kvb)),
                  pl.BlockSpec((WINDOW, _KV_W), lambda i: (prev_idx(i), kvb)),
                  pl.BlockSpec((tq, LANES), lambda i: (i, 0)),
                  pl.BlockSpec((tq, LANES), lambda i: (i, 0)),
                  pl.BlockSpec((WINDOW, LANES), lambda i: (prev_idx(i), 0)),
                  pl.BlockSpec((WINDOW, LANES), lambda i: (prev_idx(i), 0)),
                  pl.BlockSpec((None, 1, LANES), lambda i: (layer, 0, 0)),
                  pl.BlockSpec((None, 1, LANES), lambda i: (layer, 0, 0))],
        out_specs=[pl.BlockSpec((tq, B_Q), lambda i: (i, 0)),
                   pl.BlockSpec((tq, B_KV), lambda i: (i, 0))],
        out_shape=[jax.ShapeDtypeStruct((t, B_Q), BF16),
                   jax.ShapeDtypeStruct((t, B_KV), F32)],
        compiler_params=_cparams(1),
        name="attn_band",
    )(sinks[layer], p2, p2, p2, cos, sin_signed, cos, sin_signed, gq, gk)


def _cached_kernel(sink_ref, q_ref, kv_ref, ck_ref, cv_ref, c_ref, s_ref, gq_ref, gk_ref,
                   y_ref, ko_ref, *, sb):
    hd = HEAD_DIM
    tn = q_ref.shape[1]
    jobs = [(s, g) for s in range(sb) for g in range(B_KV_HEADS)]
    q_rows, k_all, v_all = {}, {}, {}
    cos = jnp.concatenate([c_ref[...]] * sb, axis=0)
    sin = jnp.concatenate([s_ref[...]] * sb, axis=0)
    q_cat = jnp.concatenate([q_ref[s] for s in range(sb)], axis=0).astype(F32)
    kv_cat = jnp.concatenate([kv_ref[s] for s in range(sb)], axis=0)
    q_cat, k_cat_new = _norm_rope([(q_cat, gq_ref[...], cos, sin),
                                   (kv_cat[:, 0:B_KV].astype(F32), gk_ref[...], cos, sin)])
    q_cat = (q_cat * _QK_SCALE).astype(BF16)
    for s in range(sb):
        q = q_cat[s * tn:(s + 1) * tn]
        k_new = k_cat_new[s * tn:(s + 1) * tn]
        ko_ref[s] = k_new
        k_cat = jnp.concatenate([ck_ref[s].astype(BF16), k_new.astype(BF16)], axis=0)
        v_cat = jnp.concatenate([cv_ref[s].astype(BF16), kv_cat[s * tn:(s + 1) * tn, B_KV:2 * B_KV]], axis=0)
        for g in range(B_KV_HEADS):
            q_rows[(s, g)] = jnp.concatenate(
                [q[:, (g * B_GROUP + j) * hd:(g * B_GROUP + j + 1) * hd] for j in range(B_GROUP)], axis=0)
            k_all[(s, g)] = k_cat[:, g * hd:(g + 1) * hd]
            v_all[(s, g)] = v_cat[:, g * hd:(g + 1) * hd]
    scores = [_dot_nt(q_rows[j], k_all[j]) for j in jobs]
    sink_g = [_sink_column(sink_ref, range(g * B_GROUP, (g + 1) * B_GROUP), tn) for g in range(B_KV_HEADS)]
    probs = _sink_softmax_all(scores, [sink_g[g] for s, g in jobs])
    outs = {j: _dot(p, v_all[j]) for j, p in zip(jobs, probs)}
    for s in range(sb):
        heads = [outs[(s, g)][j * tn:(j + 1) * tn] for g in range(B_KV_HEADS) for j in range(B_GROUP)]
        y_ref[s] = jnp.concatenate(heads, axis=1).astype(y_ref.dtype)


def _attn_cached(p3, ck, cv, cos, sin_signed, gq, gk, sinks, layer, *, sb):
    s, t, _ = p3.shape
    w = ck.shape[2]
    qb, kvb = _COL_Q // B_Q, _COL_KV // _KV_W
    return pl.pallas_call(
        functools.partial(_cached_kernel, sb=sb),
        grid=(s // sb,),
        in_specs=[pl.BlockSpec(memory_space=pltpu.SMEM),
                  pl.BlockSpec((sb, t, B_Q), lambda i: (i, 0, qb)),
                  pl.BlockSpec((sb, t, _KV_W), lambda i: (i, 0, kvb)),
                  pl.BlockSpec((None, sb, w, B_KV), lambda i: (layer, i, 0, 0)),
                  pl.BlockSpec((None, sb, w, B_KV), lambda i: (layer, i, 0, 0)),
                  pl.BlockSpec((t, LANES), lambda i: (0, 0)),
                  pl.BlockSpec((t, LANES), lambda i: (0, 0)),
                  pl.BlockSpec((None, 1, LANES), lambda i: (layer, 0, 0)),
                  pl.BlockSpec((None, 1, LANES), lambda i: (layer, 0, 0))],
        out_specs=[pl.BlockSpec((sb, t, B_Q), lambda i: (i, 0, 0)),
                   pl.BlockSpec((sb, t, B_KV), lambda i: (i, 0, 0))],
        out_shape=[jax.ShapeDtypeStruct((s, t, B_Q), BF16),
                   jax.ShapeDtypeStruct((s, t, B_KV), F32)],
        compiler_params=_cparams(1),
        name="attn_cached",
    )(sinks[layer], p3, p3, ck, cv, cos, sin_signed, gq, gk)


def _outproj_kernel(x_ref, ya_ref, yb_ref, ga_ref, gb_ref, wa_ref, wb_ref, wo_ref, o_ref):
    ua = jnp.dot(ya_ref[...], wa_ref[...], preferred_element_type=F32)
    ub = jnp.dot(yb_ref[...], wb_ref[...], preferred_element_type=F32)
    merged = (jax.nn.sigmoid(ga_ref[...].astype(F32)) * ua + jax.nn.sigmoid(gb_ref[...].astype(F32)) * ub)
    o_ref[...] = x_ref[...] + jnp.dot(merged.astype(BF16), wo_ref[...], preferred_element_type=F32)


def _outproj(x, ya, yb, p2, wa, wb, wo, layer, *, tm):
    t, d = x.shape
    gab, gbb = _COL_GA // d, _COL_GB // d
    return pl.pallas_call(
        _outproj_kernel,
        grid=(t // tm,),
        in_specs=[pl.BlockSpec((tm, d), lambda i: (i, 0)),
                  pl.BlockSpec((tm, A_WIDTH), lambda i: (i, 0)),
                  pl.BlockSpec((tm, B_Q), lambda i: (i, 0)),
                  pl.BlockSpec((tm, d), lambda i: (i, gab)),
                  pl.BlockSpec((tm, d), lambda i: (i, gbb)),
                  _single_buffered((None, A_WIDTH, d), lambda i: (layer, 0, 0)),
                  _single_buffered((None, B_Q, d), lambda i: (layer, 0, 0)),
                  _single_buffered((None, d, d), lambda i: (layer, 0, 0))],
        out_specs=pl.BlockSpec((tm, d), lambda i: (i, 0)),
        out_shape=jax.ShapeDtypeStruct((t, d), F32),
        compiler_params=_cparams(1),
        name="outproj",
    )(x, ya, yb, p2, p2, wa, wb, wo)


def _a_pieces(a):
    o = 0
    out = []
    for w in (A_WIDTH, DECAY_LORA, A_WIDTH, A_WIDTH, ICLR_LORA, GATE_LORA):
        out.append(a[..., o:o + w])
        o += w
    return out


def _pad_last(w, n):
    return jnp.pad(w, [(0, 0)] * (w.ndim - 1) + [(0, n - w.shape[-1])])


def _regroup_a(a):
    r, w_lo, k, v, a_lo, g_lo = _a_pieces(a)
    lora = _pad_last(jnp.concatenate([w_lo, a_lo, g_lo], axis=-1), _LORA_W)
    return jnp.concatenate([r, k, v], axis=-1), lora


def _regroup_w_in(w):
    o = A_COLS
    wq = w[..., o:o + B_Q]
    wk = w[..., o + B_Q:o + B_Q + B_KV]
    wv = w[..., o + B_Q + B_KV:o + B_Q + 2 * B_KV]
    o += B_Q + 2 * B_KV
    wga = w[..., o:o + D_MODEL]
    wgb = w[..., o + D_MODEL:o + 2 * D_MODEL]
    rkv, lora = _regroup_a(w[..., :A_COLS])
    return jnp.concatenate([rkv, wq, wga, wgb, lora, wk, wv], axis=-1).astype(BF16)


def _shift_row_to_a(p_last):
    p_last = p_last.astype(F32)
    r = p_last[..., _COL_R:_COL_R + A_WIDTH]
    k = p_last[..., _COL_K:_COL_K + A_WIDTH]
    v = p_last[..., _COL_V:_COL_V + A_WIDTH]
    o = _COL_LORA
    w_lo = p_last[..., o:o + DECAY_LORA]
    a_lo = p_last[..., o + DECAY_LORA:o + DECAY_LORA + ICLR_LORA]
    g_lo = p_last[..., o + DECAY_LORA + ICLR_LORA:o + DECAY_LORA + ICLR_LORA + GATE_LORA]
    return jnp.concatenate([r, w_lo, k, v, a_lo, g_lo], axis=-1)


def _lora_weights(decay_w2, iclr_a2, gate_g2):
    top = jnp.concatenate([decay_w2, jnp.zeros_like(decay_w2)], axis=-1)
    bot = jnp.concatenate([jnp.zeros_like(iclr_a2), iclr_a2], axis=-1)
    wl1 = jnp.concatenate([top, bot], axis=-2)
    wl2 = jnp.pad(gate_g2, ((0, 0), (0, _LORA_GW - GATE_LORA), (0, 0)))
    return wl1.astype(BF16), wl2.astype(BF16)


def _rope_tables(pos):
    half = HEAD_DIM // 2
    inv = ROPE_THETA ** (-jnp.arange(half, dtype=F32) / half)
    ang = pos.astype(F32)[:, None] * inv[None, :]
    cos = jnp.cos(ang)
    sin = jnp.sin(ang)
    cos_t = jnp.tile(cos, (1, LANES // half))
    sin_t = jnp.tile(jnp.concatenate([-sin, sin], axis=1), (1, LANES // HEAD_DIM))
    return cos_t, sin_t


_FF_TILE = 1024


def kernel(x_prompt, x_sample, cache_k, cache_v, state_wkv, state_shift, norm_ff1, ff1_gate, ff1_up, ff1_down,
           norm_mix, w_in, shift_mu, decay_w0, decay_w2, iclr_a0, iclr_a2, gate_g2, k_k, k_a, r_k, gn_gain,
           gn_bias, q_norm, k_norm, sinks, w_up_a, w_up_b, w_o, norm_ff2, ff2_gate, ff2_up, ff2_down):
    depth = norm_ff1.shape[0]
    bp, tp, d = x_prompt.shape
    bs, ts, _ = x_sample.shape
    assert bp == 1 and d == D_MODEL

    ff1 = _ffn_weights(ff1_gate, ff1_up, ff1_down, _FF_TILE)
    ff2 = _ffn_weights(ff2_gate, ff2_up, ff2_down, _FF_TILE)
    g_ff1 = norm_ff1[:, None, :]
    g_ff2 = norm_ff2[:, None, :]
    g_mix = norm_mix[:, None, :]
    w_in_b = _regroup_w_in(w_in)
    mu_rkv, mu_l = _regroup_a(shift_mu)
    mu_l = mu_l[:, None, :]
    rows = [mu_rkv[:, 0:A_WIDTH], mu_rkv[:, A_WIDTH:2 * A_WIDTH], mu_rkv[:, 2 * A_WIDTH:], decay_w0, iclr_a0,
            k_k, k_a, r_k.reshape(depth, A_WIDTH), gn_gain, gn_bias]
    vec = jnp.stack(rows + [jnp.zeros((depth, A_WIDTH), F32)] * (_VEC_ROWS - len(rows)), axis=1)
    wl1, wl2 = _lora_weights(decay_w2, iclr_a2, gate_g2)
    gq = jnp.tile(q_norm, (1, LANES // HEAD_DIM))[:, None, :]
    gk = jnp.tile(k_norm, (1, LANES // HEAD_DIM))[:, None, :]
    wa = w_up_a.astype(BF16)
    wb = w_up_b.astype(BF16)
    wo = w_o.astype(BF16)
    s_rkv, s_lora = _regroup_a(state_shift)
    shift_s = jnp.concatenate([s_rkv, s_lora], axis=-1)[:, :, None, :]
    zero_shift = jnp.zeros((bp, 1, _RKV_W + _LORA_W), F32)
    zero_state = jnp.zeros((bp, A_HEADS, HEAD_DIM, HEAD_DIM), F32)
    ck = cache_k.reshape(depth, bs, -1, B_KV)
    cv = cache_v.reshape(depth, bs, -1, B_KV)
    cos_p, sin_p = _rope_tables(jnp.arange(tp))
    cos_s, sin_s = _rope_tables(PAST_LEN + jnp.arange(ts))

    xp = x_prompt.reshape(tp, d)
    xs = x_sample.reshape(bs * ts, d)
    outs = {k: [] for k in ("p_wkv", "p_shift", "p_k", "p_v", "s_wkv", "s_shift", "s_k", "s_v")}
    for l in range(depth):
        xp = _ffn(xp, g_ff1, ff1, l, tm=512, tf=_FF_TILE)
        xs = _ffn(xs, g_ff1, ff1, l, tm=512, tf=_FF_TILE)

        pp = _inproj(xp, g_mix, w_in_b, l, tm=1024, tn=1024)
        ps = _inproj(xs, g_mix, w_in_b, l, tm=512, tn=1024)
        pp3 = pp.reshape(bp, tp, NP_COLS)
        ps3 = ps.reshape(bs, ts, NP_COLS)

        ya_p, wkv_p = _rwkv(pp3, zero_shift, zero_state, vec, mu_l, wl1, wl2, l, sb=1, tb=4 * CHUNK, c=CHUNK)
        ya_s, wkv_s = _rwkv(ps3, shift_s[l], state_wkv[l], vec, mu_l, wl1, wl2, l, sb=2, tb=ts, c=ts)

        yb_p, kr_p = _attn_band(pp, cos_p, sin_p, gq, gk, sinks, l, tq=2 * WINDOW)
        yb_s, kr_s = _attn_cached(ps3, ck, cv, cos_s, sin_s, gq, gk, sinks, l, sb=8)

        xp = _outproj(xp, ya_p.reshape(tp, A_WIDTH), yb_p, pp, wa, wb, wo, l, tm=256)
        xs = _outproj(xs, ya_s.reshape(bs * ts, A_WIDTH), yb_s.reshape(bs * ts, B_Q), ps, wa, wb, wo, l, tm=256)

        xp = _ffn(xp, g_ff2, ff2, l, tm=512, tf=_FF_TILE)
        xs = _ffn(xs, g_ff2, ff2, l, tm=512, tf=_FF_TILE)

        vcol = _COL_KV + B_KV
        outs["p_wkv"].append(wkv_p)
        outs["p_shift"].append(_shift_row_to_a(pp3[:, -1, :]))
        outs["p_k"].append(kr_p[-WINDOW:].reshape(bp, WINDOW, B_KV_HEADS, HEAD_DIM))
        outs["p_v"].append(pp3[:, -WINDOW:, vcol:vcol + B_KV].astype(F32).reshape(bp, WINDOW, B_KV_HEADS, HEAD_DIM))
        outs["s_wkv"].append(wkv_s)
        outs["s_shift"].append(_shift_row_to_a(ps3[:, -1, :]))
        outs["s_k"].append(kr_s.reshape(bs, ts, B_KV_HEADS, HEAD_DIM))
        outs["s_v"].append(ps3[:, :, vcol:vcol + B_KV].astype(F32).reshape(bs, ts, B_KV_HEADS, HEAD_DIM))

    return (xp.reshape(bp, tp, d), xs.reshape(bs, ts, d),
            jnp.stack(outs["p_wkv"]), jnp.stack(outs["p_shift"]), jnp.stack(outs["p_k"]), jnp.stack(outs["p_v"]),
            jnp.stack(outs["s_wkv"]), jnp.stack(outs["s_shift"]), jnp.stack(outs["s_k"]), jnp.stack(outs["s_v"]))
```

```python
import functools

import jax
import jax.numpy as jnp
from jax import lax
from jax.experimental import pallas as pl
from jax.experimental.pallas import tpu as pltpu

F32 = jnp.float32
BF16 = jnp.bfloat16

HEAD_DIM = 64
A_WIDTH = 1024
A_HEADS = A_WIDTH // HEAD_DIM
DECAY_LORA = 64
ICLR_LORA = 64
GATE_LORA = 160
GN_EPS = 64e-5
RMS_EPS = 1e-6
B_HEADS = 16
B_KV_HEADS = 4
B_GROUP = B_HEADS // B_KV_HEADS
B_Q = B_HEADS * HEAD_DIM
B_KV = B_KV_HEADS * HEAD_DIM
CHUNK = 64
WINDOW = 128
ROPE_THETA = 10000.0
PAST_LEN = 1024
D_MODEL = 2048
A_COLS = 3 * A_WIDTH + DECAY_LORA + ICLR_LORA + GATE_LORA

LANES = 128
BF16_ROWS = 16
VMEM_LIMIT_BYTES = 56 * 1024 * 1024

_COL_R = 0
_COL_K = _COL_R + A_WIDTH
_COL_V = _COL_K + A_WIDTH
_RKV_W = 3 * A_WIDTH
_COL_Q = _COL_V + A_WIDTH
_COL_GA = _COL_Q + B_Q
_COL_GB = _COL_GA + D_MODEL
_COL_LORA = _COL_GB + D_MODEL
_LORA_W = 512
_COL_KV = _COL_LORA + _LORA_W
_KV_W = 2 * B_KV
NP_COLS = _COL_KV + _KV_W
_LORA_G0 = LANES
_LORA_GW = 2 * LANES
PAIRS = A_HEADS // 2
_VEC_ROWS = 16


def _cparams(n_axes):
    return pltpu.CompilerParams(dimension_semantics=("arbitrary",) * n_axes,
                                vmem_limit_bytes=VMEM_LIMIT_BYTES)


def _dot(a, b):
    return jnp.dot(a.astype(BF16), b.astype(BF16), preferred_element_type=F32)


def _dot_nt(a, b):
    return lax.dot_general(a.astype(BF16), b.astype(BF16), (((1,), (1,)), ((), ())),
                           preferred_element_type=F32)


def _dot_tn(a, b):
    return lax.dot_general(a.astype(BF16), b.astype(BF16), (((0,), (0,)), ((), ())),
                           preferred_element_type=F32)


def _split3(x):
    hi = x.astype(BF16)
    r1 = x - hi.astype(F32)
    mid = r1.astype(BF16)
    lo = (r1 - mid.astype(F32)).astype(BF16)
    return hi, mid, lo


def _rms_rows(x, gain):
    ms = jnp.mean(x * x, axis=-1, keepdims=True)
    return x * lax.rsqrt(ms + RMS_EPS) * gain


def _single_buffered(shape, index_map):
    return pl.BlockSpec(shape, index_map, pipeline_mode=pl.Buffered(1))


def _ffn_init(x_ref, g_ref, o_ref, h_ref):
    x = x_ref[...]
    h_ref[...] = _rms_rows(x, g_ref[...]).astype(BF16)
    o_ref[...] = x


def _swiglu_accumulate(h_ref, o_ref, wg, wu, wd):
    h = h_ref[...]
    gate = jnp.dot(h, wg, preferred_element_type=F32)
    up = jnp.dot(h, wu, preferred_element_type=F32)
    act = (0.5 * gate * jax.nn.sigmoid(gate) * up).astype(BF16)
    o_ref[...] += jnp.dot(act, wd, preferred_element_type=F32)


def _ffn_kernel(x_ref, g_ref, wg_ref, wu_ref, wd_ref, wgt_ref, wut_ref, wdt_ref, o_ref, h_ref, *, n_full):
    f = pl.program_id(1)

    @pl.when(f == 0)
    def _():
        _ffn_init(x_ref, g_ref, o_ref, h_ref)

    @pl.when(f < n_full)
    def _():
        _swiglu_accumulate(h_ref, o_ref, wg_ref[...], wu_ref[...], wd_ref[...])

    @pl.when(f == n_full)
    def _():
        _swiglu_accumulate(h_ref, o_ref, wgt_ref[...], wut_ref[...], wdt_ref[...])


def _ffn(x, gain, w, layer, *, tm, tf):
    wg, wu, wd, wgt, wut, wdt = w
    t, d = x.shape
    n_full = wg.shape[1] // tf
    tail = wgt.shape[1]
    last = n_full - 1
    return pl.pallas_call(
        functools.partial(_ffn_kernel, n_full=n_full),
        grid=(t // tm, n_full + 1),
        in_specs=[pl.BlockSpec((tm, d), lambda i, f: (i, 0)),
                  pl.BlockSpec((None, 1, d), lambda i, f: (layer, 0, 0)),
                  pl.BlockSpec((d, tf), lambda i, f: (0, jnp.minimum(f, last))),
                  pl.BlockSpec((d, tf), lambda i, f: (0, jnp.minimum(f, last))),
                  pl.BlockSpec((tf, d), lambda i, f: (jnp.minimum(f, last), 0)),
                  _single_buffered((d, tail), lambda i, f: (0, 0)),
                  _single_buffered((d, tail), lambda i, f: (0, 0)),
                  _single_buffered((tail, d), lambda i, f: (0, 0))],
        out_specs=pl.BlockSpec((tm, d), lambda i, f: (i, 0)),
        out_shape=jax.ShapeDtypeStruct((t, d), F32),
        scratch_shapes=[pltpu.VMEM((tm, d), BF16)],
        compiler_params=_cparams(2),
        name="ffn",
    )(x, gain, wg, wu, wd, wgt, wut, wdt)


def _ffn_cast_kernel(x_ref, g_ref, wg_ref, wu_ref, wd_ref, wgt_ref, wut_ref, wdt_ref,
                     o_ref, wg16_ref, wu16_ref, wd16_ref, h_ref, *, n_full):
    f = pl.program_id(0)

    @pl.when(f == 0)
    def _():
        _ffn_init(x_ref, g_ref, o_ref, h_ref)

    @pl.when(f < n_full)
    def _():
        wg = wg_ref[...].astype(BF16)
        wu = wu_ref[...].astype(BF16)
        wd = wd_ref[...].astype(BF16)
        wg16_ref[...] = wg
        wu16_ref[...] = wu
        wd16_ref[...] = wd
        _swiglu_accumulate(h_ref, o_ref, wg, wu, wd)

    @pl.when(f == n_full)
    def _():
        _swiglu_accumulate(h_ref, o_ref, wgt_ref[...], wut_ref[...], wdt_ref[...])


def _ffn_cast(x, gain, gate, up, down, tails, layer, *, tf):
    wgt, wut, wdt = tails
    t, d = x.shape
    d_ff = gate.shape[2]
    tail = wgt.shape[2]
    n_full = (d_ff - tail) // tf
    last = n_full - 1
    wide = pl.BlockSpec((None, d, tf), lambda f: (layer, 0, jnp.minimum(f, last)))
    tall = pl.BlockSpec((None, tf, d), lambda f: (layer, jnp.minimum(f, last), 0))
    return pl.pallas_call(
        functools.partial(_ffn_cast_kernel, n_full=n_full),
        grid=(n_full + 1,),
        in_specs=[_single_buffered((t, d), lambda f: (0, 0)),
                  pl.BlockSpec((None, 1, d), lambda f: (layer, 0, 0)),
                  wide, wide, tall,
                  _single_buffered((None, d, tail), lambda f: (layer, 0, 0)),
                  _single_buffered((None, d, tail), lambda f: (layer, 0, 0)),
                  _single_buffered((None, tail, d), lambda f: (layer, 0, 0))],
        out_specs=[pl.BlockSpec((t, d), lambda f: (0, 0)),
                   pl.BlockSpec((d, tf), lambda f: (0, jnp.minimum(f, last))),
                   pl.BlockSpec((d, tf), lambda f: (0, jnp.minimum(f, last))),
                   pl.BlockSpec((tf, d), lambda f: (jnp.minimum(f, last), 0))],
        out_shape=[jax.ShapeDtypeStruct((t, d), F32),
                   jax.ShapeDtypeStruct((d, n_full * tf), BF16),
                   jax.ShapeDtypeStruct((d, n_full * tf), BF16),
                   jax.ShapeDtypeStruct((n_full * tf, d), BF16)],
        scratch_shapes=[pltpu.VMEM((t, d), BF16)],
        compiler_params=_cparams(1),
        name="ffn_cast",
    )(x, gain, gate, up, down, wgt, wut, wdt)


def _ffn_tails(gate, up, down, cols):
    d_ff = gate.shape[2]
    return (gate[:, :, d_ff - cols:].astype(BF16), up[:, :, d_ff - cols:].astype(BF16),
            down[:, d_ff - cols:, :].astype(BF16))


def _ffn_pair(xp, xs, gain, gate, up, down, tails, layer, *, tm, tf, tf_cast):
    xs, wg, wu, wd = _ffn_cast(xs, gain, gate, up, down, tails, layer, tf=tf_cast)
    wgt, wut, wdt = (t[layer] for t in tails)
    done = (wg.shape[1] // tf) * tf
    w = (wg, wu, wd,
         jnp.concatenate([wg[:, done:], wgt], axis=1), jnp.concatenate([wu[:, done:], wut], axis=1),
         jnp.concatenate([wd[done:, :], wdt], axis=0))
    return _ffn(xp, gain, w, layer, tm=tm, tf=tf), xs


def _inproj_kernel(x_ref, g_ref, w_ref, o_ref, h_ref):
    @pl.when(pl.program_id(1) == 0)
    def _():
        h_ref[...] = _rms_rows(x_ref[...], g_ref[...]).astype(BF16)

    o_ref[...] = jnp.dot(h_ref[...], w_ref[...], preferred_element_type=F32).astype(o_ref.dtype)


def _inproj(x, gain, w, layer, *, tm, tn):
    t, d = x.shape
    n = w.shape[2]
    return pl.pallas_call(
        _inproj_kernel,
        grid=(t // tm, n // tn),
        in_specs=[pl.BlockSpec((tm, d), lambda i, j: (i, 0)),
                  pl.BlockSpec((None, 1, d), lambda i, j: (layer, 0, 0)),
                  pl.BlockSpec((None, d, tn), lambda i, j: (layer, 0, j))],
        out_specs=pl.BlockSpec((tm, tn), lambda i, j: (i, j)),
        out_shape=jax.ShapeDtypeStruct((t, n), BF16),
        scratch_shapes=[pltpu.VMEM((tm, d), BF16)],
        compiler_params=_cparams(2),
        name="inproj",
    )(x, gain, w)


def _seg_sum(x, first):
    s0 = jnp.sum(jnp.where(first, x, 0.0), axis=1, keepdims=True)
    s1 = jnp.sum(jnp.where(first, 0.0, x), axis=1, keepdims=True)
    return jnp.where(first, s0, s1)


def _bd(x):
    w = x.shape[1]
    first = lax.broadcasted_iota(jnp.int32, (1, w), 1) < (w // 2)
    zero = jnp.zeros_like(x)
    return jnp.concatenate([jnp.where(first, x, zero), jnp.where(first, zero, x)], axis=0)


def _diag_blocks(m):
    n = m.shape[0] // 2
    first = lax.broadcasted_iota(jnp.int32, (1, 2 * n), 1) < n
    return jnp.where(first, m[0:n], m[n:2 * n])


def _rwkv_kernel(rkv_ref, l_ref, rkvp_ref, lp_ref, srkv_ref, sl_ref, vec_ref, mul_ref, wl1_ref, wl2_ref,
                 s0_ref, y_ref, sout_ref, z_ref, *, sb, tb, c, nb):
    b = pl.program_id(1)
    nc = tb // c
    hd = HEAD_DIM

    vec = vec_ref[...]
    mu_rkv = jnp.concatenate([vec[0:1], vec[1:2], vec[2:3]], axis=1)
    mu_l = mul_ref[...]

    lane = lax.broadcasted_iota(jnp.int32, (1, LANES), 1)
    first = lane < hd
    row_t = lax.broadcasted_iota(jnp.int32, (tb, 1), 0)
    ri = lax.broadcasted_iota(jnp.int32, (c, 2 * c), 0)
    ci = lax.broadcasted_iota(jnp.int32, (c, 2 * c), 1) % c
    lower_incl = ri >= ci
    lower_strict = ri > ci
    eye_p = jnp.where(ri == ci, 1.0, 0.0).astype(F32)
    rt_ = lax.broadcasted_iota(jnp.int32, (c, c), 0)
    ct_ = lax.broadcasted_iota(jnp.int32, (c, c), 1)
    tri = jnp.where(rt_ >= ct_, 1.0, 0.0).astype(BF16)
    rj = lax.broadcasted_iota(jnp.int32, (hd, LANES), 0)
    cj = lax.broadcasted_iota(jnp.int32, (hd, LANES), 1) % hd
    eye_h = rj == cj

    @pl.when(b == 0)
    def _():
        for s in range(sb):
            for p in range(PAIRS):
                z_ref[s, p] = jnp.concatenate([s0_ref[s, 2 * p].T, s0_ref[s, 2 * p + 1].T], axis=1)

    def shifted(ref, pref, sref, s, mu):
        raw = ref[s].astype(F32)
        before = jnp.where(b == 0, sref[s], pref[s][BF16_ROWS - 1:BF16_ROWS, :].astype(F32))
        prev = jnp.where(row_t == 0, before, pltpu.roll(raw, 1, 0))
        return raw + (prev - raw) * mu

    pre = {}
    for s in range(sb):
        x3 = shifted(rkv_ref, rkvp_ref, srkv_ref, s, mu_rkv)
        lx = shifted(l_ref, lp_ref, sl_ref, s, mu_l)
        l01 = lx[:, 0:LANES]
        z01 = jnp.where(first, jnp.tanh(l01), l01)
        wa = _dot(z01, wl1_ref[...])
        gate = _dot(jax.nn.sigmoid(lx[:, _LORA_G0:_LORA_G0 + _LORA_GW]), wl2_ref[...])
        for p in range(PAIRS):
            ps = slice(p * LANES, (p + 1) * LANES)
            w0, a0, k_k, k_a, r_k = vec[3:4, ps], vec[4:5, ps], vec[5:6, ps], vec[6:7, ps], vec[7:8, ps]
            rx = x3[:, _COL_R + p * LANES:_COL_R + (p + 1) * LANES]
            kx = x3[:, _COL_K + p * LANES:_COL_K + (p + 1) * LANES]
            vx = x3[:, _COL_V + p * LANES:_COL_V + (p + 1) * LANES]
            w_pre = wa[:, ps] + w0
            a_pre = wa[:, A_WIDTH + p * LANES:A_WIDTH + (p + 1) * LANES] + a0
            softplus = jnp.maximum(-w_pre, 0.0) + jnp.log1p(jnp.exp(-jnp.abs(w_pre)))
            logd = -jnp.exp(-softplus - 0.5)
            a = jax.nn.sigmoid(a_pre)
            kxk = kx * k_k
            kk = kxk / jnp.maximum(jnp.sqrt(_seg_sum(kxk * kxk, first)), 1e-12)
            kp = kx * (1.0 + (a - 1.0) * k_a)
            pre[(s, p)] = dict(r=rx, kk=kk, kp=kp, bb=kk * a, v=vx, logd=logd, gate=gate[:, ps],
                               bonus=_seg_sum(rx * kp * r_k, first) * vx)

    units = [(s, ch, p) for ch in range(nc) for s in range(sb) for p in range(PAIRS)]

    def rows(name, u):
        s, ch, p = units[u]
        return pre[(s, p)][name][ch * c:(ch + 1) * c]

    nu = len(units)
    cin = []
    for u in range(nu):
        hi, mid, lo = _split3(rows("logd", u))
        cs = jnp.dot(tri, jnp.concatenate([hi, mid, lo], axis=1), preferred_element_type=F32)
        cin.append(cs[:, 0:LANES] + cs[:, LANES:2 * LANES] + cs[:, 2 * LANES:3 * LANES])
    kk_t, r_t, k_d, b_d, p_end, g = [], [], [], [], [], []
    for u in range(nu):
        ld = rows("logd", u)
        c_end = cin[u][c - 1:c, :]
        e_inv = jnp.exp(-cin[u])
        e_dec = jnp.exp(c_end - cin[u])
        kk_t.append((rows("kk", u) * jnp.exp(cin[u] - ld)).astype(BF16))
        r_t.append(rows("r", u) * jnp.exp(cin[u]))
        k_d.append((rows("kp", u) * e_dec).astype(BF16))
        b_d.append((rows("bb", u) * e_dec).astype(BF16))
        p_end.append(jnp.exp(c_end))
        lhs = jnp.concatenate([kk_t[u], r_t[u].astype(BF16)], axis=0)
        rhs = jnp.concatenate([_bd((rows("bb", u) * e_inv).astype(BF16)),
                               _bd((rows("kp", u) * e_inv).astype(BF16))], axis=0)
        g.append(_dot_nt(lhs, rhs))
    m_ab = [jnp.where(lower_strict, g[u][0:c, 0:2 * c], 0.0).astype(BF16) for u in range(nu)]
    a_rb = [jnp.where(lower_incl, g[u][c:2 * c, 0:2 * c], 0.0).astype(BF16) for u in range(nu)]
    m_ak = [jnp.where(lower_strict, g[u][0:c, 2 * c:4 * c], 0.0).astype(BF16) for u in range(nu)]
    a_rk = [jnp.where(lower_incl, g[u][c:2 * c, 2 * c:4 * c], 0.0).astype(BF16) for u in range(nu)]
    v_bd = [_bd(rows("v", u).astype(BF16)) for u in range(nu)]
    x = [eye_p - m_ab[u].astype(F32) for u in range(nu)]
    pw = [_dot(m_ab[u], _bd(m_ab[u])) for u in range(nu)]
    k = 2
    while 2 * k < c:
        both = [_dot(jnp.concatenate([x[u].astype(BF16), pw[u].astype(BF16)], axis=0), _bd(pw[u].astype(BF16)))
                for u in range(nu)]
        x = [x[u] + both[u][0:c] for u in range(nu)]
        pw = [both[u][c:2 * c] for u in range(nu)]
        k *= 2
    t_inv = [(x[u] + _dot(x[u], _bd(pw[u].astype(BF16)))).astype(BF16) for u in range(nu)]
    mv = [_dot(m_ak[u], v_bd[u]) for u in range(nu)]
    tw = [_dot(t_inv[u], jnp.concatenate([_bd(kk_t[u]), _bd(mv[u].astype(BF16))], axis=1)).astype(BF16)
          for u in range(nu)]
    bt = [_dot_tn(b_d[u], tw[u]) for u in range(nu)]
    kv = [_dot_tn(k_d[u], rows("v", u)) for u in range(nu)]
    ab = [_dot(a_rb[u], jnp.concatenate([_bd(tw[u][:, 0:LANES]), _bd(tw[u][:, LANES:2 * LANES])], axis=1))
          for u in range(nu)]
    av = [_dot(a_rk[u], v_bd[u]) for u in range(nu)]
    lhs_z, u_c, y_i = [], [], []
    for u in range(nu):
        a_c = jnp.where(eye_h, p_end[u], 0.0) - _diag_blocks(bt[u][:, 0:LANES])
        q_e = r_t[u] - ab[u][:, 0:LANES]
        lhs_z.append(jnp.concatenate([a_c, q_e], axis=0).astype(BF16))
        u_c.append(_diag_blocks(kv[u]) - _diag_blocks(bt[u][:, LANES:2 * LANES]))
        y_i.append(av[u] - ab[u][:, LANES:2 * LANES])
    z = {(s, p): z_ref[s, p] for s in range(sb) for p in range(PAIRS)}
    per_level = sb * PAIRS
    for ch in range(nc):
        level = range(ch * per_level, (ch + 1) * per_level)
        both = {u: _dot(lhs_z[u], _bd(z[units[u][0], units[u][2]].astype(BF16))) for u in level}
        ys = {u: both[u][hd:hd + c] + y_i[u] for u in level}
        for u in level:
            z[units[u][0], units[u][2]] = both[u][0:hd] + u_c[u]
        means = {u: _seg_sum(ys[u], first) * (1.0 / hd) for u in level}
        ycs = {u: ys[u] - means[u] for u in level}
        variances = {u: _seg_sum(ycs[u] * ycs[u], first) * (1.0 / hd) for u in level}
        for u in level:
            s, _, p = units[u]
            ps = slice(p * LANES, (p + 1) * LANES)
            yn = ycs[u] * lax.rsqrt(variances[u] + GN_EPS)
            out = (yn * vec[8:9, ps] + vec[9:10, ps] + rows("bonus", u)) * rows("gate", u)
            y_ref[s, ch * c:(ch + 1) * c, ps] = out.astype(y_ref.dtype)
    for s in range(sb):
        for p in range(PAIRS):
            z_ref[s, p] = z[(s, p)]

    @pl.when(b == nb - 1)
    def _():
        for s in range(sb):
            for p in range(PAIRS):
                sout_ref[s, 2 * p] = z[(s, p)][:, 0:hd].T
                sout_ref[s, 2 * p + 1] = z[(s, p)][:, hd:2 * hd].T


def _rwkv(p3, shiftp, s0, vec, mu_l, wl1, wl2, layer, *, sb, tb, c):
    s, t, _ = p3.shape
    lb = _COL_LORA // _LORA_W
    tpb = tb // BF16_ROWS
    nb = t // tb

    def prev_tile(b):
        return jnp.maximum(b * tpb - 1, 0)

    in_specs = [
        pl.BlockSpec((sb, tb, _RKV_W), lambda i, b: (i, b, 0)),
        pl.BlockSpec((sb, tb, _LORA_W), lambda i, b: (i, b, lb)),
        pl.BlockSpec((sb, BF16_ROWS, _RKV_W), lambda i, b: (i, prev_tile(b), 0)),
        pl.BlockSpec((sb, BF16_ROWS, _LORA_W), lambda i, b: (i, prev_tile(b), lb)),
        pl.BlockSpec((sb, 1, _RKV_W), lambda i, b: (i, 0, 0)),
        pl.BlockSpec((sb, 1, _LORA_W), lambda i, b: (i, 0, _RKV_W // _LORA_W)),
        pl.BlockSpec((None, _VEC_ROWS, A_WIDTH), lambda i, b: (layer, 0, 0)),
        pl.BlockSpec((None, 1, _LORA_W), lambda i, b: (layer, 0, 0)),
        pl.BlockSpec((None, LANES, 2 * A_WIDTH), lambda i, b: (layer, 0, 0)),
        pl.BlockSpec((None, _LORA_GW, A_WIDTH), lambda i, b: (layer, 0, 0)),
        pl.BlockSpec((sb, A_HEADS, HEAD_DIM, HEAD_DIM), lambda i, b: (i, 0, 0, 0)),
    ]
    out_specs = [
        pl.BlockSpec((sb, tb, A_WIDTH), lambda i, b: (i, b, 0)),
        pl.BlockSpec((sb, A_HEADS, HEAD_DIM, HEAD_DIM), lambda i, b: (i, 0, 0, 0)),
    ]
    return pl.pallas_call(
        functools.partial(_rwkv_kernel, sb=sb, tb=tb, c=c, nb=nb),
        grid=(s // sb, nb),
        in_specs=in_specs,
        out_specs=out_specs,
        out_shape=[jax.ShapeDtypeStruct((s, t, A_WIDTH), BF16),
                   jax.ShapeDtypeStruct((s, A_HEADS, HEAD_DIM, HEAD_DIM), F32)],
        scratch_shapes=[pltpu.VMEM((sb, PAIRS, HEAD_DIM, LANES), F32)],
        compiler_params=_cparams(2),
        name="rwkv",
    )(p3, p3, p3, p3, shiftp, shiftp, vec, mu_l, wl1, wl2, s0)


def _norm_rope(slabs):
    lane = lax.broadcasted_iota(jnp.int32, (1, LANES), 1)
    first = lane < HEAD_DIM
    low_half = (lane % HEAD_DIM) < (HEAD_DIM // 2)
    tiles = [(n, x[:, j * LANES:(j + 1) * LANES]) for n, (x, _, _, _) in enumerate(slabs)
             for j in range(x.shape[1] // LANES)]
    sq = [xs * xs for _, xs in tiles]
    s0 = [jnp.sum(jnp.where(first, s, 0.0), axis=1, keepdims=True) for s in sq]
    s1 = [jnp.sum(jnp.where(first, 0.0, s), axis=1, keepdims=True) for s in sq]
    xn = [xs * lax.rsqrt(jnp.where(first, a, b) * (1.0 / HEAD_DIM) + RMS_EPS) * slabs[n][1]
          for (n, xs), a, b in zip(tiles, s0, s1)]
    up = [pltpu.roll(v, LANES - HEAD_DIM // 2, 1) for v in xn]
    down = [pltpu.roll(v, HEAD_DIM // 2, 1) for v in xn]
    out = [v * slabs[n][2] + jnp.where(low_half, u, w) * slabs[n][3]
           for (n, _), v, u, w in zip(tiles, xn, up, down)]
    res, o = [], 0
    for x, _, _, _ in slabs:
        k = x.shape[1] // LANES
        res.append(jnp.concatenate(out[o:o + k], axis=1) if k > 1 else out[o])
        o += k
    return res


def _sink_softmax_all(scores, sinks):
    ms = [jnp.maximum(jnp.max(s, axis=-1, keepdims=True), k) for s, k in zip(scores, sinks)]
    ps = [jnp.exp(s - m) for s, m in zip(scores, ms)]
    ds = [jnp.sum(p, axis=-1, keepdims=True) + jnp.exp(k - m) for p, k, m in zip(ps, sinks, ms)]
    return [(p / d).astype(BF16) for p, d in zip(ps, ds)]


def _sink_column(sink_ref, heads, rows):
    return jnp.concatenate([jnp.full((rows, 1), sink_ref[h], F32) for h in heads], axis=0)


_QK_SCALE = HEAD_DIM ** -0.5


def _band_kernel(sink_ref, q_ref, kv_ref, kvp_ref, cq_ref, sq_ref, cp_ref, sp_ref, gq_ref, gk_ref,
                 y_ref, ko_ref, *, tq):
    i = pl.program_id(0)
    hd = HEAD_DIM
    band = WINDOW + CHUNK
    kv = kv_ref[...]
    kvp = kvp_ref[...]
    q, k_cur, k_prev = _norm_rope([
        (q_ref[...].astype(F32), gq_ref[...], cq_ref[...], sq_ref[...]),
        (kv[:, 0:B_KV].astype(F32), gk_ref[...], cq_ref[...], sq_ref[...]),
        (kvp[:, 0:B_KV].astype(F32), gk_ref[...], cp_ref[...], sp_ref[...])])
    q = (q * _QK_SCALE).astype(BF16)
    ko_ref[...] = k_cur
    k_all = jnp.concatenate([k_prev.astype(BF16), k_cur.astype(BF16)], axis=0)
    v_all = jnp.concatenate([kvp[:, B_KV:2 * B_KV], kv[:, B_KV:2 * B_KV]], axis=0)
    k_g = [k_all[:, g * hd:(g + 1) * hd] for g in range(B_KV_HEADS)]
    v_g = [v_all[:, g * hd:(g + 1) * hd] for g in range(B_KV_HEADS)]
    col = lax.broadcasted_iota(jnp.int32, (1, band), 1)
    sink_g = [_sink_column(sink_ref, range(g * B_GROUP, (g + 1) * B_GROUP), CHUNK) for g in range(B_KV_HEADS)]

    jobs = [(k0, g) for k0 in range(0, tq, CHUNK) for g in range(B_KV_HEADS)]
    q_rows = [jnp.concatenate([q[k0:k0 + CHUNK, (g * B_GROUP + j) * hd:(g * B_GROUP + j + 1) * hd]
                               for j in range(B_GROUP)], axis=0) for k0, g in jobs]
    scores = [_dot_nt(qr, k_g[g][k0:k0 + band]) for (k0, g), qr in zip(jobs, q_rows)]
    scores = [jnp.where((col >= WINDOW - k0) | (i > 0), s, -1e30) for (k0, g), s in zip(jobs, scores)]
    probs = _sink_softmax_all(scores, [sink_g[g] for k0, g in jobs])
    outs = {job: _dot(p, v_g[job[1]][job[0]:job[0] + band]) for job, p in zip(jobs, probs)}
    for k0 in range(0, tq, CHUNK):
        heads = [outs[(k0, g)][j * CHUNK:(j + 1) * CHUNK] for g in range(B_KV_HEADS) for j in range(B_GROUP)]
        y_ref[k0:k0 + CHUNK, :] = jnp.concatenate(heads, axis=1).astype(y_ref.dtype)


def _attn_band(p2, cos, sin_signed, gq, gk, sinks, layer, *, tq):
    t = p2.shape[0]
    qb, kvb = _COL_Q // B_Q, _COL_KV // _KV_W
    wpb = tq // WINDOW

    def prev_idx(i):
        return jnp.maximum(i * wpb - 1, 0)

    return pl.pallas_call(
        functools.partial(_band_kernel, tq=tq),
        grid=(t // tq,),
        in_specs=[pl.BlockSpec(memory_space=pltpu.SMEM),
                  pl.BlockSpec((tq, B_Q), lambda i: (i, qb)),
                  pl.BlockSpec((tq, _KV_W), lambda i: (i, kvb)),
                  pl.BlockSpec((WINDOW, _KV_W), lambda i: (prev_idx(i), kvb)),
                  pl.BlockSpec((tq, LANES), lambda i: (i, 0)),
                  pl.BlockSpec((tq, LANES), lambda i: (i, 0)),
                  pl.BlockSpec((WINDOW, LANES), lambda i: (prev_idx(i), 0)),
                  pl.BlockSpec((WINDOW, LANES), lambda i: (prev_idx(i), 0)),
                  pl.BlockSpec((None, 1, LANES), lambda i: (layer, 0, 0)),
                  pl.BlockSpec((None, 1, LANES), lambda i: (layer, 0, 0))],
        out_specs=[pl.BlockSpec((tq, B_Q), lambda i: (i, 0)),
                   pl.BlockSpec((tq, B_KV), lambda i: (i, 0))],
        out_shape=[jax.ShapeDtypeStruct((t, B_Q), BF16),
                   jax.ShapeDtypeStruct((t, B_KV), F32)],
        compiler_params=_cparams(1),
        name="attn_band",
    )(sinks[layer], p2, p2, p2, cos, sin_signed, cos, sin_signed, gq, gk)


def _cached_kernel(sink_ref, q_ref, kv_ref, ck_ref, cv_ref, c_ref, s_ref, gq_ref, gk_ref,
                   y_ref, ko_ref, *, sb):
    hd = HEAD_DIM
    tn = q_ref.shape[1]
    jobs = [(s, g) for s in range(sb) for g in range(B_KV_HEADS)]
    q_rows, k_all, v_all = {}, {}, {}
    cos = jnp.concatenate([c_ref[...]] * sb, axis=0)
    sin = jnp.concatenate([s_ref[...]] * sb, axis=0)
    q_cat = jnp.concatenate([q_ref[s] for s in range(sb)], axis=0).astype(F32)
    kv_cat = jnp.concatenate([kv_ref[s] for s in range(sb)], axis=0)
    q_cat, k_cat_new = _norm_rope([(q_cat, gq_ref[...], cos, sin),
                                   (kv_cat[:, 0:B_KV].astype(F32), gk_ref[...], cos, sin)])
    q_cat = (q_cat * _QK_SCALE).astype(BF16)
    for s in range(sb):
        q = q_cat[s * tn:(s + 1) * tn]
        k_new = k_cat_new[s * tn:(s + 1) * tn]
        ko_ref[s] = k_new
        k_cat = jnp.concatenate([ck_ref[s].astype(BF16), k_new.astype(BF16)], axis=0)
        v_cat = jnp.concatenate([cv_ref[s].astype(BF16), kv_cat[s * tn:(s + 1) * tn, B_KV:2 * B_KV]], axis=0)
        for g in range(B_KV_HEADS):
            q_rows[(s, g)] = jnp.concatenate(
                [q[:, (g * B_GROUP + j) * hd:(g * B_GROUP + j + 1) * hd] for j in range(B_GROUP)], axis=0)
            k_all[(s, g)] = k_cat[:, g * hd:(g + 1) * hd]
            v_all[(s, g)] = v_cat[:, g * hd:(g + 1) * hd]
    scores = [_dot_nt(q_rows[j], k_all[j]) for j in jobs]
    sink_g = [_sink_column(sink_ref, range(g * B_GROUP, (g + 1) * B_GROUP), tn) for g in range(B_KV_HEADS)]
    probs = _sink_softmax_all(scores, [sink_g[g] for s, g in jobs])
    outs = {j: _dot(p, v_all[j]) for j, p in zip(jobs, probs)}
    for s in range(sb):
        heads = [outs[(s, g)][j * tn:(j + 1) * tn] for g in range(B_KV_HEADS) for j in range(B_GROUP)]
        y_ref[s] = jnp.concatenate(heads, axis=1).astype(y_ref.dtype)


def _attn_cached(p3, ck, cv, cos, sin_signed, gq, gk, sinks, layer, *, sb):
    s, t, _ = p3.shape
    w = ck.shape[2]
    qb, kvb = _COL_Q // B_Q, _COL_KV // _KV_W
    return pl.pallas_call(
        functools.partial(_cached_kernel, sb=sb),
        grid=(s // sb,),
        in_specs=[pl.BlockSpec(memory_space=pltpu.SMEM),
                  pl.BlockSpec((sb, t, B_Q), lambda i: (i, 0, qb)),
                  pl.BlockSpec((sb, t, _KV_W), lambda i: (i, 0, kvb)),
                  pl.BlockSpec((None, sb, w, B_KV), lambda i: (layer, i, 0, 0)),
                  pl.BlockSpec((None, sb, w, B_KV), lambda i: (layer, i, 0, 0)),
                  pl.BlockSpec((t, LANES), lambda i: (0, 0)),
                  pl.BlockSpec((t, LANES), lambda i: (0, 0)),
                  pl.BlockSpec((None, 1, LANES), lambda i: (layer, 0, 0)),
                  pl.BlockSpec((None, 1, LANES), lambda i: (layer, 0, 0))],
        out_specs=[pl.BlockSpec((sb, t, B_Q), lambda i: (i, 0, 0)),
                   pl.BlockSpec((sb, t, B_KV), lambda i: (i, 0, 0))],
        out_shape=[jax.ShapeDtypeStruct((s, t, B_Q), BF16),
                   jax.ShapeDtypeStruct((s, t, B_KV), F32)],
        compiler_params=_cparams(1),
        name="attn_cached",
    )(sinks[layer], p3, p3, ck, cv, cos, sin_signed, gq, gk)


def _outproj_kernel(x_ref, ya_ref, yb_ref, ga_ref, gb_ref, wa_ref, wb_ref, wo_ref, o_ref):
    ua = jnp.dot(ya_ref[...], wa_ref[...], preferred_element_type=F32)
    ub = jnp.dot(yb_ref[...], wb_ref[...], preferred_element_type=F32)
    merged = (jax.nn.sigmoid(ga_ref[...].astype(F32)) * ua + jax.nn.sigmoid(gb_ref[...].astype(F32)) * ub)
    o_ref[...] = x_ref[...] + jnp.dot(merged.astype(BF16), wo_ref[...], preferred_element_type=F32)


def _outproj(x, ya, yb, p2, wa, wb, wo, layer, *, tm):
    t, d = x.shape
    gab, gbb = _COL_GA // d, _COL_GB // d
    return pl.pallas_call(
        _outproj_kernel,
        grid=(t // tm,),
        in_specs=[pl.BlockSpec((tm, d), lambda i: (i, 0)),
                  pl.BlockSpec((tm, A_WIDTH), lambda i: (i, 0)),
                  pl.BlockSpec((tm, B_Q), lambda i: (i, 0)),
                  pl.BlockSpec((tm, d), lambda i: (i, gab)),
                  pl.BlockSpec((tm, d), lambda i: (i, gbb)),
                  _single_buffered((None, A_WIDTH, d), lambda i: (layer, 0, 0)),
                  _single_buffered((None, B_Q, d), lambda i: (layer, 0, 0)),
                  _single_buffered((None, d, d), lambda i: (layer, 0, 0))],
        out_specs=pl.BlockSpec((tm, d), lambda i: (i, 0)),
        out_shape=jax.ShapeDtypeStruct((t, d), F32),
        compiler_params=_cparams(1),
        name="outproj",
    )(x, ya, yb, p2, p2, wa, wb, wo)


def _a_pieces(a):
    o = 0
    out = []
    for w in (A_WIDTH, DECAY_LORA, A_WIDTH, A_WIDTH, ICLR_LORA, GATE_LORA):
        out.append(a[..., o:o + w])
        o += w
    return out


def _pad_last(w, n):
    return jnp.pad(w, [(0, 0)] * (w.ndim - 1) + [(0, n - w.shape[-1])])


def _regroup_a(a):
    r, w_lo, k, v, a_lo, g_lo = _a_pieces(a)
    lora = _pad_last(jnp.concatenate([w_lo, a_lo, g_lo], axis=-1), _LORA_W)
    return jnp.concatenate([r, k, v], axis=-1), lora


def _regroup_w_in(w):
    o = A_COLS
    wq = w[..., o:o + B_Q]
    wk = w[..., o + B_Q:o + B_Q + B_KV]
    wv = w[..., o + B_Q + B_KV:o + B_Q + 2 * B_KV]
    o += B_Q + 2 * B_KV
    wga = w[..., o:o + D_MODEL]
    wgb = w[..., o + D_MODEL:o + 2 * D_MODEL]
    rkv, lora = _regroup_a(w[..., :A_COLS])
    return jnp.concatenate([rkv, wq, wga, wgb, lora, wk, wv], axis=-1).astype(BF16)


def _shift_row_to_a(p_last):
    p_last = p_last.astype(F32)
    r = p_last[..., _COL_R:_COL_R + A_WIDTH]
    k = p_last[..., _COL_K:_COL_K + A_WIDTH]
    v = p_last[..., _COL_V:_COL_V + A_WIDTH]
    o = _COL_LORA
    w_lo = p_last[..., o:o + DECAY_LORA]
    a_lo = p_last[..., o + DECAY_LORA:o + DECAY_LORA + ICLR_LORA]
    g_lo = p_last[..., o + DECAY_LORA + ICLR_LORA:o + DECAY_LORA + ICLR_LORA + GATE_LORA]
    return jnp.concatenate([r, w_lo, k, v, a_lo, g_lo], axis=-1)


def _lora_weights(decay_w2, iclr_a2, gate_g2):
    top = jnp.concatenate([decay_w2, jnp.zeros_like(decay_w2)], axis=-1)
    bot = jnp.concatenate([jnp.zeros_like(iclr_a2), iclr_a2], axis=-1)
    wl1 = jnp.concatenate([top, bot], axis=-2)
    wl2 = jnp.pad(gate_g2, ((0, 0), (0, _LORA_GW - GATE_LORA), (0, 0)))
    return wl1.astype(BF16), wl2.astype(BF16)


def _rope_tables(pos):
    half = HEAD_DIM // 2
    inv = ROPE_THETA ** (-jnp.arange(half, dtype=F32) / half)
    ang = pos.astype(F32)[:, None] * inv[None, :]
    cos = jnp.cos(ang)
    sin = jnp.sin(ang)
    cos_t = jnp.tile(cos, (1, LANES // half))
    sin_t = jnp.tile(jnp.concatenate([-sin, sin], axis=1), (1, LANES // HEAD_DIM))
    return cos_t, sin_t


_FF_TILE = 512
_FF_CAST_TILE = 256


def kernel(x_prompt, x_sample, cache_k, cache_v, state_wkv, state_shift, norm_ff1, ff1_gate, ff1_up, ff1_down,
           norm_mix, w_in, shift_mu, decay_w0, decay_w2, iclr_a0, iclr_a2, gate_g2, k_k, k_a, r_k, gn_gain,
           gn_bias, q_norm, k_norm, sinks, w_up_a, w_up_b, w_o, norm_ff2, ff2_gate, ff2_up, ff2_down):
    depth = norm_ff1.shape[0]
    bp, tp, d = x_prompt.shape
    bs, ts, _ = x_sample.shape
    assert bp == 1 and d == D_MODEL

    d_ff = ff1_gate.shape[2]
    cast_tail = d_ff % _FF_CAST_TILE
    assert cast_tail % LANES == 0 and cast_tail > 0
    ff1_tails = _ffn_tails(ff1_gate, ff1_up, ff1_down, cast_tail)
    ff2_tails = _ffn_tails(ff2_gate, ff2_up, ff2_down, cast_tail)
    g_ff1 = norm_ff1[:, None, :]
    g_ff2 = norm_ff2[:, None, :]
    g_mix = norm_mix[:, None, :]
    w_in_b = _regroup_w_in(w_in)
    mu_rkv, mu_l = _regroup_a(shift_mu)
    mu_l = mu_l[:, None, :]
    rows = [mu_rkv[:, 0:A_WIDTH], mu_rkv[:, A_WIDTH:2 * A_WIDTH], mu_rkv[:, 2 * A_WIDTH:], decay_w0, iclr_a0,
            k_k, k_a, r_k.reshape(depth, A_WIDTH), gn_gain, gn_bias]
    vec = jnp.stack(rows + [jnp.zeros((depth, A_WIDTH), F32)] * (_VEC_ROWS - len(rows)), axis=1)
    wl1, wl2 = _lora_weights(decay_w2, iclr_a2, gate_g2)
    gq = jnp.tile(q_norm, (1, LANES // HEAD_DIM))[:, None, :]
    gk = jnp.tile(k_norm, (1, LANES // HEAD_DIM))[:, None, :]
    wa = w_up_a.astype(BF16)
    wb = w_up_b.astype(BF16)
    wo = w_o.astype(BF16)
    s_rkv, s_lora = _regroup_a(state_shift)
    shift_s = jnp.concatenate([s_rkv, s_lora], axis=-1)[:, :, None, :]
    zero_shift = jnp.zeros((bp, 1, _RKV_W + _LORA_W), F32)
    zero_state = jnp.zeros((bp, A_HEADS, HEAD_DIM, HEAD_DIM), F32)
    ck = cache_k.reshape(depth, bs, -1, B_KV)
    cv = cache_v.reshape(depth, bs, -1, B_KV)
    cos_p, sin_p = _rope_tables(jnp.arange(tp))
    cos_s, sin_s = _rope_tables(PAST_LEN + jnp.arange(ts))

    xp = x_prompt.reshape(tp, d)
    xs = x_sample.reshape(bs * ts, d)
    outs = {k: [] for k in ("p_wkv", "p_shift", "p_k", "p_v", "s_wkv", "s_shift", "s_k", "s_v")}
    for l in range(depth):
        xp, xs = _ffn_pair(xp, xs, g_ff1, ff1_gate, ff1_up, ff1_down, ff1_tails, l,
                           tm=512, tf=_FF_TILE, tf_cast=_FF_CAST_TILE)

        pp = _inproj(xp, g_mix, w_in_b, l, tm=1024, tn=1024)
        ps = _inproj(xs, g_mix, w_in_b, l, tm=512, tn=1024)
        pp3 = pp.reshape(bp, tp, NP_COLS)
        ps3 = ps.reshape(bs, ts, NP_COLS)

        ya_p, wkv_p = _rwkv(pp3, zero_shift, zero_state, vec, mu_l, wl1, wl2, l, sb=1, tb=4 * CHUNK, c=CHUNK)
        ya_s, wkv_s = _rwkv(ps3, shift_s[l], state_wkv[l], vec, mu_l, wl1, wl2, l, sb=2, tb=ts, c=ts)

        yb_p, kr_p = _attn_band(pp, cos_p, sin_p, gq, gk, sinks, l, tq=2 * WINDOW)
        yb_s, kr_s = _attn_cached(ps3, ck, cv, cos_s, sin_s, gq, gk, sinks, l, sb=8)

        xp = _outproj(xp, ya_p.reshape(tp, A_WIDTH), yb_p, pp, wa, wb, wo, l, tm=256)
        xs = _outproj(xs, ya_s.reshape(bs * ts, A_WIDTH), yb_s.reshape(bs * ts, B_Q), ps, wa, wb, wo, l, tm=256)

        xp, xs = _ffn_pair(xp, xs, g_ff2, ff2_gate, ff2_up, ff2_down, ff2_tails, l,
                           tm=512, tf=_FF_TILE, tf_cast=_FF_CAST_TILE)

        vcol = _COL_KV + B_KV
        outs["p_wkv"].append(wkv_p)
        outs["p_shift"].append(_shift_row_to_a(pp3[:, -1, :]))
        outs["p_k"].append(kr_p[-WINDOW:].reshape(bp, WINDOW, B_KV_HEADS, HEAD_DIM))
        outs["p_v"].append(pp3[:, -WINDOW:, vcol:vcol + B_KV].astype(F32).reshape(bp, WINDOW, B_KV_HEADS, HEAD_DIM))
        outs["s_wkv"].append(wkv_s)
        outs["s_shift"].append(_shift_row_to_a(ps3[:, -1, :]))
        outs["s_k"].append(kr_s.reshape(bs, ts, B_KV_HEADS, HEAD_DIM))
        outs["s_v"].append(ps3[:, :, vcol:vcol + B_KV].astype(F32).reshape(bs, ts, B_KV_HEADS, HEAD_DIM))

    return (xp.reshape(bp, tp, d), xs.reshape(bs, ts, d),
            jnp.stack(outs["p_wkv"]), jnp.stack(outs["p_shift"]), jnp.stack(outs["p_k"]), jnp.stack(outs["p_v"]),
            jnp.stack(outs["s_wkv"]), jnp.stack(outs["s_shift"]), jnp.stack(outs["s_k"]), jnp.stack(outs["s_v"]))
```

```python
import functools

import jax
import jax.numpy as jnp
from jax import lax
from jax.experimental import pallas as pl
from jax.experimental.pallas import tpu as pltpu

F32 = jnp.float32
BF16 = jnp.bfloat16

HEAD_DIM = 64
A_WIDTH = 1024
A_HEADS = A_WIDTH // HEAD_DIM
DECAY_LORA = 64
ICLR_LORA = 64
GATE_LORA = 160
GN_EPS = 64e-5
RMS_EPS = 1e-6
B_HEADS = 16
B_KV_HEADS = 4
B_GROUP = B_HEADS // B_KV_HEADS
B_Q = B_HEADS * HEAD_DIM
B_KV = B_KV_HEADS * HEAD_DIM
CHUNK = 64
WINDOW = 128
ROPE_THETA = 10000.0
PAST_LEN = 1024
D_MODEL = 2048
A_COLS = 3 * A_WIDTH + DECAY_LORA + ICLR_LORA + GATE_LORA

LANES = 128
BF16_ROWS = 16
VMEM_LIMIT_BYTES = 56 * 1024 * 1024

_COL_R = 0
_COL_K = _COL_R + A_WIDTH
_COL_V = _COL_K + A_WIDTH
_RKV_W = 3 * A_WIDTH
_COL_Q = _COL_V + A_WIDTH
_COL_GA = _COL_Q + B_Q
_COL_GB = _COL_GA + D_MODEL
_COL_LORA = _COL_GB + D_MODEL
_LORA_W = 512
_COL_KV = _COL_LORA + _LORA_W
_KV_W = 2 * B_KV
NP_COLS = _COL_KV + _KV_W
_LORA_G0 = LANES
_LORA_GW = 2 * LANES
PAIRS = A_HEADS // 2
_VEC_ROWS = 16


def _cparams(n_axes):
    return pltpu.CompilerParams(dimension_semantics=("arbitrary",) * n_axes,
                                vmem_limit_bytes=VMEM_LIMIT_BYTES)


def _dot(a, b):
    return jnp.dot(a.astype(BF16), b.astype(BF16), preferred_element_type=F32)


def _dot_nt(a, b):
    return lax.dot_general(a.astype(BF16), b.astype(BF16), (((1,), (1,)), ((), ())),
                           preferred_element_type=F32)


def _dot_tn(a, b):
    return lax.dot_general(a.astype(BF16), b.astype(BF16), (((0,), (0,)), ((), ())),
                           preferred_element_type=F32)


def _split3(x):
    hi = x.astype(BF16)
    r1 = x - hi.astype(F32)
    mid = r1.astype(BF16)
    lo = (r1 - mid.astype(F32)).astype(BF16)
    return hi, mid, lo


def _rms_rows(x, gain):
    ms = jnp.mean(x * x, axis=-1, keepdims=True)
    return x * lax.rsqrt(ms + RMS_EPS) * gain


def _single_buffered(shape, index_map):
    return pl.BlockSpec(shape, index_map, pipeline_mode=pl.Buffered(1))


def _ffn_init(x_ref, g_ref, o_ref, h_ref):
    x = x_ref[...]
    h_ref[...] = _rms_rows(x, g_ref[...]).astype(BF16)
    o_ref[...] = x


def _swiglu_accumulate(h_ref, o_ref, wg, wu, wd):
    h = h_ref[...]
    gate = jnp.dot(h, wg, preferred_element_type=F32)
    up = jnp.dot(h, wu, preferred_element_type=F32)
    act = (0.5 * gate * jax.nn.sigmoid(gate) * up).astype(BF16)
    o_ref[...] += jnp.dot(act, wd, preferred_element_type=F32)


def _ffn_kernel(x_ref, g_ref, wg_ref, wu_ref, wd_ref, wgt_ref, wut_ref, wdt_ref, o_ref, h_ref, *, n_full):
    f = pl.program_id(1)

    @pl.when(f == 0)
    def _():
        _ffn_init(x_ref, g_ref, o_ref, h_ref)

    @pl.when(f < n_full)
    def _():
        _swiglu_accumulate(h_ref, o_ref, wg_ref[...], wu_ref[...], wd_ref[...])

    @pl.when(f == n_full)
    def _():
        _swiglu_accumulate(h_ref, o_ref, wgt_ref[...], wut_ref[...], wdt_ref[...])


def _ffn(x, gain, w, layer, *, tm, tf):
    wg, wu, wd, wgt, wut, wdt = w
    t, d = x.shape
    n_full = wg.shape[1] // tf
    tail = wgt.shape[1]
    last = n_full - 1
    return pl.pallas_call(
        functools.partial(_ffn_kernel, n_full=n_full),
        grid=(t // tm, n_full + 1),
        in_specs=[pl.BlockSpec((tm, d), lambda i, f: (i, 0)),
                  pl.BlockSpec((None, 1, d), lambda i, f: (layer, 0, 0)),
                  pl.BlockSpec((d, tf), lambda i, f: (0, jnp.minimum(f, last))),
                  pl.BlockSpec((d, tf), lambda i, f: (0, jnp.minimum(f, last))),
                  pl.BlockSpec((tf, d), lambda i, f: (jnp.minimum(f, last), 0)),
                  _single_buffered((d, tail), lambda i, f: (0, 0)),
                  _single_buffered((d, tail), lambda i, f: (0, 0)),
                  _single_buffered((tail, d), lambda i, f: (0, 0))],
        out_specs=pl.BlockSpec((tm, d), lambda i, f: (i, 0)),
        out_shape=jax.ShapeDtypeStruct((t, d), F32),
        scratch_shapes=[pltpu.VMEM((tm, d), BF16)],
        compiler_params=_cparams(2),
        name="ffn",
    )(x, gain, wg, wu, wd, wgt, wut, wdt)


def _ffn_cast_kernel(x_ref, g_ref, wg_ref, wu_ref, wd_ref, wgt_ref, wut_ref, wdt_ref,
                     o_ref, wg16_ref, wu16_ref, wd16_ref, wgt16_ref, wut16_ref, wdt16_ref, h_ref, *, n_full):
    f = pl.program_id(0)

    @pl.when(f == 0)
    def _():
        _ffn_init(x_ref, g_ref, o_ref, h_ref)

    def step(src, dst):
        w = [r[...].astype(BF16) for r in src]
        for r, v in zip(dst, w):
            r[...] = v
        _swiglu_accumulate(h_ref, o_ref, *w)

    @pl.when(f < n_full)
    def _():
        step((wg_ref, wu_ref, wd_ref), (wg16_ref, wu16_ref, wd16_ref))

    @pl.when(f == n_full)
    def _():
        step((wgt_ref, wut_ref, wdt_ref), (wgt16_ref, wut16_ref, wdt16_ref))


def _ffn_cast(x, gain, gate, up, down, layer, *, tf, tail):
    t, d = x.shape
    d_ff = gate.shape[2]
    n_full = (d_ff - tail) // tf
    assert n_full * tf + tail == d_ff and d_ff % tail == 0
    last = n_full - 1
    tail_block = d_ff // tail - 1
    wide = pl.BlockSpec((None, d, tf), lambda f: (layer, 0, jnp.minimum(f, last)))
    tall = pl.BlockSpec((None, tf, d), lambda f: (layer, jnp.minimum(f, last), 0))
    wide16 = pl.BlockSpec((d, tf), lambda f: (0, jnp.minimum(f, last)))
    tall16 = pl.BlockSpec((tf, d), lambda f: (jnp.minimum(f, last), 0))
    return pl.pallas_call(
        functools.partial(_ffn_cast_kernel, n_full=n_full),
        grid=(n_full + 1,),
        in_specs=[_single_buffered((t, d), lambda f: (0, 0)),
                  pl.BlockSpec((None, 1, d), lambda f: (layer, 0, 0)),
                  wide, wide, tall,
                  _single_buffered((None, d, tail), lambda f: (layer, 0, tail_block)),
                  _single_buffered((None, d, tail), lambda f: (layer, 0, tail_block)),
                  _single_buffered((None, tail, d), lambda f: (layer, tail_block, 0))],
        out_specs=[pl.BlockSpec((t, d), lambda f: (0, 0)),
                   wide16, wide16, tall16,
                   pl.BlockSpec((d, tail), lambda f: (0, 0)),
                   pl.BlockSpec((d, tail), lambda f: (0, 0)),
                   pl.BlockSpec((tail, d), lambda f: (0, 0))],
        out_shape=[jax.ShapeDtypeStruct((t, d), F32),
                   jax.ShapeDtypeStruct((d, n_full * tf), BF16),
                   jax.ShapeDtypeStruct((d, n_full * tf), BF16),
                   jax.ShapeDtypeStruct((n_full * tf, d), BF16),
                   jax.ShapeDtypeStruct((d, tail), BF16),
                   jax.ShapeDtypeStruct((d, tail), BF16),
                   jax.ShapeDtypeStruct((tail, d), BF16)],
        scratch_shapes=[pltpu.VMEM((t, d), BF16)],
        compiler_params=_cparams(1),
        name="ffn_cast",
    )(x, gain, gate, up, down, gate, up, down)


def _ffn_pair(xp, xs, gain, gate, up, down, layer, *, tm, tf, tf_cast):
    tail = gate.shape[2] % tf_cast
    xs, wg, wu, wd, wgt, wut, wdt = _ffn_cast(xs, gain, gate, up, down, layer, tf=tf_cast, tail=tail)
    done = (wg.shape[1] // tf) * tf
    w = (wg, wu, wd,
         jnp.concatenate([wg[:, done:], wgt], axis=1), jnp.concatenate([wu[:, done:], wut], axis=1),
         jnp.concatenate([wd[done:, :], wdt], axis=0))
    return _ffn(xp, gain, w, layer, tm=tm, tf=tf), xs


def _inproj_kernel(x_ref, g_ref, w_ref, o_ref, h_ref):
    @pl.when(pl.program_id(1) == 0)
    def _():
        h_ref[...] = _rms_rows(x_ref[...], g_ref[...]).astype(BF16)

    o_ref[...] = jnp.dot(h_ref[...], w_ref[...], preferred_element_type=F32).astype(o_ref.dtype)


def _inproj(x, gain, w, layer, *, tm, tn):
    t, d = x.shape
    n = w.shape[2]
    return pl.pallas_call(
        _inproj_kernel,
        grid=(t // tm, n // tn),
        in_specs=[pl.BlockSpec((tm, d), lambda i, j: (i, 0)),
                  pl.BlockSpec((None, 1, d), lambda i, j: (layer, 0, 0)),
                  pl.BlockSpec((None, d, tn), lambda i, j: (layer, 0, j))],
        out_specs=pl.BlockSpec((tm, tn), lambda i, j: (i, j)),
        out_shape=jax.ShapeDtypeStruct((t, n), BF16),
        scratch_shapes=[pltpu.VMEM((tm, d), BF16)],
        compiler_params=_cparams(2),
        name="inproj",
    )(x, gain, w)


def _seg_sum(x, first):
    s0 = jnp.sum(jnp.where(first, x, 0.0), axis=1, keepdims=True)
    s1 = jnp.sum(jnp.where(first, 0.0, x), axis=1, keepdims=True)
    return jnp.where(first, s0, s1)


def _bd(x):
    w = x.shape[1]
    first = lax.broadcasted_iota(jnp.int32, (1, w), 1) < (w // 2)
    zero = jnp.zeros_like(x)
    return jnp.concatenate([jnp.where(first, x, zero), jnp.where(first, zero, x)], axis=0)


def _diag_blocks(m):
    n = m.shape[0] // 2
    first = lax.broadcasted_iota(jnp.int32, (1, 2 * n), 1) < n
    return jnp.where(first, m[0:n], m[n:2 * n])


def _rwkv_kernel(rkv_ref, l_ref, rkvp_ref, lp_ref, srkv_ref, sl_ref, vec_ref, mul_ref, wl1_ref, wl2_ref,
                 s0_ref, y_ref, sout_ref, z_ref, *, sb, tb, c, nb):
    b = pl.program_id(1)
    nc = tb // c
    hd = HEAD_DIM

    vec = vec_ref[...]
    mu_rkv = jnp.concatenate([vec[0:1], vec[1:2], vec[2:3]], axis=1)
    mu_l = mul_ref[...]

    lane = lax.broadcasted_iota(jnp.int32, (1, LANES), 1)
    first = lane < hd
    row_t = lax.broadcasted_iota(jnp.int32, (tb, 1), 0)
    ri = lax.broadcasted_iota(jnp.int32, (c, 2 * c), 0)
    ci = lax.broadcasted_iota(jnp.int32, (c, 2 * c), 1) % c
    lower_incl = ri >= ci
    lower_strict = ri > ci
    eye_p = jnp.where(ri == ci, 1.0, 0.0).astype(F32)
    rt_ = lax.broadcasted_iota(jnp.int32, (c, c), 0)
    ct_ = lax.broadcasted_iota(jnp.int32, (c, c), 1)
    tri = jnp.where(rt_ >= ct_, 1.0, 0.0).astype(BF16)
    rj = lax.broadcasted_iota(jnp.int32, (hd, LANES), 0)
    cj = lax.broadcasted_iota(jnp.int32, (hd, LANES), 1) % hd
    eye_h = rj == cj

    @pl.when(b == 0)
    def _():
        for s in range(sb):
            for p in range(PAIRS):
                z_ref[s, p] = jnp.concatenate([s0_ref[s, 2 * p].T, s0_ref[s, 2 * p + 1].T], axis=1)

    def shifted(ref, pref, sref, s, mu):
        raw = ref[s].astype(F32)
        before = jnp.where(b == 0, sref[s], pref[s][BF16_ROWS - 1:BF16_ROWS, :].astype(F32))
        prev = jnp.where(row_t == 0, before, pltpu.roll(raw, 1, 0))
        return raw + (prev - raw) * mu

    pre = {}
    for s in range(sb):
        x3 = shifted(rkv_ref, rkvp_ref, srkv_ref, s, mu_rkv)
        lx = shifted(l_ref, lp_ref, sl_ref, s, mu_l)
        l01 = lx[:, 0:LANES]
        z01 = jnp.where(first, jnp.tanh(l01), l01)
        wa = _dot(z01, wl1_ref[...])
        gate = _dot(jax.nn.sigmoid(lx[:, _LORA_G0:_LORA_G0 + _LORA_GW]), wl2_ref[...])
        for p in range(PAIRS):
            ps = slice(p * LANES, (p + 1) * LANES)
            w0, a0, k_k, k_a, r_k = vec[3:4, ps], vec[4:5, ps], vec[5:6, ps], vec[6:7, ps], vec[7:8, ps]
            rx = x3[:, _COL_R + p * LANES:_COL_R + (p + 1) * LANES]
            kx = x3[:, _COL_K + p * LANES:_COL_K + (p + 1) * LANES]
            vx = x3[:, _COL_V + p * LANES:_COL_V + (p + 1) * LANES]
            w_pre = wa[:, ps] + w0
            a_pre = wa[:, A_WIDTH + p * LANES:A_WIDTH + (p + 1) * LANES] + a0
            softplus = jnp.maximum(-w_pre, 0.0) + jnp.log1p(jnp.exp(-jnp.abs(w_pre)))
            logd = -jnp.exp(-softplus - 0.5)
            a = jax.nn.sigmoid(a_pre)
            kxk = kx * k_k
            kk = kxk / jnp.maximum(jnp.sqrt(_seg_sum(kxk * kxk, first)), 1e-12)
            kp = kx * (1.0 + (a - 1.0) * k_a)
            pre[(s, p)] = dict(r=rx, kk=kk, kp=kp, bb=kk * a, v=vx, logd=logd, gate=gate[:, ps],
                               bonus=_seg_sum(rx * kp * r_k, first) * vx)

    units = [(s, ch, p) for ch in range(nc) for s in range(sb) for p in range(PAIRS)]

    def rows(name, u):
        s, ch, p = units[u]
        return pre[(s, p)][name][ch * c:(ch + 1) * c]

    nu = len(units)
    cin = []
    for u in range(nu):
        hi, mid, lo = _split3(rows("logd", u))
        cs = jnp.dot(tri, jnp.concatenate([hi, mid, lo], axis=1), preferred_element_type=F32)
        cin.append(cs[:, 0:LANES] + cs[:, LANES:2 * LANES] + cs[:, 2 * LANES:3 * LANES])
    kk_t, r_t, k_d, b_d, p_end, g = [], [], [], [], [], []
    for u in range(nu):
        ld = rows("logd", u)
        c_end = cin[u][c - 1:c, :]
        e_inv = jnp.exp(-cin[u])
        e_dec = jnp.exp(c_end - cin[u])
        kk_t.append((rows("kk", u) * jnp.exp(cin[u] - ld)).astype(BF16))
        r_t.append(rows("r", u) * jnp.exp(cin[u]))
        k_d.append((rows("kp", u) * e_dec).astype(BF16))
        b_d.append((rows("bb", u) * e_dec).astype(BF16))
        p_end.append(jnp.exp(c_end))
        lhs = jnp.concatenate([kk_t[u], r_t[u].astype(BF16)], axis=0)
        rhs = jnp.concatenate([_bd((rows("bb", u) * e_inv).astype(BF16)),
                               _bd((rows("kp", u) * e_inv).astype(BF16))], axis=0)
        g.append(_dot_nt(lhs, rhs))
    m_ab = [jnp.where(lower_strict, g[u][0:c, 0:2 * c], 0.0).astype(BF16) for u in range(nu)]
    a_rb = [jnp.where(lower_incl, g[u][c:2 * c, 0:2 * c], 0.0).astype(BF16) for u in range(nu)]
    m_ak = [jnp.where(lower_strict, g[u][0:c, 2 * c:4 * c], 0.0).astype(BF16) for u in range(nu)]
    a_rk = [jnp.where(lower_incl, g[u][c:2 * c, 2 * c:4 * c], 0.0).astype(BF16) for u in range(nu)]
    v_bd = [_bd(rows("v", u).astype(BF16)) for u in range(nu)]
    x = [eye_p - m_ab[u].astype(F32) for u in range(nu)]
    pw = [_dot(m_ab[u], _bd(m_ab[u])) for u in range(nu)]
    k = 2
    while 2 * k < c:
        both = [_dot(jnp.concatenate([x[u].astype(BF16), pw[u].astype(BF16)], axis=0), _bd(pw[u].astype(BF16)))
                for u in range(nu)]
        x = [x[u] + both[u][0:c] for u in range(nu)]
        pw = [both[u][c:2 * c] for u in range(nu)]
        k *= 2
    t_inv = [(x[u] + _dot(x[u], _bd(pw[u].astype(BF16)))).astype(BF16) for u in range(nu)]
    mv = [_dot(m_ak[u], v_bd[u]) for u in range(nu)]
    tw = [_dot(t_inv[u], jnp.concatenate([_bd(kk_t[u]), _bd(mv[u].astype(BF16))], axis=1)).astype(BF16)
          for u in range(nu)]
    bt = [_dot_tn(b_d[u], tw[u]) for u in range(nu)]
    kv = [_dot_tn(k_d[u], rows("v", u)) for u in range(nu)]
    ab = [_dot(a_rb[u], jnp.concatenate([_bd(tw[u][:, 0:LANES]), _bd(tw[u][:, LANES:2 * LANES])], axis=1))
          for u in range(nu)]
    av = [_dot(a_rk[u], v_bd[u]) for u in range(nu)]
    lhs_z, u_c, y_i = [], [], []
    for u in range(nu):
        a_c = jnp.where(eye_h, p_end[u], 0.0) - _diag_blocks(bt[u][:, 0:LANES])
        q_e = r_t[u] - ab[u][:, 0:LANES]
        lhs_z.append(jnp.concatenate([a_c, q_e], axis=0).astype(BF16))
        u_c.append(_diag_blocks(kv[u]) - _diag_blocks(bt[u][:, LANES:2 * LANES]))
        y_i.append(av[u] - ab[u][:, LANES:2 * LANES])
    z = {(s, p): z_ref[s, p] for s in range(sb) for p in range(PAIRS)}
    per_level = sb * PAIRS
    for ch in range(nc):
        level = range(ch * per_level, (ch + 1) * per_level)
        both = {u: _dot(lhs_z[u], _bd(z[units[u][0], units[u][2]].astype(BF16))) for u in level}
        ys = {u: both[u][hd:hd + c] + y_i[u] for u in level}
        for u in level:
            z[units[u][0], units[u][2]] = both[u][0:hd] + u_c[u]
        means = {u: _seg_sum(ys[u], first) * (1.0 / hd) for u in level}
        ycs = {u: ys[u] - means[u] for u in level}
        variances = {u: _seg_sum(ycs[u] * ycs[u], first) * (1.0 / hd) for u in level}
        for u in level:
            s, _, p = units[u]
            ps = slice(p * LANES, (p + 1) * LANES)
            yn = ycs[u] * lax.rsqrt(variances[u] + GN_EPS)
            out = (yn * vec[8:9, ps] + vec[9:10, ps] + rows("bonus", u)) * rows("gate", u)
            y_ref[s, ch * c:(ch + 1) * c, ps] = out.astype(y_ref.dtype)
    for s in range(sb):
        for p in range(PAIRS):
            z_ref[s, p] = z[(s, p)]

    @pl.when(b == nb - 1)
    def _():
        for s in range(sb):
            for p in range(PAIRS):
                sout_ref[s, 2 * p] = z[(s, p)][:, 0:hd].T
                sout_ref[s, 2 * p + 1] = z[(s, p)][:, hd:2 * hd].T


def _rwkv(p3, shiftp, s0, vec, mu_l, wl1, wl2, layer, *, sb, tb, c):
    s, t, _ = p3.shape
    lb = _COL_LORA // _LORA_W
    tpb = tb // BF16_ROWS
    nb = t // tb

    def prev_tile(b):
        return jnp.maximum(b * tpb - 1, 0)

    in_specs = [
        pl.BlockSpec((sb, tb, _RKV_W), lambda i, b: (i, b, 0)),
        pl.BlockSpec((sb, tb, _LORA_W), lambda i, b: (i, b, lb)),
        pl.BlockSpec((sb, BF16_ROWS, _RKV_W), lambda i, b: (i, prev_tile(b), 0)),
        pl.BlockSpec((sb, BF16_ROWS, _LORA_W), lambda i, b: (i, prev_tile(b), lb)),
        pl.BlockSpec((sb, 1, _RKV_W), lambda i, b: (i, 0, 0)),
        pl.BlockSpec((sb, 1, _LORA_W), lambda i, b: (i, 0, _RKV_W // _LORA_W)),
        pl.BlockSpec((None, _VEC_ROWS, A_WIDTH), lambda i, b: (layer, 0, 0)),
        pl.BlockSpec((None, 1, _LORA_W), lambda i, b: (layer, 0, 0)),
        pl.BlockSpec((None, LANES, 2 * A_WIDTH), lambda i, b: (layer, 0, 0)),
        pl.BlockSpec((None, _LORA_GW, A_WIDTH), lambda i, b: (layer, 0, 0)),
        pl.BlockSpec((sb, A_HEADS, HEAD_DIM, HEAD_DIM), lambda i, b: (i, 0, 0, 0)),
    ]
    out_specs = [
        pl.BlockSpec((sb, tb, A_WIDTH), lambda i, b: (i, b, 0)),
        pl.BlockSpec((sb, A_HEADS, HEAD_DIM, HEAD_DIM), lambda i, b: (i, 0, 0, 0)),
    ]
    return pl.pallas_call(
        functools.partial(_rwkv_kernel, sb=sb, tb=tb, c=c, nb=nb),
        grid=(s // sb, nb),
        in_specs=in_specs,
        out_specs=out_specs,
        out_shape=[jax.ShapeDtypeStruct((s, t, A_WIDTH), BF16),
                   jax.ShapeDtypeStruct((s, A_HEADS, HEAD_DIM, HEAD_DIM), F32)],
        scratch_shapes=[pltpu.VMEM((sb, PAIRS, HEAD_DIM, LANES), F32)],
        compiler_params=_cparams(2),
        name="rwkv",
    )(p3, p3, p3, p3, shiftp, shiftp, vec, mu_l, wl1, wl2, s0)


def _norm_rope(slabs):
    lane = lax.broadcasted_iota(jnp.int32, (1, LANES), 1)
    first = lane < HEAD_DIM
    low_half = (lane % HEAD_DIM) < (HEAD_DIM // 2)
    tiles = [(n, x[:, j * LANES:(j + 1) * LANES]) for n, (x, _, _, _) in enumerate(slabs)
             for j in range(x.shape[1] // LANES)]
    sq = [xs * xs for _, xs in tiles]
    s0 = [jnp.sum(jnp.where(first, s, 0.0), axis=1, keepdims=True) for s in sq]
    s1 = [jnp.sum(jnp.where(first, 0.0, s), axis=1, keepdims=True) for s in sq]
    xn = [xs * lax.rsqrt(jnp.where(first, a, b) * (1.0 / HEAD_DIM) + RMS_EPS) * slabs[n][1]
          for (n, xs), a, b in zip(tiles, s0, s1)]
    up = [pltpu.roll(v, LANES - HEAD_DIM // 2, 1) for v in xn]
    down = [pltpu.roll(v, HEAD_DIM // 2, 1) for v in xn]
    out = [v * slabs[n][2] + jnp.where(low_half, u, w) * slabs[n][3]
           for (n, _), v, u, w in zip(tiles, xn, up, down)]
    res, o = [], 0
    for x, _, _, _ in slabs:
        k = x.shape[1] // LANES
        res.append(jnp.concatenate(out[o:o + k], axis=1) if k > 1 else out[o])
        o += k
    return res


def _sink_softmax_all(scores, sinks):
    ms = [jnp.maximum(jnp.max(s, axis=-1, keepdims=True), k) for s, k in zip(scores, sinks)]
    ps = [jnp.exp(s - m) for s, m in zip(scores, ms)]
    ds = [jnp.sum(p, axis=-1, keepdims=True) + jnp.exp(k - m) for p, k, m in zip(ps, sinks, ms)]
    return [(p / d).astype(BF16) for p, d in zip(ps, ds)]


def _sink_column(sink_ref, heads, rows):
    return jnp.concatenate([jnp.full((rows, 1), sink_ref[h], F32) for h in heads], axis=0)


_QK_SCALE = HEAD_DIM ** -0.5


def _band_kernel(sink_ref, q_ref, kv_ref, kvp_ref, cq_ref, sq_ref, cp_ref, sp_ref, gq_ref, gk_ref,
                 y_ref, ko_ref, *, tq):
    i = pl.program_id(0)
    hd = HEAD_DIM
    band = WINDOW + CHUNK
    kv = kv_ref[...]
    kvp = kvp_ref[...]
    q, k_cur, k_prev = _norm_rope([
        (q_ref[...].astype(F32), gq_ref[...], cq_ref[...], sq_ref[...]),
        (kv[:, 0:B_KV].astype(F32), gk_ref[...], cq_ref[...], sq_ref[...]),
        (kvp[:, 0:B_KV].astype(F32), gk_ref[...], cp_ref[...], sp_ref[...])])
    q = (q * _QK_SCALE).astype(BF16)
    ko_ref[...] = k_cur
    k_all = jnp.concatenate([k_prev.astype(BF16), k_cur.astype(BF16)], axis=0)
    v_all = jnp.concatenate([kvp[:, B_KV:2 * B_KV], kv[:, B_KV:2 * B_KV]], axis=0)
    k_g = [k_all[:, g * hd:(g + 1) * hd] for g in range(B_KV_HEADS)]
    v_g = [v_all[:, g * hd:(g + 1) * hd] for g in range(B_KV_HEADS)]
    col = lax.broadcasted_iota(jnp.int32, (1, band), 1)
    sink_g = [_sink_column(sink_ref, range(g * B_GROUP, (g + 1) * B_GROUP), CHUNK) for g in range(B_KV_HEADS)]

    jobs = [(k0, g) for k0 in range(0, tq, CHUNK) for g in range(B_KV_HEADS)]
    q_rows = [jnp.concatenate([q[k0:k0 + CHUNK, (g * B_GROUP + j) * hd:(g * B_GROUP + j + 1) * hd]
                               for j in range(B_GROUP)], axis=0) for k0, g in jobs]
    scores = [_dot_nt(qr, k_g[g][k0:k0 + band]) for (k0, g), qr in zip(jobs, q_rows)]
    scores = [jnp.where((col >= WINDOW - k0) | (i > 0), s, -1e30) for (k0, g), s in zip(jobs, scores)]
    probs = _sink_softmax_all(scores, [sink_g[g] for k0, g in jobs])
    outs = {job: _dot(p, v_g[job[1]][job[0]:job[0] + band]) for job, p in zip(jobs, probs)}
    for k0 in range(0, tq, CHUNK):
        heads = [outs[(k0, g)][j * CHUNK:(j + 1) * CHUNK] for g in range(B_KV_HEADS) for j in range(B_GROUP)]
        y_ref[k0:k0 + CHUNK, :] = jnp.concatenate(heads, axis=1).astype(y_ref.dtype)


def _attn_band(p2, cos, sin_signed, gq, gk, sinks, layer, *, tq):
    t = p2.shape[0]
    qb, kvb = _COL_Q // B_Q, _COL_KV // _KV_W
    wpb = tq // WINDOW

    def prev_idx(i):
        return jnp.maximum(i * wpb - 1, 0)

    return pl.pallas_call(
        functools.partial(_band_kernel, tq=tq),
        grid=(t // tq,),
        in_specs=[pl.BlockSpec(memory_space=pltpu.SMEM),
                  pl.BlockSpec((tq, B_Q), lambda i: (i, qb)),
                  pl.BlockSpec((tq, _KV_W), lambda i: (i, kvb)),
                  pl.BlockSpec((WINDOW, _KV_W), lambda i: (prev_idx(i), kvb)),
                  pl.BlockSpec((tq, LANES), lambda i: (i, 0)),
                  pl.BlockSpec((tq, LANES), lambda i: (i, 0)),
                  pl.BlockSpec((WINDOW, LANES), lambda i: (prev_idx(i), 0)),
                  pl.BlockSpec((WINDOW, LANES), lambda i: (prev_idx(i), 0)),
                  pl.BlockSpec((None, 1, LANES), lambda i: (layer, 0, 0)),
                  pl.BlockSpec((None, 1, LANES), lambda i: (layer, 0, 0))],
        out_specs=[pl.BlockSpec((tq, B_Q), lambda i: (i, 0)),
                   pl.BlockSpec((tq, B_KV), lambda i: (i, 0))],
        out_shape=[jax.ShapeDtypeStruct((t, B_Q), BF16),
                   jax.ShapeDtypeStruct((t, B_KV), F32)],
        compiler_params=_cparams(1),
        name="attn_band",
    )(sinks[layer], p2, p2, p2, cos, sin_signed, cos, sin_signed, gq, gk)


def _cached_kernel(sink_ref, q_ref, kv_ref, ck_ref, cv_ref, c_ref, s_ref, gq_ref, gk_ref,
                   y_ref, ko_ref, *, sb):
    hd = HEAD_DIM
    tn = q_ref.shape[1]
    jobs = [(s, g) for s in range(sb) for g in range(B_KV_HEADS)]
    q_rows, k_all, v_all = {}, {}, {}
    cos = jnp.concatenate([c_ref[...]] * sb, axis=0)
    sin = jnp.concatenate([s_ref[...]] * sb, axis=0)
    q_cat = jnp.concatenate([q_ref[s] for s in range(sb)], axis=0).astype(F32)
    kv_cat = jnp.concatenate([kv_ref[s] for s in range(sb)], axis=0)
    q_cat, k_cat_new = _norm_rope([(q_cat, gq_ref[...], cos, sin),
                                   (kv_cat[:, 0:B_KV].astype(F32), gk_ref[...], cos, sin)])
    q_cat = (q_cat * _QK_SCALE).astype(BF16)
    for s in range(sb):
        q = q_cat[s * tn:(s + 1) * tn]
        k_new = k_cat_new[s * tn:(s + 1) * tn]
        ko_ref[s] = k_new
        k_cat = jnp.concatenate([ck_ref[s].astype(BF16), k_new.astype(BF16)], axis=0)
        v_cat = jnp.concatenate([cv_ref[s].astype(BF16), kv_cat[s * tn:(s + 1) * tn, B_KV:2 * B_KV]], axis=0)
        for g in range(B_KV_HEADS):
            q_rows[(s, g)] = jnp.concatenate(
                [q[:, (g * B_GROUP + j) * hd:(g * B_GROUP + j + 1) * hd] for j in range(B_GROUP)], axis=0)
            k_all[(s, g)] = k_cat[:, g * hd:(g + 1) * hd]
            v_all[(s, g)] = v_cat[:, g * hd:(g + 1) * hd]
    scores = [_dot_nt(q_rows[j], k_all[j]) for j in jobs]
    sink_g = [_sink_column(sink_ref, range(g * B_GROUP, (g + 1) * B_GROUP), tn) for g in range(B_KV_HEADS)]
    probs = _sink_softmax_all(scores, [sink_g[g] for s, g in jobs])
    outs = {j: _dot(p, v_all[j]) for j, p in zip(jobs, probs)}
    for s in range(sb):
        heads = [outs[(s, g)][j * tn:(j + 1) * tn] for g in range(B_KV_HEADS) for j in range(B_GROUP)]
        y_ref[s] = jnp.concatenate(heads, axis=1).astype(y_ref.dtype)


def _attn_cached(p3, ck, cv, cos, sin_signed, gq, gk, sinks, layer, *, sb):
    s, t, _ = p3.shape
    w = ck.shape[2]
    qb, kvb = _COL_Q // B_Q, _COL_KV // _KV_W
    return pl.pallas_call(
        functools.partial(_cached_kernel, sb=sb),
        grid=(s // sb,),
        in_specs=[pl.BlockSpec(memory_space=pltpu.SMEM),
                  pl.BlockSpec((sb, t, B_Q), lambda i: (i, 0, qb)),
                  pl.BlockSpec((sb, t, _KV_W), lambda i: (i, 0, kvb)),
                  pl.BlockSpec((None, sb, w, B_KV), lambda i: (layer, i, 0, 0)),
                  pl.BlockSpec((None, sb, w, B_KV), lambda i: (layer, i, 0, 0)),
                  pl.BlockSpec((t, LANES), lambda i: (0, 0)),
                  pl.BlockSpec((t, LANES), lambda i: (0, 0)),
                  pl.BlockSpec((None, 1, LANES), lambda i: (layer, 0, 0)),
                  pl.BlockSpec((None, 1, LANES), lambda i: (layer, 0, 0))],
        out_specs=[pl.BlockSpec((sb, t, B_Q), lambda i: (i, 0, 0)),
                   pl.BlockSpec((sb, t, B_KV), lambda i: (i, 0, 0))],
        out_shape=[jax.ShapeDtypeStruct((s, t, B_Q), BF16),
                   jax.ShapeDtypeStruct((s, t, B_KV), F32)],
        compiler_params=_cparams(1),
        name="attn_cached",
    )(sinks[layer], p3, p3, ck, cv, cos, sin_signed, gq, gk)


def _outproj_kernel(x_ref, ya_ref, yb_ref, ga_ref, gb_ref, wa_ref, wb_ref, wo_ref, o_ref):
    ua = jnp.dot(ya_ref[...], wa_ref[...], preferred_element_type=F32)
    ub = jnp.dot(yb_ref[...], wb_ref[...], preferred_element_type=F32)
    merged = (jax.nn.sigmoid(ga_ref[...].astype(F32)) * ua + jax.nn.sigmoid(gb_ref[...].astype(F32)) * ub)
    o_ref[...] = x_ref[...] + jnp.dot(merged.astype(BF16), wo_ref[...], preferred_element_type=F32)


def _outproj(x, ya, yb, p2, wa, wb, wo, layer, *, tm):
    t, d = x.shape
    gab, gbb = _COL_GA // d, _COL_GB // d
    return pl.pallas_call(
        _outproj_kernel,
        grid=(t // tm,),
        in_specs=[pl.BlockSpec((tm, d), lambda i: (i, 0)),
                  pl.BlockSpec((tm, A_WIDTH), lambda i: (i, 0)),
                  pl.BlockSpec((tm, B_Q), lambda i: (i, 0)),
                  pl.BlockSpec((tm, d), lambda i: (i, gab)),
                  pl.BlockSpec((tm, d), lambda i: (i, gbb)),
                  _single_buffered((None, A_WIDTH, d), lambda i: (layer, 0, 0)),
                  _single_buffered((None, B_Q, d), lambda i: (layer, 0, 0)),
                  _single_buffered((None, d, d), lambda i: (layer, 0, 0))],
        out_specs=pl.BlockSpec((tm, d), lambda i: (i, 0)),
        out_shape=jax.ShapeDtypeStruct((t, d), F32),
        compiler_params=_cparams(1),
        name="outproj",
    )(x, ya, yb, p2, p2, wa, wb, wo)


def _a_pieces(a):
    o = 0
    out = []
    for w in (A_WIDTH, DECAY_LORA, A_WIDTH, A_WIDTH, ICLR_LORA, GATE_LORA):
        out.append(a[..., o:o + w])
        o += w
    return out


def _pad_last(w, n):
    return jnp.pad(w, [(0, 0)] * (w.ndim - 1) + [(0, n - w.shape[-1])])


def _regroup_a(a):
    r, w_lo, k, v, a_lo, g_lo = _a_pieces(a)
    lora = _pad_last(jnp.concatenate([w_lo, a_lo, g_lo], axis=-1), _LORA_W)
    return jnp.concatenate([r, k, v], axis=-1), lora


def _regroup_w_in(w):
    o = A_COLS
    wq = w[..., o:o + B_Q]
    wk = w[..., o + B_Q:o + B_Q + B_KV]
    wv = w[..., o + B_Q + B_KV:o + B_Q + 2 * B_KV]
    o += B_Q + 2 * B_KV
    wga = w[..., o:o + D_MODEL]
    wgb = w[..., o + D_MODEL:o + 2 * D_MODEL]
    rkv, lora = _regroup_a(w[..., :A_COLS])
    return jnp.concatenate([rkv, wq, wga, wgb, lora, wk, wv], axis=-1).astype(BF16)


def _shift_row_to_a(p_last):
    p_last = p_last.astype(F32)
    r = p_last[..., _COL_R:_COL_R + A_WIDTH]
    k = p_last[..., _COL_K:_COL_K + A_WIDTH]
    v = p_last[..., _COL_V:_COL_V + A_WIDTH]
    o = _COL_LORA
    w_lo = p_last[..., o:o + DECAY_LORA]
    a_lo = p_last[..., o + DECAY_LORA:o + DECAY_LORA + ICLR_LORA]
    g_lo = p_last[..., o + DECAY_LORA + ICLR_LORA:o + DECAY_LORA + ICLR_LORA + GATE_LORA]
    return jnp.concatenate([r, w_lo, k, v, a_lo, g_lo], axis=-1)


def _lora_weights(decay_w2, iclr_a2, gate_g2):
    top = jnp.concatenate([decay_w2, jnp.zeros_like(decay_w2)], axis=-1)
    bot = jnp.concatenate([jnp.zeros_like(iclr_a2), iclr_a2], axis=-1)
    wl1 = jnp.concatenate([top, bot], axis=-2)
    wl2 = jnp.pad(gate_g2, ((0, 0), (0, _LORA_GW - GATE_LORA), (0, 0)))
    return wl1.astype(BF16), wl2.astype(BF16)


def _rope_tables(pos):
    half = HEAD_DIM // 2
    inv = ROPE_THETA ** (-jnp.arange(half, dtype=F32) / half)
    ang = pos.astype(F32)[:, None] * inv[None, :]
    cos = jnp.cos(ang)
    sin = jnp.sin(ang)
    cos_t = jnp.tile(cos, (1, LANES // half))
    sin_t = jnp.tile(jnp.concatenate([-sin, sin], axis=1), (1, LANES // HEAD_DIM))
    return cos_t, sin_t


_FF_TILE = 512
_FF_CAST_TILE = 256


def kernel(x_prompt, x_sample, cache_k, cache_v, state_wkv, state_shift, norm_ff1, ff1_gate, ff1_up, ff1_down,
           norm_mix, w_in, shift_mu, decay_w0, decay_w2, iclr_a0, iclr_a2, gate_g2, k_k, k_a, r_k, gn_gain,
           gn_bias, q_norm, k_norm, sinks, w_up_a, w_up_b, w_o, norm_ff2, ff2_gate, ff2_up, ff2_down):
    depth = norm_ff1.shape[0]
    bp, tp, d = x_prompt.shape
    bs, ts, _ = x_sample.shape
    assert bp == 1 and d == D_MODEL

    cast_tail = ff1_gate.shape[2] % _FF_CAST_TILE
    assert cast_tail % LANES == 0 and cast_tail > 0
    g_ff1 = norm_ff1[:, None, :]
    g_ff2 = norm_ff2[:, None, :]
    g_mix = norm_mix[:, None, :]
    w_in_b = _regroup_w_in(w_in)
    mu_rkv, mu_l = _regroup_a(shift_mu)
    mu_l = mu_l[:, None, :]
    rows = [mu_rkv[:, 0:A_WIDTH], mu_rkv[:, A_WIDTH:2 * A_WIDTH], mu_rkv[:, 2 * A_WIDTH:], decay_w0, iclr_a0,
            k_k, k_a, r_k.reshape(depth, A_WIDTH), gn_gain, gn_bias]
    vec = jnp.stack(rows + [jnp.zeros((depth, A_WIDTH), F32)] * (_VEC_ROWS - len(rows)), axis=1)
    wl1, wl2 = _lora_weights(decay_w2, iclr_a2, gate_g2)
    gq = jnp.tile(q_norm, (1, LANES // HEAD_DIM))[:, None, :]
    gk = jnp.tile(k_norm, (1, LANES // HEAD_DIM))[:, None, :]
    wa = w_up_a.astype(BF16)
    wb = w_up_b.astype(BF16)
    wo = w_o.astype(BF16)
    s_rkv, s_lora = _regroup_a(state_shift)
    shift_s = jnp.concatenate([s_rkv, s_lora], axis=-1)[:, :, None, :]
    zero_shift = jnp.zeros((bp, 1, _RKV_W + _LORA_W), F32)
    zero_state = jnp.zeros((bp, A_HEADS, HEAD_DIM, HEAD_DIM), F32)
    ck = cache_k.reshape(depth, bs, -1, B_KV)
    cv = cache_v.reshape(depth, bs, -1, B_KV)
    cos_p, sin_p = _rope_tables(jnp.arange(tp))
    cos_s, sin_s = _rope_tables(PAST_LEN + jnp.arange(ts))

    xp = x_prompt.reshape(tp, d)
    xs = x_sample.reshape(bs * ts, d)
    outs = {k: [] for k in ("p_wkv", "p_shift", "p_k", "p_v", "s_wkv", "s_shift", "s_k", "s_v")}
    for l in range(depth):
        xp, xs = _ffn_pair(xp, xs, g_ff1, ff1_gate, ff1_up, ff1_down, l,
                           tm=512, tf=_FF_TILE, tf_cast=_FF_CAST_TILE)

        pp = _inproj(xp, g_mix, w_in_b, l, tm=1024, tn=1024)
        ps = _inproj(xs, g_mix, w_in_b, l, tm=512, tn=1024)
        pp3 = pp.reshape(bp, tp, NP_COLS)
        ps3 = ps.reshape(bs, ts, NP_COLS)

        ya_p, wkv_p = _rwkv(pp3, zero_shift, zero_state, vec, mu_l, wl1, wl2, l, sb=1, tb=4 * CHUNK, c=CHUNK)
        ya_s, wkv_s = _rwkv(ps3, shift_s[l], state_wkv[l], vec, mu_l, wl1, wl2, l, sb=2, tb=ts, c=ts)

        yb_p, kr_p = _attn_band(pp, cos_p, sin_p, gq, gk, sinks, l, tq=2 * WINDOW)
        yb_s, kr_s = _attn_cached(ps3, ck, cv, cos_s, sin_s, gq, gk, sinks, l, sb=8)

        xp = _outproj(xp, ya_p.reshape(tp, A_WIDTH), yb_p, pp, wa, wb, wo, l, tm=256)
        xs = _outproj(xs, ya_s.reshape(bs * ts, A_WIDTH), yb_s.reshape(bs * ts, B_Q), ps, wa, wb, wo, l, tm=256)

        xp, xs = _ffn_pair(xp, xs, g_ff2, ff2_gate, ff2_up, ff2_down, l,
                           tm=512, tf=_FF_TILE, tf_cast=_FF_CAST_TILE)

        vcol = _COL_KV + B_KV
        outs["p_wkv"].append(wkv_p)
        outs["p_shift"].append(_shift_row_to_a(pp3[:, -1, :]))
        outs["p_k"].append(kr_p[-WINDOW:].reshape(bp, WINDOW, B_KV_HEADS, HEAD_DIM))
        outs["p_v"].append(pp3[:, -WINDOW:, vcol:vcol + B_KV].astype(F32).reshape(bp, WINDOW, B_KV_HEADS, HEAD_DIM))
        outs["s_wkv"].append(wkv_s)
        outs["s_shift"].append(_shift_row_to_a(ps3[:, -1, :]))
        outs["s_k"].append(kr_s.reshape(bs, ts, B_KV_HEADS, HEAD_DIM))
        outs["s_v"].append(ps3[:, :, vcol:vcol + B_KV].astype(F32).reshape(bs, ts, B_KV_HEADS, HEAD_DIM))

    return (xp.reshape(bp, tp, d), xs.reshape(bs, ts, d),
            jnp.stack(outs["p_wkv"]), jnp.stack(outs["p_shift"]), jnp.stack(outs["p_k"]), jnp.stack(outs["p_v"]),
            jnp.stack(outs["s_wkv"]), jnp.stack(outs["s_shift"]), jnp.stack(outs["s_k"]), jnp.stack(outs["s_v"]))
```

```python
import functools

import jax
import jax.numpy as jnp
from jax import lax
from jax.experimental import pallas as pl
from jax.experimental.pallas import tpu as pltpu

F32 = jnp.float32
BF16 = jnp.bfloat16

HEAD_DIM = 64
A_WIDTH = 1024
A_HEADS = A_WIDTH // HEAD_DIM
DECAY_LORA = 64
ICLR_LORA = 64
GATE_LORA = 160
GN_EPS = 64e-5
RMS_EPS = 1e-6
B_HEADS = 16
B_KV_HEADS = 4
B_GROUP = B_HEADS // B_KV_HEADS
B_Q = B_HEADS * HEAD_DIM
B_KV = B_KV_HEADS * HEAD_DIM
CHUNK = 64
WINDOW = 128
ROPE_THETA = 10000.0
PAST_LEN = 1024
D_MODEL = 2048
A_COLS = 3 * A_WIDTH + DECAY_LORA + ICLR_LORA + GATE_LORA

LANES = 128
BF16_ROWS = 16
VMEM_LIMIT_BYTES = 56 * 1024 * 1024

_COL_R = 0
_COL_K = _COL_R + A_WIDTH
_COL_V = _COL_K + A_WIDTH
_RKV_W = 3 * A_WIDTH
_COL_Q = _COL_V + A_WIDTH
_COL_GA = _COL_Q + B_Q
_COL_GB = _COL_GA + D_MODEL
_COL_LORA = _COL_GB + D_MODEL
_LORA_W = 512
_COL_KV = _COL_LORA + _LORA_W
_KV_W = 2 * B_KV
NP_COLS = _COL_KV + _KV_W
_LORA_G0 = LANES
_LORA_GW = 2 * LANES
PAIRS = A_HEADS // 2
_VEC_ROWS = 16


def _cparams(n_axes):
    return pltpu.CompilerParams(dimension_semantics=("arbitrary",) * n_axes,
                                vmem_limit_bytes=VMEM_LIMIT_BYTES)


def _dot(a, b):
    return jnp.dot(a.astype(BF16), b.astype(BF16), preferred_element_type=F32)


def _dot_nt(a, b):
    return lax.dot_general(a.astype(BF16), b.astype(BF16), (((1,), (1,)), ((), ())),
                           preferred_element_type=F32)


def _dot_tn(a, b):
    return lax.dot_general(a.astype(BF16), b.astype(BF16), (((0,), (0,)), ((), ())),
                           preferred_element_type=F32)


def _split3(x):
    hi = x.astype(BF16)
    r1 = x - hi.astype(F32)
    mid = r1.astype(BF16)
    lo = (r1 - mid.astype(F32)).astype(BF16)
    return hi, mid, lo


def _rms_rows(x, gain):
    ms = jnp.mean(x * x, axis=-1, keepdims=True)
    return x * lax.rsqrt(ms + RMS_EPS) * gain


def _single_buffered(shape, index_map):
    return pl.BlockSpec(shape, index_map, pipeline_mode=pl.Buffered(1))


def _ffn_init(x_ref, g_ref, o_ref, h_ref):
    x = x_ref[...]
    h_ref[...] = _rms_rows(x, g_ref[...]).astype(BF16)
    o_ref[...] = x


def _swiglu_accumulate(h_ref, o_ref, wg, wu, wd):
    h = h_ref[...]
    gate = jnp.dot(h, wg, preferred_element_type=F32)
    up = jnp.dot(h, wu, preferred_element_type=F32)
    act = (0.5 * gate * jax.nn.sigmoid(gate) * up).astype(BF16)
    o_ref[...] += jnp.dot(act, wd, preferred_element_type=F32)


def _ffn_kernel(x_ref, g_ref, wg_ref, wu_ref, wd_ref, wgt_ref, wut_ref, wdt_ref, o_ref, h_ref, *, n_full):
    f = pl.program_id(1)

    @pl.when(f == 0)
    def _():
        _ffn_init(x_ref, g_ref, o_ref, h_ref)

    @pl.when(f < n_full)
    def _():
        _swiglu_accumulate(h_ref, o_ref, wg_ref[...], wu_ref[...], wd_ref[...])

    @pl.when(f == n_full)
    def _():
        _swiglu_accumulate(h_ref, o_ref, wgt_ref[...], wut_ref[...], wdt_ref[...])


def _ffn(x, gain, w, layer, *, tm, tf):
    wg, wu, wd, wgt, wut, wdt = w
    t, d = x.shape
    n_full = wg.shape[1] // tf
    tail = wgt.shape[1]
    last = n_full - 1
    return pl.pallas_call(
        functools.partial(_ffn_kernel, n_full=n_full),
        grid=(t // tm, n_full + 1),
        in_specs=[pl.BlockSpec((tm, d), lambda i, f: (i, 0)),
                  pl.BlockSpec((None, 1, d), lambda i, f: (layer, 0, 0)),
                  pl.BlockSpec((d, tf), lambda i, f: (0, jnp.minimum(f, last))),
                  pl.BlockSpec((d, tf), lambda i, f: (0, jnp.minimum(f, last))),
                  pl.BlockSpec((tf, d), lambda i, f: (jnp.minimum(f, last), 0)),
                  _single_buffered((d, tail), lambda i, f: (0, 0)),
                  _single_buffered((d, tail), lambda i, f: (0, 0)),
                  _single_buffered((tail, d), lambda i, f: (0, 0))],
        out_specs=pl.BlockSpec((tm, d), lambda i, f: (i, 0)),
        out_shape=jax.ShapeDtypeStruct((t, d), F32),
        scratch_shapes=[pltpu.VMEM((tm, d), BF16)],
        compiler_params=_cparams(2),
        name="ffn",
    )(x, gain, wg, wu, wd, wgt, wut, wdt)


def _ffn_cast_kernel(x_ref, g_ref, wg_ref, wu_ref, wd_ref, wgt_ref, wut_ref, wdt_ref,
                     o_ref, wg16_ref, wu16_ref, wd16_ref, wgt16_ref, wut16_ref, wdt16_ref, h_ref, *, n_full):
    f = pl.program_id(0)

    @pl.when(f == 0)
    def _():
        _ffn_init(x_ref, g_ref, o_ref, h_ref)

    def step(src, dst):
        w = [r[...].astype(BF16) for r in src]
        for r, v in zip(dst, w):
            r[...] = v
        _swiglu_accumulate(h_ref, o_ref, *w)

    @pl.when(f < n_full)
    def _():
        step((wg_ref, wu_ref, wd_ref), (wg16_ref, wu16_ref, wd16_ref))

    @pl.when(f == n_full)
    def _():
        step((wgt_ref, wut_ref, wdt_ref), (wgt16_ref, wut16_ref, wdt16_ref))


def _ffn_cast(x, gain, gate, up, down, layer, *, tf, tail):
    t, d = x.shape
    d_ff = gate.shape[2]
    n_full = (d_ff - tail) // tf
    assert n_full * tf + tail == d_ff and d_ff % tail == 0
    last = n_full - 1
    tail_block = d_ff // tail - 1
    wide = pl.BlockSpec((None, d, tf), lambda f: (layer, 0, jnp.minimum(f, last)))
    tall = pl.BlockSpec((None, tf, d), lambda f: (layer, jnp.minimum(f, last), 0))
    wide16 = pl.BlockSpec((d, tf), lambda f: (0, jnp.minimum(f, last)))
    tall16 = pl.BlockSpec((tf, d), lambda f: (jnp.minimum(f, last), 0))
    return pl.pallas_call(
        functools.partial(_ffn_cast_kernel, n_full=n_full),
        grid=(n_full + 1,),
        in_specs=[_single_buffered((t, d), lambda f: (0, 0)),
                  pl.BlockSpec((None, 1, d), lambda f: (layer, 0, 0)),
                  wide, wide, tall,
                  _single_buffered((None, d, tail), lambda f: (layer, 0, tail_block)),
                  _single_buffered((None, d, tail), lambda f: (layer, 0, tail_block)),
                  _single_buffered((None, tail, d), lambda f: (layer, tail_block, 0))],
        out_specs=[pl.BlockSpec((t, d), lambda f: (0, 0)),
                   wide16, wide16, tall16,
                   pl.BlockSpec((d, tail), lambda f: (0, 0)),
                   pl.BlockSpec((d, tail), lambda f: (0, 0)),
                   pl.BlockSpec((tail, d), lambda f: (0, 0))],
        out_shape=[jax.ShapeDtypeStruct((t, d), F32),
                   jax.ShapeDtypeStruct((d, n_full * tf), BF16),
                   jax.ShapeDtypeStruct((d, n_full * tf), BF16),
                   jax.ShapeDtypeStruct((n_full * tf, d), BF16),
                   jax.ShapeDtypeStruct((d, tail), BF16),
                   jax.ShapeDtypeStruct((d, tail), BF16),
                   jax.ShapeDtypeStruct((tail, d), BF16)],
        scratch_shapes=[pltpu.VMEM((t, d), BF16)],
        compiler_params=_cparams(1),
        name="ffn_cast",
    )(x, gain, gate, up, down, gate, up, down)


def _ffn_pair(xp, xs, gain, gate, up, down, layer, *, tm, tf, tf_cast):
    tail = gate.shape[2] % tf_cast
    xs, wg, wu, wd, wgt, wut, wdt = _ffn_cast(xs, gain, gate, up, down, layer, tf=tf_cast, tail=tail)
    done = (wg.shape[1] // tf) * tf
    w = (wg, wu, wd,
         jnp.concatenate([wg[:, done:], wgt], axis=1), jnp.concatenate([wu[:, done:], wut], axis=1),
         jnp.concatenate([wd[done:, :], wdt], axis=0))
    return _ffn(xp, gain, w, layer, tm=tm, tf=tf), xs


def _inproj_kernel(x_ref, g_ref, w_ref, o_ref, h_ref):
    @pl.when(pl.program_id(1) == 0)
    def _():
        h_ref[...] = _rms_rows(x_ref[...], g_ref[...]).astype(BF16)

    o_ref[...] = jnp.dot(h_ref[...], w_ref[...], preferred_element_type=F32).astype(o_ref.dtype)


def _inproj(x, gain, w, layer, *, tm, tn):
    t, d = x.shape
    n = w.shape[2]
    return pl.pallas_call(
        _inproj_kernel,
        grid=(t // tm, n // tn),
        in_specs=[pl.BlockSpec((tm, d), lambda i, j: (i, 0)),
                  pl.BlockSpec((None, 1, d), lambda i, j: (layer, 0, 0)),
                  pl.BlockSpec((None, d, tn), lambda i, j: (layer, 0, j))],
        out_specs=pl.BlockSpec((tm, tn), lambda i, j: (i, j)),
        out_shape=jax.ShapeDtypeStruct((t, n), BF16),
        scratch_shapes=[pltpu.VMEM((tm, d), BF16)],
        compiler_params=_cparams(2),
        name="inproj",
    )(x, gain, w)


def _seg_sum(x, first):
    s0 = jnp.sum(jnp.where(first, x, 0.0), axis=1, keepdims=True)
    s1 = jnp.sum(jnp.where(first, 0.0, x), axis=1, keepdims=True)
    return jnp.where(first, s0, s1)


def _bd(x):
    w = x.shape[1]
    first = lax.broadcasted_iota(jnp.int32, (1, w), 1) < (w // 2)
    zero = jnp.zeros_like(x)
    return jnp.concatenate([jnp.where(first, x, zero), jnp.where(first, zero, x)], axis=0)


def _diag_blocks(m):
    n = m.shape[0] // 2
    first = lax.broadcasted_iota(jnp.int32, (1, 2 * n), 1) < n
    return jnp.where(first, m[0:n], m[n:2 * n])


def _rwkv_kernel(rkv_ref, l_ref, rkvp_ref, lp_ref, srkv_ref, sl_ref, vec_ref, mul_ref, wl1_ref, wl2_ref,
                 s0_ref, y_ref, sout_ref, z_ref, *, sb, tb, c, nb):
    b = pl.program_id(1)
    nc = tb // c
    hd = HEAD_DIM

    vec = vec_ref[...]
    mu_rkv = jnp.concatenate([vec[0:1], vec[1:2], vec[2:3]], axis=1)
    mu_l = mul_ref[...]

    lane = lax.broadcasted_iota(jnp.int32, (1, LANES), 1)
    first = lane < hd
    row_t = lax.broadcasted_iota(jnp.int32, (tb, 1), 0)
    ri = lax.broadcasted_iota(jnp.int32, (c, 2 * c), 0)
    ci = lax.broadcasted_iota(jnp.int32, (c, 2 * c), 1) % c
    lower_incl = ri >= ci
    lower_strict = ri > ci
    eye_p = jnp.where(ri == ci, 1.0, 0.0).astype(F32)
    rt_ = lax.broadcasted_iota(jnp.int32, (c, c), 0)
    ct_ = lax.broadcasted_iota(jnp.int32, (c, c), 1)
    tri = jnp.where(rt_ >= ct_, 1.0, 0.0).astype(BF16)
    rj = lax.broadcasted_iota(jnp.int32, (hd, LANES), 0)
    cj = lax.broadcasted_iota(jnp.int32, (hd, LANES), 1) % hd
    eye_h = rj == cj

    @pl.when(b == 0)
    def _():
        for s in range(sb):
            for p in range(PAIRS):
                z_ref[s, p] = jnp.concatenate([s0_ref[s, 2 * p].T, s0_ref[s, 2 * p + 1].T], axis=1)

    def shifted(ref, pref, sref, s, mu):
        raw = ref[s].astype(F32)
        before = jnp.where(b == 0, sref[s], pref[s][BF16_ROWS - 1:BF16_ROWS, :].astype(F32))
        prev = jnp.where(row_t == 0, before, pltpu.roll(raw, 1, 0))
        return raw + (prev - raw) * mu

    pre = {}
    for s in range(sb):
        x3 = shifted(rkv_ref, rkvp_ref, srkv_ref, s, mu_rkv)
        lx = shifted(l_ref, lp_ref, sl_ref, s, mu_l)
        l01 = lx[:, 0:LANES]
        z01 = jnp.where(first, jnp.tanh(l01), l01)
        wa = _dot(z01, wl1_ref[...])
        gate = _dot(jax.nn.sigmoid(lx[:, _LORA_G0:_LORA_G0 + _LORA_GW]), wl2_ref[...])
        for p in range(PAIRS):
            ps = slice(p * LANES, (p + 1) * LANES)
            w0, a0, k_k, k_a, r_k = vec[3:4, ps], vec[4:5, ps], vec[5:6, ps], vec[6:7, ps], vec[7:8, ps]
            rx = x3[:, _COL_R + p * LANES:_COL_R + (p + 1) * LANES]
            kx = x3[:, _COL_K + p * LANES:_COL_K + (p + 1) * LANES]
            vx = x3[:, _COL_V + p * LANES:_COL_V + (p + 1) * LANES]
            w_pre = wa[:, ps] + w0
            a_pre = wa[:, A_WIDTH + p * LANES:A_WIDTH + (p + 1) * LANES] + a0
            softplus = jnp.maximum(-w_pre, 0.0) + jnp.log1p(jnp.exp(-jnp.abs(w_pre)))
            logd = -jnp.exp(-softplus - 0.5)
            a = jax.nn.sigmoid(a_pre)
            kxk = kx * k_k
            kk = kxk / jnp.maximum(jnp.sqrt(_seg_sum(kxk * kxk, first)), 1e-12)
            kp = kx * (1.0 + (a - 1.0) * k_a)
            pre[(s, p)] = dict(r=rx, kk=kk, kp=kp, bb=kk * a, v=vx, logd=logd, gate=gate[:, ps],
                               bonus=_seg_sum(rx * kp * r_k, first) * vx)

    units = [(s, ch, p) for ch in range(nc) for s in range(sb) for p in range(PAIRS)]

    def rows(name, u):
        s, ch, p = units[u]
        return pre[(s, p)][name][ch * c:(ch + 1) * c]

    nu = len(units)
    cin = []
    for u in range(nu):
        hi, mid, lo = _split3(rows("logd", u))
        cs = jnp.dot(tri, jnp.concatenate([hi, mid, lo], axis=1), preferred_element_type=F32)
        cin.append(cs[:, 0:LANES] + cs[:, LANES:2 * LANES] + cs[:, 2 * LANES:3 * LANES])
    kk_t, r_t, k_d, b_d, p_end, g = [], [], [], [], [], []
    for u in range(nu):
        ld = rows("logd", u)
        c_end = cin[u][c - 1:c, :]
        e_inv = jnp.exp(-cin[u])
        e_dec = jnp.exp(c_end - cin[u])
        kk_t.append((rows("kk", u) * jnp.exp(cin[u] - ld)).astype(BF16))
        r_t.append(rows("r", u) * jnp.exp(cin[u]))
        k_d.append((rows("kp", u) * e_dec).astype(BF16))
        b_d.append((rows("bb", u) * e_dec).astype(BF16))
        p_end.append(jnp.exp(c_end))
        lhs = jnp.concatenate([kk_t[u], r_t[u].astype(BF16)], axis=0)
        rhs = jnp.concatenate([_bd((rows("bb", u) * e_inv).astype(BF16)),
                               _bd((rows("kp", u) * e_inv).astype(BF16))], axis=0)
        g.append(_dot_nt(lhs, rhs))
    m_ab = [jnp.where(lower_strict, g[u][0:c, 0:2 * c], 0.0).astype(BF16) for u in range(nu)]
    a_rb = [jnp.where(lower_incl, g[u][c:2 * c, 0:2 * c], 0.0).astype(BF16) for u in range(nu)]
    m_ak = [jnp.where(lower_strict, g[u][0:c, 2 * c:4 * c], 0.0).astype(BF16) for u in range(nu)]
    a_rk = [jnp.where(lower_incl, g[u][c:2 * c, 2 * c:4 * c], 0.0).astype(BF16) for u in range(nu)]
    v_bd = [_bd(rows("v", u).astype(BF16)) for u in range(nu)]
    x = [eye_p - m_ab[u].astype(F32) for u in range(nu)]
    pw = [_dot(m_ab[u], _bd(m_ab[u])) for u in range(nu)]
    k = 2
    while 2 * k < c:
        both = [_dot(jnp.concatenate([x[u].astype(BF16), pw[u].astype(BF16)], axis=0), _bd(pw[u].astype(BF16)))
                for u in range(nu)]
        x = [x[u] + both[u][0:c] for u in range(nu)]
        pw = [both[u][c:2 * c] for u in range(nu)]
        k *= 2
    t_inv = [(x[u] + _dot(x[u], _bd(pw[u].astype(BF16)))).astype(BF16) for u in range(nu)]
    mv = [_dot(m_ak[u], v_bd[u]) for u in range(nu)]
    tw = [_dot(t_inv[u], jnp.concatenate([_bd(kk_t[u]), _bd(mv[u].astype(BF16))], axis=1)).astype(BF16)
          for u in range(nu)]
    bt = [_dot_tn(b_d[u], tw[u]) for u in range(nu)]
    kv = [_dot_tn(k_d[u], rows("v", u)) for u in range(nu)]
    ab = [_dot(a_rb[u], jnp.concatenate([_bd(tw[u][:, 0:LANES]), _bd(tw[u][:, LANES:2 * LANES])], axis=1))
          for u in range(nu)]
    av = [_dot(a_rk[u], v_bd[u]) for u in range(nu)]
    lhs_z, u_c, y_i = [], [], []
    for u in range(nu):
        a_c = jnp.where(eye_h, p_end[u], 0.0) - _diag_blocks(bt[u][:, 0:LANES])
        q_e = r_t[u] - ab[u][:, 0:LANES]
        lhs_z.append(jnp.concatenate([a_c, q_e], axis=0).astype(BF16))
        u_c.append(_diag_blocks(kv[u]) - _diag_blocks(bt[u][:, LANES:2 * LANES]))
        y_i.append(av[u] - ab[u][:, LANES:2 * LANES])
    z = {(s, p): z_ref[s, p] for s in range(sb) for p in range(PAIRS)}
    per_level = sb * PAIRS
    for ch in range(nc):
        level = range(ch * per_level, (ch + 1) * per_level)
        both = {u: _dot(lhs_z[u], _bd(z[units[u][0], units[u][2]].astype(BF16))) for u in level}
        ys = {u: both[u][hd:hd + c] + y_i[u] for u in level}
        for u in level:
            z[units[u][0], units[u][2]] = both[u][0:hd] + u_c[u]
        means = {u: _seg_sum(ys[u], first) * (1.0 / hd) for u in level}
        ycs = {u: ys[u] - means[u] for u in level}
        variances = {u: _seg_sum(ycs[u] * ycs[u], first) * (1.0 / hd) for u in level}
        for u in level:
            s, _, p = units[u]
            ps = slice(p * LANES, (p + 1) * LANES)
            yn = ycs[u] * lax.rsqrt(variances[u] + GN_EPS)
            out = (yn * vec[8:9, ps] + vec[9:10, ps] + rows("bonus", u)) * rows("gate", u)
            y_ref[s, ch * c:(ch + 1) * c, ps] = out.astype(y_ref.dtype)
    for s in range(sb):
        for p in range(PAIRS):
            z_ref[s, p] = z[(s, p)]

    @pl.when(b == nb - 1)
    def _():
        for s in range(sb):
            for p in range(PAIRS):
                sout_ref[s, 2 * p] = z[(s, p)][:, 0:hd].T
                sout_ref[s, 2 * p + 1] = z[(s, p)][:, hd:2 * hd].T


def _rwkv(p3, shiftp, s0, vec, mu_l, wl1, wl2, layer, *, sb, tb, c):
    s, t, _ = p3.shape
    lb = _COL_LORA // _LORA_W
    tpb = tb // BF16_ROWS
    nb = t // tb

    def prev_tile(b):
        return jnp.maximum(b * tpb - 1, 0)

    in_specs = [
        pl.BlockSpec((sb, tb, _RKV_W), lambda i, b: (i, b, 0)),
        pl.BlockSpec((sb, tb, _LORA_W), lambda i, b: (i, b, lb)),
        pl.BlockSpec((sb, BF16_ROWS, _RKV_W), lambda i, b: (i, prev_tile(b), 0)),
        pl.BlockSpec((sb, BF16_ROWS, _LORA_W), lambda i, b: (i, prev_tile(b), lb)),
        pl.BlockSpec((sb, 1, _RKV_W), lambda i, b: (i, 0, 0)),
        pl.BlockSpec((sb, 1, _LORA_W), lambda i, b: (i, 0, _RKV_W // _LORA_W)),
        pl.BlockSpec((None, _VEC_ROWS, A_WIDTH), lambda i, b: (layer, 0, 0)),
        pl.BlockSpec((None, 1, _LORA_W), lambda i, b: (layer, 0, 0)),
        pl.BlockSpec((None, LANES, 2 * A_WIDTH), lambda i, b: (layer, 0, 0)),
        pl.BlockSpec((None, _LORA_GW, A_WIDTH), lambda i, b: (layer, 0, 0)),
        pl.BlockSpec((sb, A_HEADS, HEAD_DIM, HEAD_DIM), lambda i, b: (i, 0, 0, 0)),
    ]
    out_specs = [
        pl.BlockSpec((sb, tb, A_WIDTH), lambda i, b: (i, b, 0)),
        pl.BlockSpec((sb, A_HEADS, HEAD_DIM, HEAD_DIM), lambda i, b: (i, 0, 0, 0)),
    ]
    return pl.pallas_call(
        functools.partial(_rwkv_kernel, sb=sb, tb=tb, c=c, nb=nb),
        grid=(s // sb, nb),
        in_specs=in_specs,
        out_specs=out_specs,
        out_shape=[jax.ShapeDtypeStruct((s, t, A_WIDTH), BF16),
                   jax.ShapeDtypeStruct((s, A_HEADS, HEAD_DIM, HEAD_DIM), F32)],
        scratch_shapes=[pltpu.VMEM((sb, PAIRS, HEAD_DIM, LANES), F32)],
        compiler_params=_cparams(2),
        name="rwkv",
    )(p3, p3, p3, p3, shiftp, shiftp, vec, mu_l, wl1, wl2, s0)


def _norm_rope(slabs):
    lane = lax.broadcasted_iota(jnp.int32, (1, LANES), 1)
    first = lane < HEAD_DIM
    low_half = (lane % HEAD_DIM) < (HEAD_DIM // 2)
    tiles = [(n, x[:, j * LANES:(j + 1) * LANES]) for n, (x, _, _, _) in enumerate(slabs)
             for j in range(x.shape[1] // LANES)]
    sq = [xs * xs for _, xs in tiles]
    s0 = [jnp.sum(jnp.where(first, s, 0.0), axis=1, keepdims=True) for s in sq]
    s1 = [jnp.sum(jnp.where(first, 0.0, s), axis=1, keepdims=True) for s in sq]
    xn = [xs * lax.rsqrt(jnp.where(first, a, b) * (1.0 / HEAD_DIM) + RMS_EPS) * slabs[n][1]
          for (n, xs), a, b in zip(tiles, s0, s1)]
    up = [pltpu.roll(v, LANES - HEAD_DIM // 2, 1) for v in xn]
    down = [pltpu.roll(v, HEAD_DIM // 2, 1) for v in xn]
    out = [v * slabs[n][2] + jnp.where(low_half, u, w) * slabs[n][3]
           for (n, _), v, u, w in zip(tiles, xn, up, down)]
    res, o = [], 0
    for x, _, _, _ in slabs:
        k = x.shape[1] // LANES
        res.append(jnp.concatenate(out[o:o + k], axis=1) if k > 1 else out[o])
        o += k
    return res


def _sink_softmax_all(scores, sinks):
    ms = [jnp.maximum(jnp.max(s, axis=-1, keepdims=True), k) for s, k in zip(scores, sinks)]
    ps = [jnp.exp(s - m) for s, m in zip(scores, ms)]
    ds = [jnp.sum(p, axis=-1, keepdims=True) + jnp.exp(k - m) for p, k, m in zip(ps, sinks, ms)]
    return [(p / d).astype(BF16) for p, d in zip(ps, ds)]


def _sink_column(sink_ref, heads, rows):
    return jnp.concatenate([jnp.full((rows, 1), sink_ref[h], F32) for h in heads], axis=0)


_QK_SCALE = HEAD_DIM ** -0.5


def _band_kernel(sink_ref, q_ref, kv_ref, kvp_ref, cq_ref, sq_ref, cp_ref, sp_ref, gq_ref, gk_ref,
                 y_ref, ko_ref, *, tq):
    i = pl.program_id(0)
    hd = HEAD_DIM
    band = WINDOW + CHUNK
    kv = kv_ref[...]
    kvp = kvp_ref[...]
    q, k_cur, k_prev = _norm_rope([
        (q_ref[...].astype(F32), gq_ref[...], cq_ref[...], sq_ref[...]),
        (kv[:, 0:B_KV].astype(F32), gk_ref[...], cq_ref[...], sq_ref[...]),
        (kvp[:, 0:B_KV].astype(F32), gk_ref[...], cp_ref[...], sp_ref[...])])
    q = (q * _QK_SCALE).astype(BF16)
    ko_ref[...] = k_cur
    k_all = jnp.concatenate([k_prev.astype(BF16), k_cur.astype(BF16)], axis=0)
    v_all = jnp.concatenate([kvp[:, B_KV:2 * B_KV], kv[:, B_KV:2 * B_KV]], axis=0)
    k_g = [k_all[:, g * hd:(g + 1) * hd] for g in range(B_KV_HEADS)]
    v_g = [v_all[:, g * hd:(g + 1) * hd] for g in range(B_KV_HEADS)]
    col = lax.broadcasted_iota(jnp.int32, (1, band), 1)
    sink_g = [_sink_column(sink_ref, range(g * B_GROUP, (g + 1) * B_GROUP), CHUNK) for g in range(B_KV_HEADS)]

    jobs = [(k0, g) for k0 in range(0, tq, CHUNK) for g in range(B_KV_HEADS)]
    q_rows = [jnp.concatenate([q[k0:k0 + CHUNK, (g * B_GROUP + j) * hd:(g * B_GROUP + j + 1) * hd]
                               for j in range(B_GROUP)], axis=0) for k0, g in jobs]
    scores = [_dot_nt(qr, k_g[g][k0:k0 + band]) for (k0, g), qr in zip(jobs, q_rows)]
    scores = [jnp.where((col >= WINDOW - k0) | (i > 0), s, -1e30) for (k0, g), s in zip(jobs, scores)]
    probs = _sink_softmax_all(scores, [sink_g[g] for k0, g in jobs])
    outs = {job: _dot(p, v_g[job[1]][job[0]:job[0] + band]) for job, p in zip(jobs, probs)}
    for k0 in range(0, tq, CHUNK):
        heads = [outs[(k0, g)][j * CHUNK:(j + 1) * CHUNK] for g in range(B_KV_HEADS) for j in range(B_GROUP)]
        y_ref[k0:k0 + CHUNK, :] = jnp.concatenate(heads, axis=1).astype(y_ref.dtype)


def _attn_band(p2, cos, sin_signed, gq, gk, sinks, layer, *, tq):
    t = p2.shape[0]
    qb, kvb = _COL_Q // B_Q, _COL_KV // _KV_W
    wpb = tq // WINDOW

    def prev_idx(i):
        return jnp.maximum(i * wpb - 1, 0)

    return pl.pallas_call(
        functools.partial(_band_kernel, tq=tq),
        grid=(t // tq,),
        in_specs=[pl.BlockSpec(memory_space=pltpu.SMEM),
                  pl.BlockSpec((tq, B_Q), lambda i: (i, qb)),
                  pl.BlockSpec((tq, _KV_W), lambda i: (i, kvb)),
                  pl.BlockSpec((WINDOW, _KV_W), lambda i: (prev_idx(i), kvb)),
                  pl.BlockSpec((tq, LANES), lambda i: (i, 0)),
                  pl.BlockSpec((tq, LANES), lambda i: (i, 0)),
                  pl.BlockSpec((WINDOW, LANES), lambda i: (prev_idx(i), 0)),
                  pl.BlockSpec((WINDOW, LANES), lambda i: (prev_idx(i), 0)),
                  pl.BlockSpec((None, 1, LANES), lambda i: (layer, 0, 0)),
                  pl.BlockSpec((None, 1, LANES), lambda i: (layer, 0, 0))],
        out_specs=[pl.BlockSpec((tq, B_Q), lambda i: (i, 0)),
                   pl.BlockSpec((tq, B_KV), lambda i: (i, 0))],
        out_shape=[jax.ShapeDtypeStruct((t, B_Q), BF16),
                   jax.ShapeDtypeStruct((t, B_KV), F32)],
        compiler_params=_cparams(1),
        name="attn_band",
    )(sinks[layer], p2, p2, p2, cos, sin_signed, cos, sin_signed, gq, gk)


def _cached_kernel(sink_ref, q_ref, kv_ref, ck_ref, cv_ref, c_ref, s_ref, gq_ref, gk_ref,
                   y_ref, ko_ref, *, sb):
    hd = HEAD_DIM
    tn = q_ref.shape[1]
    jobs = [(s, g) for s in range(sb) for g in range(B_KV_HEADS)]
    q_rows, k_all, v_all = {}, {}, {}
    cos = jnp.concatenate([c_ref[...]] * sb, axis=0)
    sin = jnp.concatenate([s_ref[...]] * sb, axis=0)
    q_cat = jnp.concatenate([q_ref[s] for s in range(sb)], axis=0).astype(F32)
    kv_cat = jnp.concatenate([kv_ref[s] for s in range(sb)], axis=0)
    q_cat, k_cat_new = _norm_rope([(q_cat, gq_ref[...], cos, sin),
                                   (kv_cat[:, 0:B_KV].astype(F32), gk_ref[...], cos, sin)])
    q_cat = (q_cat * _QK_SCALE).astype(BF16)
    for s in range(sb):
        q = q_cat[s * tn:(s + 1) * tn]
        k_new = k_cat_new[s * tn:(s + 1) * tn]
        ko_ref[s] = k_new
        k_cat = jnp.concatenate([ck_ref[s].astype(BF16), k_new.astype(BF16)], axis=0)
        v_cat = jnp.concatenate([cv_ref[s].astype(BF16), kv_cat[s * tn:(s + 1) * tn, B_KV:2 * B_KV]], axis=0)
        for g in range(B_KV_HEADS):
            q_rows[(s, g)] = jnp.concatenate(
                [q[:, (g * B_GROUP + j) * hd:(g * B_GROUP + j + 1) * hd] for j in range(B_GROUP)], axis=0)
            k_all[(s, g)] = k_cat[:, g * hd:(g + 1) * hd]
            v_all[(s, g)] = v_cat[:, g * hd:(g + 1) * hd]
    scores = [_dot_nt(q_rows[j], k_all[j]) for j in jobs]
    sink_g = [_sink_column(sink_ref, range(g * B_GROUP, (g + 1) * B_GROUP), tn) for g in range(B_KV_HEADS)]
    probs = _sink_softmax_all(scores, [sink_g[g] for s, g in jobs])
    outs = {j: _dot(p, v_all[j]) for j, p in zip(jobs, probs)}
    for s in range(sb):
        heads = [outs[(s, g)][j * tn:(j + 1) * tn] for g in range(B_KV_HEADS) for j in range(B_GROUP)]
        y_ref[s] = jnp.concatenate(heads, axis=1).astype(y_ref.dtype)


def _attn_cached(p3, ck, cv, cos, sin_signed, gq, gk, sinks, layer, *, sb):
    s, t, _ = p3.shape
    w = ck.shape[2]
    qb, kvb = _COL_Q // B_Q, _COL_KV // _KV_W
    return pl.pallas_call(
        functools.partial(_cached_kernel, sb=sb),
        grid=(s // sb,),
        in_specs=[pl.BlockSpec(memory_space=pltpu.SMEM),
                  pl.BlockSpec((sb, t, B_Q), lambda i: (i, 0, qb)),
                  pl.BlockSpec((sb, t, _KV_W), lambda i: (i, 0, kvb)),
                  pl.BlockSpec((None, sb, w, B_KV), lambda i: (layer, i, 0, 0)),
                  pl.BlockSpec((None, sb, w, B_KV), lambda i: (layer, i, 0, 0)),
                  pl.BlockSpec((t, LANES), lambda i: (0, 0)),
                  pl.BlockSpec((t, LANES), lambda i: (0, 0)),
                  pl.BlockSpec((None, 1, LANES), lambda i: (layer, 0, 0)),
                  pl.BlockSpec((None, 1, LANES), lambda i: (layer, 0, 0))],
        out_specs=[pl.BlockSpec((sb, t, B_Q), lambda i: (i, 0, 0)),
                   pl.BlockSpec((sb, t, B_KV), lambda i: (i, 0, 0))],
        out_shape=[jax.ShapeDtypeStruct((s, t, B_Q), BF16),
                   jax.ShapeDtypeStruct((s, t, B_KV), F32)],
        compiler_params=_cparams(1),
        name="attn_cached",
    )(sinks[layer], p3, p3, ck, cv, cos, sin_signed, gq, gk)


def _outproj_kernel(x_ref, ya_ref, yb_ref, ga_ref, gb_ref, wa_ref, wb_ref, wo_ref, o_ref):
    ua = jnp.dot(ya_ref[...], wa_ref[...], preferred_element_type=F32)
    ub = jnp.dot(yb_ref[...], wb_ref[...], preferred_element_type=F32)
    merged = (jax.nn.sigmoid(ga_ref[...].astype(F32)) * ua + jax.nn.sigmoid(gb_ref[...].astype(F32)) * ub)
    o_ref[...] = x_ref[...] + jnp.dot(merged.astype(BF16), wo_ref[...], preferred_element_type=F32)


def _outproj(x, ya, yb, p2, wa, wb, wo, layer, *, tm):
    t, d = x.shape
    gab, gbb = _COL_GA // d, _COL_GB // d
    return pl.pallas_call(
        _outproj_kernel,
        grid=(t // tm,),
        in_specs=[pl.BlockSpec((tm, d), lambda i: (i, 0)),
                  pl.BlockSpec((tm, A_WIDTH), lambda i: (i, 0)),
                  pl.BlockSpec((tm, B_Q), lambda i: (i, 0)),
                  pl.BlockSpec((tm, d), lambda i: (i, gab)),
                  pl.BlockSpec((tm, d), lambda i: (i, gbb)),
                  _single_buffered((None, A_WIDTH, d), lambda i: (layer, 0, 0)),
                  _single_buffered((None, B_Q, d), lambda i: (layer, 0, 0)),
                  _single_buffered((None, d, d), lambda i: (layer, 0, 0))],
        out_specs=pl.BlockSpec((tm, d), lambda i: (i, 0)),
        out_shape=jax.ShapeDtypeStruct((t, d), F32),
        compiler_params=_cparams(1),
        name="outproj",
    )(x, ya, yb, p2, p2, wa, wb, wo)


def _a_pieces(a):
    o = 0
    out = []
    for w in (A_WIDTH, DECAY_LORA, A_WIDTH, A_WIDTH, ICLR_LORA, GATE_LORA):
        out.append(a[..., o:o + w])
        o += w
    return out


def _pad_last(w, n):
    return jnp.pad(w, [(0, 0)] * (w.ndim - 1) + [(0, n - w.shape[-1])])


def _regroup_a(a):
    r, w_lo, k, v, a_lo, g_lo = _a_pieces(a)
    lora = _pad_last(jnp.concatenate([w_lo, a_lo, g_lo], axis=-1), _LORA_W)
    return jnp.concatenate([r, k, v], axis=-1), lora


def _w_in_moves():
    moves, src = [], 0
    lora_dst = (_COL_LORA, _COL_LORA + DECAY_LORA, _COL_LORA + DECAY_LORA + ICLR_LORA)
    for width, dst in zip((A_WIDTH, DECAY_LORA, A_WIDTH, A_WIDTH, ICLR_LORA, GATE_LORA),
                          (_COL_R, lora_dst[0], _COL_K, _COL_V, lora_dst[1], lora_dst[2])):
        moves.append((src, width, dst))
        src += width
    for width, dst in ((B_Q, _COL_Q), (B_KV, _COL_KV), (B_KV, _COL_KV + B_KV), (D_MODEL, _COL_GA), (D_MODEL, _COL_GB)):
        moves.append((src, width, dst))
        src += width
    used = DECAY_LORA + ICLR_LORA + GATE_LORA
    return moves, (_COL_LORA + used, _LORA_W - used)


def _regroup_kernel(w_ref, o_ref):
    moves, (pad0, pad_w) = _w_in_moves()
    for src, width, dst in moves:
        o_ref[:, dst:dst + width] = w_ref[:, src:src + width].astype(BF16)
    o_ref[:, pad0:pad0 + pad_w] = jnp.zeros((o_ref.shape[0], pad_w), BF16)


def _regroup_w_in(w, *, tr):
    depth, d, n_in = w.shape
    return pl.pallas_call(
        _regroup_kernel,
        grid=(depth, d // tr),
        in_specs=[pl.BlockSpec((None, tr, n_in), lambda l, i: (l, i, 0))],
        out_specs=pl.BlockSpec((None, tr, NP_COLS), lambda l, i: (l, i, 0)),
        out_shape=jax.ShapeDtypeStruct((depth, d, NP_COLS), BF16),
        compiler_params=_cparams(2),
        name="regroup_w_in",
    )(w)


def _shift_row_to_a(p_last):
    p_last = p_last.astype(F32)
    r = p_last[..., _COL_R:_COL_R + A_WIDTH]
    k = p_last[..., _COL_K:_COL_K + A_WIDTH]
    v = p_last[..., _COL_V:_COL_V + A_WIDTH]
    o = _COL_LORA
    w_lo = p_last[..., o:o + DECAY_LORA]
    a_lo = p_last[..., o + DECAY_LORA:o + DECAY_LORA + ICLR_LORA]
    g_lo = p_last[..., o + DECAY_LORA + ICLR_LORA:o + DECAY_LORA + ICLR_LORA + GATE_LORA]
    return jnp.concatenate([r, w_lo, k, v, a_lo, g_lo], axis=-1)


def _lora_weights(decay_w2, iclr_a2, gate_g2):
    top = jnp.concatenate([decay_w2, jnp.zeros_like(decay_w2)], axis=-1)
    bot = jnp.concatenate([jnp.zeros_like(iclr_a2), iclr_a2], axis=-1)
    wl1 = jnp.concatenate([top, bot], axis=-2)
    wl2 = jnp.pad(gate_g2, ((0, 0), (0, _LORA_GW - GATE_LORA), (0, 0)))
    return wl1.astype(BF16), wl2.astype(BF16)


def _rope_tables(pos):
    half = HEAD_DIM // 2
    inv = ROPE_THETA ** (-jnp.arange(half, dtype=F32) / half)
    ang = pos.astype(F32)[:, None] * inv[None, :]
    cos = jnp.cos(ang)
    sin = jnp.sin(ang)
    cos_t = jnp.tile(cos, (1, LANES // half))
    sin_t = jnp.tile(jnp.concatenate([-sin, sin], axis=1), (1, LANES // HEAD_DIM))
    return cos_t, sin_t


_FF_TILE = 512
_FF_CAST_TILE = 256


def kernel(x_prompt, x_sample, cache_k, cache_v, state_wkv, state_shift, norm_ff1, ff1_gate, ff1_up, ff1_down,
           norm_mix, w_in, shift_mu, decay_w0, decay_w2, iclr_a0, iclr_a2, gate_g2, k_k, k_a, r_k, gn_gain,
           gn_bias, q_norm, k_norm, sinks, w_up_a, w_up_b, w_o, norm_ff2, ff2_gate, ff2_up, ff2_down):
    depth = norm_ff1.shape[0]
    bp, tp, d = x_prompt.shape
    bs, ts, _ = x_sample.shape
    assert bp == 1 and d == D_MODEL

    cast_tail = ff1_gate.shape[2] % _FF_CAST_TILE
    assert cast_tail % LANES == 0 and cast_tail > 0
    g_ff1 = norm_ff1[:, None, :]
    g_ff2 = norm_ff2[:, None, :]
    g_mix = norm_mix[:, None, :]
    w_in_b = _regroup_w_in(w_in, tr=256)
    mu_rkv, mu_l = _regroup_a(shift_mu)
    mu_l = mu_l[:, None, :]
    rows = [mu_rkv[:, 0:A_WIDTH], mu_rkv[:, A_WIDTH:2 * A_WIDTH], mu_rkv[:, 2 * A_WIDTH:], decay_w0, iclr_a0,
            k_k, k_a, r_k.reshape(depth, A_WIDTH), gn_gain, gn_bias]
    vec = jnp.stack(rows + [jnp.zeros((depth, A_WIDTH), F32)] * (_VEC_ROWS - len(rows)), axis=1)
    wl1, wl2 = _lora_weights(decay_w2, iclr_a2, gate_g2)
    gq = jnp.tile(q_norm, (1, LANES // HEAD_DIM))[:, None, :]
    gk = jnp.tile(k_norm, (1, LANES // HEAD_DIM))[:, None, :]
    wa = w_up_a.astype(BF16)
    wb = w_up_b.astype(BF16)
    wo = w_o.astype(BF16)
    s_rkv, s_lora = _regroup_a(state_shift)
    shift_s = jnp.concatenate([s_rkv, s_lora], axis=-1)[:, :, None, :]
    zero_shift = jnp.zeros((bp, 1, _RKV_W + _LORA_W), F32)
    zero_state = jnp.zeros((bp, A_HEADS, HEAD_DIM, HEAD_DIM), F32)
    ck = cache_k.reshape(depth, bs, -1, B_KV)
    cv = cache_v.reshape(depth, bs, -1, B_KV)
    cos_p, sin_p = _rope_tables(jnp.arange(tp))
    cos_s, sin_s = _rope_tables(PAST_LEN + jnp.arange(ts))

    xp = x_prompt.reshape(tp, d)
    xs = x_sample.reshape(bs * ts, d)
    outs = {k: [] for k in ("p_wkv", "p_shift", "p_k", "p_v", "s_wkv", "s_shift", "s_k", "s_v")}
    for l in range(depth):
        xp, xs = _ffn_pair(xp, xs, g_ff1, ff1_gate, ff1_up, ff1_down, l,
                           tm=512, tf=_FF_TILE, tf_cast=_FF_CAST_TILE)

        pp = _inproj(xp, g_mix, w_in_b, l, tm=1024, tn=1024)
        ps = _inproj(xs, g_mix, w_in_b, l, tm=512, tn=1024)
        pp3 = pp.reshape(bp, tp, NP_COLS)
        ps3 = ps.reshape(bs, ts, NP_COLS)

        ya_p, wkv_p = _rwkv(pp3, zero_shift, zero_state, vec, mu_l, wl1, wl2, l, sb=1, tb=4 * CHUNK, c=CHUNK)
        ya_s, wkv_s = _rwkv(ps3, shift_s[l], state_wkv[l], vec, mu_l, wl1, wl2, l, sb=2, tb=ts, c=ts)

        yb_p, kr_p = _attn_band(pp, cos_p, sin_p, gq, gk, sinks, l, tq=2 * WINDOW)
        yb_s, kr_s = _attn_cached(ps3, ck, cv, cos_s, sin_s, gq, gk, sinks, l, sb=8)

        xp = _outproj(xp, ya_p.reshape(tp, A_WIDTH), yb_p, pp, wa, wb, wo, l, tm=256)
        xs = _outproj(xs, ya_s.reshape(bs * ts, A_WIDTH), yb_s.reshape(bs * ts, B_Q), ps, wa, wb, wo, l, tm=256)

        xp, xs = _ffn_pair(xp, xs, g_ff2, ff2_gate, ff2_up, ff2_down, l,
                           tm=512, tf=_FF_TILE, tf_cast=_FF_CAST_TILE)

        vcol = _COL_KV + B_KV
        outs["p_wkv"].append(wkv_p)
        outs["p_shift"].append(_shift_row_to_a(pp3[:, -1, :]))
        outs["p_k"].append(kr_p[-WINDOW:].reshape(bp, WINDOW, B_KV_HEADS, HEAD_DIM))
        outs["p_v"].append(pp3[:, -WINDOW:, vcol:vcol + B_KV].astype(F32).reshape(bp, WINDOW, B_KV_HEADS, HEAD_DIM))
        outs["s_wkv"].append(wkv_s)
        outs["s_shift"].append(_shift_row_to_a(ps3[:, -1, :]))
        outs["s_k"].append(kr_s.reshape(bs, ts, B_KV_HEADS, HEAD_DIM))
        outs["s_v"].append(ps3[:, :, vcol:vcol + B_KV].astype(F32).reshape(bs, ts, B_KV_HEADS, HEAD_DIM))

    return (xp.reshape(bp, tp, d), xs.reshape(bs, ts, d),
            jnp.stack(outs["p_wkv"]), jnp.stack(outs["p_shift"]), jnp.stack(outs["p_k"]), jnp.stack(outs["p_v"]),
            jnp.stack(outs["s_wkv"]), jnp.stack(outs["s_shift"]), jnp.stack(outs["s_k"]), jnp.stack(outs["s_v"]))
```

```python
import functools

import jax
import jax.numpy as jnp
from jax import lax
from jax.experimental import pallas as pl
from jax.experimental.pallas import tpu as pltpu

F32 = jnp.float32
BF16 = jnp.bfloat16

HEAD_DIM = 64
A_WIDTH = 1024
A_HEADS = A_WIDTH // HEAD_DIM
DECAY_LORA = 64
ICLR_LORA = 64
GATE_LORA = 160
GN_EPS = 64e-5
RMS_EPS = 1e-6
B_HEADS = 16
B_KV_HEADS = 4
B_GROUP = B_HEADS // B_KV_HEADS
B_Q = B_HEADS * HEAD_DIM
B_KV = B_KV_HEADS * HEAD_DIM
CHUNK = 64
WINDOW = 128
ROPE_THETA = 10000.0
PAST_LEN = 1024
D_MODEL = 2048
A_COLS = 3 * A_WIDTH + DECAY_LORA + ICLR_LORA + GATE_LORA

LANES = 128
BF16_ROWS = 16
VMEM_LIMIT_BYTES = 56 * 1024 * 1024

_COL_R = 0
_COL_K = _COL_R + A_WIDTH
_COL_V = _COL_K + A_WIDTH
_RKV_W = 3 * A_WIDTH
_COL_Q = _COL_V + A_WIDTH
_COL_GA = _COL_Q + B_Q
_COL_GB = _COL_GA + D_MODEL
_COL_LORA = _COL_GB + D_MODEL
_LORA_W = 512
_COL_KV = _COL_LORA + _LORA_W
_KV_W = 2 * B_KV
NP_COLS = _COL_KV + _KV_W
_LORA_G0 = LANES
_LORA_GW = 2 * LANES
PAIRS = A_HEADS // 2
_VEC_ROWS = 16


def _cparams(n_axes):
    return pltpu.CompilerParams(dimension_semantics=("arbitrary",) * n_axes,
                                vmem_limit_bytes=VMEM_LIMIT_BYTES)


def _dot(a, b):
    return jnp.dot(a.astype(BF16), b.astype(BF16), preferred_element_type=F32)


def _dot_nt(a, b):
    return lax.dot_general(a.astype(BF16), b.astype(BF16), (((1,), (1,)), ((), ())),
                           preferred_element_type=F32)


def _dot_tn(a, b):
    return lax.dot_general(a.astype(BF16), b.astype(BF16), (((0,), (0,)), ((), ())),
                           preferred_element_type=F32)


def _split3(x):
    hi = x.astype(BF16)
    r1 = x - hi.astype(F32)
    mid = r1.astype(BF16)
    lo = (r1 - mid.astype(F32)).astype(BF16)
    return hi, mid, lo


def _rms_rows(x, gain):
    ms = jnp.mean(x * x, axis=-1, keepdims=True)
    return x * lax.rsqrt(ms + RMS_EPS) * gain


def _single_buffered(shape, index_map):
    return pl.BlockSpec(shape, index_map, pipeline_mode=pl.Buffered(1))


def _ffn_init(x_ref, g_ref, o_ref, h_ref):
    x = x_ref[...]
    h_ref[...] = _rms_rows(x, g_ref[...]).astype(BF16)
    o_ref[...] = x


def _swiglu_accumulate(h_ref, o_ref, wg, wu, wd):
    h = h_ref[...]
    gate = jnp.dot(h, wg, preferred_element_type=F32)
    up = jnp.dot(h, wu, preferred_element_type=F32)
    act = (0.5 * gate * jax.nn.sigmoid(gate) * up).astype(BF16)
    o_ref[...] += jnp.dot(act, wd, preferred_element_type=F32)


def _ffn_kernel(x_ref, g_ref, wg_ref, wu_ref, wd_ref, wgt_ref, wut_ref, wdt_ref, o_ref, h_ref, *, n_full):
    f = pl.program_id(1)

    @pl.when(f == 0)
    def _():
        _ffn_init(x_ref, g_ref, o_ref, h_ref)

    @pl.when(f < n_full)
    def _():
        _swiglu_accumulate(h_ref, o_ref, wg_ref[...], wu_ref[...], wd_ref[...])

    @pl.when(f == n_full)
    def _():
        _swiglu_accumulate(h_ref, o_ref, wgt_ref[...], wut_ref[...], wdt_ref[...])


def _ffn(x, gain, w, layer, *, tm, tf):
    wg, wu, wd, wgt, wut, wdt = w
    t, d = x.shape
    n_full = wg.shape[1] // tf
    tail = wgt.shape[1]
    last = n_full - 1
    return pl.pallas_call(
        functools.partial(_ffn_kernel, n_full=n_full),
        grid=(t // tm, n_full + 1),
        in_specs=[pl.BlockSpec((tm, d), lambda i, f: (i, 0)),
                  pl.BlockSpec((None, 1, d), lambda i, f: (layer, 0, 0)),
                  pl.BlockSpec((d, tf), lambda i, f: (0, jnp.minimum(f, last))),
                  pl.BlockSpec((d, tf), lambda i, f: (0, jnp.minimum(f, last))),
                  pl.BlockSpec((tf, d), lambda i, f: (jnp.minimum(f, last), 0)),
                  _single_buffered((d, tail), lambda i, f: (0, 0)),
                  _single_buffered((d, tail), lambda i, f: (0, 0)),
                  _single_buffered((tail, d), lambda i, f: (0, 0))],
        out_specs=pl.BlockSpec((tm, d), lambda i, f: (i, 0)),
        out_shape=jax.ShapeDtypeStruct((t, d), F32),
        scratch_shapes=[pltpu.VMEM((tm, d), BF16)],
        compiler_params=_cparams(2),
        name="ffn",
    )(x, gain, wg, wu, wd, wgt, wut, wdt)


def _ffn_cast_kernel(x_ref, g_ref, wg_ref, wu_ref, wd_ref, wgt_ref, wut_ref, wdt_ref,
                     o_ref, wg16_ref, wu16_ref, wd16_ref, wgt16_ref, wut16_ref, wdt16_ref, h_ref, *, n_full):
    f = pl.program_id(0)

    @pl.when(f == 0)
    def _():
        _ffn_init(x_ref, g_ref, o_ref, h_ref)

    def step(src, dst):
        w = [r[...].astype(BF16) for r in src]
        for r, v in zip(dst, w):
            r[...] = v
        _swiglu_accumulate(h_ref, o_ref, *w)

    @pl.when(f < n_full)
    def _():
        step((wg_ref, wu_ref, wd_ref), (wg16_ref, wu16_ref, wd16_ref))

    @pl.when(f == n_full)
    def _():
        step((wgt_ref, wut_ref, wdt_ref), (wgt16_ref, wut16_ref, wdt16_ref))


def _ffn_cast(x, gain, gate, up, down, layer, *, tf, tail):
    t, d = x.shape
    d_ff = gate.shape[2]
    n_full = (d_ff - tail) // tf
    assert n_full * tf + tail == d_ff and d_ff % tail == 0
    last = n_full - 1
    tail_block = d_ff // tail - 1
    wide = pl.BlockSpec((None, d, tf), lambda f: (layer, 0, jnp.minimum(f, last)))
    tall = pl.BlockSpec((None, tf, d), lambda f: (layer, jnp.minimum(f, last), 0))
    wide16 = pl.BlockSpec((d, tf), lambda f: (0, jnp.minimum(f, last)))
    tall16 = pl.BlockSpec((tf, d), lambda f: (jnp.minimum(f, last), 0))
    return pl.pallas_call(
        functools.partial(_ffn_cast_kernel, n_full=n_full),
        grid=(n_full + 1,),
        in_specs=[_single_buffered((t, d), lambda f: (0, 0)),
                  pl.BlockSpec((None, 1, d), lambda f: (layer, 0, 0)),
                  wide, wide, tall,
                  _single_buffered((None, d, tail), lambda f: (layer, 0, tail_block)),
                  _single_buffered((None, d, tail), lambda f: (layer, 0, tail_block)),
                  _single_buffered((None, tail, d), lambda f: (layer, tail_block, 0))],
        out_specs=[pl.BlockSpec((t, d), lambda f: (0, 0)),
                   wide16, wide16, tall16,
                   pl.BlockSpec((d, tail), lambda f: (0, 0)),
                   pl.BlockSpec((d, tail), lambda f: (0, 0)),
                   pl.BlockSpec((tail, d), lambda f: (0, 0))],
        out_shape=[jax.ShapeDtypeStruct((t, d), F32),
                   jax.ShapeDtypeStruct((d, n_full * tf), BF16),
                   jax.ShapeDtypeStruct((d, n_full * tf), BF16),
                   jax.ShapeDtypeStruct((n_full * tf, d), BF16),
                   jax.ShapeDtypeStruct((d, tail), BF16),
                   jax.ShapeDtypeStruct((d, tail), BF16),
                   jax.ShapeDtypeStruct((tail, d), BF16)],
        scratch_shapes=[pltpu.VMEM((t, d), BF16)],
        compiler_params=_cparams(1),
        name="ffn_cast",
    )(x, gain, gate, up, down, gate, up, down)


def _ffn_pair(xp, xs, gain, gate, up, down, layer, *, tm, tf, tf_cast):
    tail = gate.shape[2] % tf_cast
    xs, wg, wu, wd, wgt, wut, wdt = _ffn_cast(xs, gain, gate, up, down, layer, tf=tf_cast, tail=tail)
    done = (wg.shape[1] // tf) * tf
    w = (wg, wu, wd,
         jnp.concatenate([wg[:, done:], wgt], axis=1), jnp.concatenate([wu[:, done:], wut], axis=1),
         jnp.concatenate([wd[done:, :], wdt], axis=0))
    return _ffn(xp, gain, w, layer, tm=tm, tf=tf), xs


def _inproj_kernel(x_ref, g_ref, w_ref, o_ref, h_ref):
    @pl.when(pl.program_id(1) == 0)
    def _():
        h_ref[...] = _rms_rows(x_ref[...], g_ref[...]).astype(BF16)

    o_ref[...] = jnp.dot(h_ref[...], w_ref[...], preferred_element_type=F32).astype(o_ref.dtype)


def _inproj(x, gain, w, layer, *, tm, tn):
    t, d = x.shape
    n = w.shape[2]
    return pl.pallas_call(
        _inproj_kernel,
        grid=(t // tm, n // tn),
        in_specs=[pl.BlockSpec((tm, d), lambda i, j: (i, 0)),
                  pl.BlockSpec((None, 1, d), lambda i, j: (layer, 0, 0)),
                  pl.BlockSpec((None, d, tn), lambda i, j: (layer, 0, j))],
        out_specs=pl.BlockSpec((tm, tn), lambda i, j: (i, j)),
        out_shape=jax.ShapeDtypeStruct((t, n), BF16),
        scratch_shapes=[pltpu.VMEM((tm, d), BF16)],
        compiler_params=_cparams(2),
        name="inproj",
    )(x, gain, w)


def _seg_sum(x, first):
    s0 = jnp.sum(jnp.where(first, x, 0.0), axis=1, keepdims=True)
    s1 = jnp.sum(jnp.where(first, 0.0, x), axis=1, keepdims=True)
    return jnp.where(first, s0, s1)


def _bd(x):
    w = x.shape[1]
    first = lax.broadcasted_iota(jnp.int32, (1, w), 1) < (w // 2)
    zero = jnp.zeros_like(x)
    return jnp.concatenate([jnp.where(first, x, zero), jnp.where(first, zero, x)], axis=0)


def _diag_blocks(m):
    n = m.shape[0] // 2
    first = lax.broadcasted_iota(jnp.int32, (1, 2 * n), 1) < n
    return jnp.where(first, m[0:n], m[n:2 * n])


def _rwkv_kernel(rkv_ref, l_ref, rkvp_ref, lp_ref, srkv_ref, sl_ref, vec_ref, mul_ref, wl1_ref, wl2_ref,
                 s0_ref, y_ref, sout_ref, z_ref, *, sb, tb, c, nb):
    b = pl.program_id(1)
    nc = tb // c
    hd = HEAD_DIM

    vec = vec_ref[...]
    mu_rkv = jnp.concatenate([vec[0:1], vec[1:2], vec[2:3]], axis=1)
    mu_l = mul_ref[...]

    lane = lax.broadcasted_iota(jnp.int32, (1, LANES), 1)
    first = lane < hd
    row_t = lax.broadcasted_iota(jnp.int32, (tb, 1), 0)
    ri = lax.broadcasted_iota(jnp.int32, (c, 2 * c), 0)
    ci = lax.broadcasted_iota(jnp.int32, (c, 2 * c), 1) % c
    lower_incl = ri >= ci
    lower_strict = ri > ci
    eye_p = jnp.where(ri == ci, 1.0, 0.0).astype(F32)
    rt_ = lax.broadcasted_iota(jnp.int32, (c, c), 0)
    ct_ = lax.broadcasted_iota(jnp.int32, (c, c), 1)
    tri = jnp.where(rt_ >= ct_, 1.0, 0.0).astype(BF16)
    rj = lax.broadcasted_iota(jnp.int32, (hd, LANES), 0)
    cj = lax.broadcasted_iota(jnp.int32, (hd, LANES), 1) % hd
    eye_h = rj == cj

    @pl.when(b == 0)
    def _():
        for s in range(sb):
            for p in range(PAIRS):
                z_ref[s, p] = jnp.concatenate([s0_ref[s, 2 * p].T, s0_ref[s, 2 * p + 1].T], axis=1)

    def shifted(ref, pref, sref, s, mu):
        raw = ref[s].astype(F32)
        before = jnp.where(b == 0, sref[s], pref[s][BF16_ROWS - 1:BF16_ROWS, :].astype(F32))
        prev = jnp.where(row_t == 0, before, pltpu.roll(raw, 1, 0))
        return raw + (prev - raw) * mu

    pre = {}
    for s in range(sb):
        x3 = shifted(rkv_ref, rkvp_ref, srkv_ref, s, mu_rkv)
        lx = shifted(l_ref, lp_ref, sl_ref, s, mu_l)
        l01 = lx[:, 0:LANES]
        z01 = jnp.where(first, jnp.tanh(l01), l01)
        wa = _dot(z01, wl1_ref[...])
        gate = _dot(jax.nn.sigmoid(lx[:, _LORA_G0:_LORA_G0 + _LORA_GW]), wl2_ref[...])
        for p in range(PAIRS):
            ps = slice(p * LANES, (p + 1) * LANES)
            w0, a0, k_k, k_a, r_k = vec[3:4, ps], vec[4:5, ps], vec[5:6, ps], vec[6:7, ps], vec[7:8, ps]
            rx = x3[:, _COL_R + p * LANES:_COL_R + (p + 1) * LANES]
            kx = x3[:, _COL_K + p * LANES:_COL_K + (p + 1) * LANES]
            vx = x3[:, _COL_V + p * LANES:_COL_V + (p + 1) * LANES]
            w_pre = wa[:, ps] + w0
            a_pre = wa[:, A_WIDTH + p * LANES:A_WIDTH + (p + 1) * LANES] + a0
            softplus = jnp.maximum(-w_pre, 0.0) + jnp.log1p(jnp.exp(-jnp.abs(w_pre)))
            logd = -jnp.exp(-softplus - 0.5)
            a = jax.nn.sigmoid(a_pre)
            kxk = kx * k_k
            kk = kxk / jnp.maximum(jnp.sqrt(_seg_sum(kxk * kxk, first)), 1e-12)
            kp = kx * (1.0 + (a - 1.0) * k_a)
            pre[(s, p)] = dict(r=rx, kk=kk, kp=kp, bb=kk * a, v=vx, logd=logd, gate=gate[:, ps],
                               bonus=_seg_sum(rx * kp * r_k, first) * vx)

    units = [(s, ch, p) for ch in range(nc) for s in range(sb) for p in range(PAIRS)]

    def rows(name, u):
        s, ch, p = units[u]
        return pre[(s, p)][name][ch * c:(ch + 1) * c]

    nu = len(units)
    cin = []
    for u in range(nu):
        hi, mid, lo = _split3(rows("logd", u))
        cs = jnp.dot(tri, jnp.concatenate([hi, mid, lo], axis=1), preferred_element_type=F32)
        cin.append(cs[:, 0:LANES] + cs[:, LANES:2 * LANES] + cs[:, 2 * LANES:3 * LANES])
    kk_t, r_t, k_d, b_d, p_end, g = [], [], [], [], [], []
    for u in range(nu):
        ld = rows("logd", u)
        c_end = cin[u][c - 1:c, :]
        e_inv = jnp.exp(-cin[u])
        e_dec = jnp.exp(c_end - cin[u])
        kk_t.append((rows("kk", u) * jnp.exp(cin[u] - ld)).astype(BF16))
        r_t.append(rows("r", u) * jnp.exp(cin[u]))
        k_d.append((rows("kp", u) * e_dec).astype(BF16))
        b_d.append((rows("bb", u) * e_dec).astype(BF16))
        p_end.append(jnp.exp(c_end))
        lhs = jnp.concatenate([kk_t[u], r_t[u].astype(BF16)], axis=0)
        rhs = jnp.concatenate([_bd((rows("bb", u) * e_inv).astype(BF16)),
                               _bd((rows("kp", u) * e_inv).astype(BF16))], axis=0)
        g.append(_dot_nt(lhs, rhs))
    m_ab = [jnp.where(lower_strict, g[u][0:c, 0:2 * c], 0.0).astype(BF16) for u in range(nu)]
    a_rb = [jnp.where(lower_incl, g[u][c:2 * c, 0:2 * c], 0.0).astype(BF16) for u in range(nu)]
    m_ak = [jnp.where(lower_strict, g[u][0:c, 2 * c:4 * c], 0.0).astype(BF16) for u in range(nu)]
    a_rk = [jnp.where(lower_incl, g[u][c:2 * c, 2 * c:4 * c], 0.0).astype(BF16) for u in range(nu)]
    v_bd = [_bd(rows("v", u).astype(BF16)) for u in range(nu)]
    x = [eye_p - m_ab[u].astype(F32) for u in range(nu)]
    pw = [_dot(m_ab[u], _bd(m_ab[u])) for u in range(nu)]
    k = 2
    while 2 * k < c:
        both = [_dot(jnp.concatenate([x[u].astype(BF16), pw[u].astype(BF16)], axis=0), _bd(pw[u].astype(BF16)))
                for u in range(nu)]
        x = [x[u] + both[u][0:c] for u in range(nu)]
        pw = [both[u][c:2 * c] for u in range(nu)]
        k *= 2
    t_inv = [(x[u] + _dot(x[u], _bd(pw[u].astype(BF16)))).astype(BF16) for u in range(nu)]
    mv = [_dot(m_ak[u], v_bd[u]) for u in range(nu)]
    tw = [_dot(t_inv[u], jnp.concatenate([_bd(kk_t[u]), _bd(mv[u].astype(BF16))], axis=1)).astype(BF16)
          for u in range(nu)]
    bt = [_dot_tn(b_d[u], tw[u]) for u in range(nu)]
    kv = [_dot_tn(k_d[u], rows("v", u)) for u in range(nu)]
    ab = [_dot(a_rb[u], jnp.concatenate([_bd(tw[u][:, 0:LANES]), _bd(tw[u][:, LANES:2 * LANES])], axis=1))
          for u in range(nu)]
    av = [_dot(a_rk[u], v_bd[u]) for u in range(nu)]
    lhs_z, u_c, y_i = [], [], []
    for u in range(nu):
        a_c = jnp.where(eye_h, p_end[u], 0.0) - _diag_blocks(bt[u][:, 0:LANES])
        q_e = r_t[u] - ab[u][:, 0:LANES]
        lhs_z.append(jnp.concatenate([a_c, q_e], axis=0).astype(BF16))
        u_c.append(_diag_blocks(kv[u]) - _diag_blocks(bt[u][:, LANES:2 * LANES]))
        y_i.append(av[u] - ab[u][:, LANES:2 * LANES])
    z = {(s, p): z_ref[s, p] for s in range(sb) for p in range(PAIRS)}
    per_level = sb * PAIRS
    for ch in range(nc):
        level = range(ch * per_level, (ch + 1) * per_level)
        both = {u: _dot(lhs_z[u], _bd(z[units[u][0], units[u][2]].astype(BF16))) for u in level}
        ys = {u: both[u][hd:hd + c] + y_i[u] for u in level}
        for u in level:
            z[units[u][0], units[u][2]] = both[u][0:hd] + u_c[u]
        means = {u: _seg_sum(ys[u], first) * (1.0 / hd) for u in level}
        ycs = {u: ys[u] - means[u] for u in level}
        variances = {u: _seg_sum(ycs[u] * ycs[u], first) * (1.0 / hd) for u in level}
        for u in level:
            s, _, p = units[u]
            ps = slice(p * LANES, (p + 1) * LANES)
            yn = ycs[u] * lax.rsqrt(variances[u] + GN_EPS)
            out = (yn * vec[8:9, ps] + vec[9:10, ps] + rows("bonus", u)) * rows("gate", u)
            y_ref[s, ch * c:(ch + 1) * c, ps] = out.astype(y_ref.dtype)
    for s in range(sb):
        for p in range(PAIRS):
            z_ref[s, p] = z[(s, p)]

    @pl.when(b == nb - 1)
    def _():
        for s in range(sb):
            for p in range(PAIRS):
                sout_ref[s, 2 * p] = z[(s, p)][:, 0:hd].T
                sout_ref[s, 2 * p + 1] = z[(s, p)][:, hd:2 * hd].T


def _rwkv(p3, shiftp, s0, vec, mu_l, wl1, wl2, layer, *, sb, tb, c):
    s, t, _ = p3.shape
    lb = _COL_LORA // _LORA_W
    tpb = tb // BF16_ROWS
    nb = t // tb

    def prev_tile(b):
        return jnp.maximum(b * tpb - 1, 0)

    in_specs = [
        pl.BlockSpec((sb, tb, _RKV_W), lambda i, b: (i, b, 0)),
        pl.BlockSpec((sb, tb, _LORA_W), lambda i, b: (i, b, lb)),
        pl.BlockSpec((sb, BF16_ROWS, _RKV_W), lambda i, b: (i, prev_tile(b), 0)),
        pl.BlockSpec((sb, BF16_ROWS, _LORA_W), lambda i, b: (i, prev_tile(b), lb)),
        pl.BlockSpec((sb, 1, _RKV_W), lambda i, b: (i, 0, 0)),
        pl.BlockSpec((sb, 1, _LORA_W), lambda i, b: (i, 0, _RKV_W // _LORA_W)),
        pl.BlockSpec((None, _VEC_ROWS, A_WIDTH), lambda i, b: (layer, 0, 0)),
        pl.BlockSpec((None, 1, _LORA_W), lambda i, b: (layer, 0, 0)),
        pl.BlockSpec((None, LANES, 2 * A_WIDTH), lambda i, b: (layer, 0, 0)),
        pl.BlockSpec((None, _LORA_GW, A_WIDTH), lambda i, b: (layer, 0, 0)),
        pl.BlockSpec((sb, A_HEADS, HEAD_DIM, HEAD_DIM), lambda i, b: (i, 0, 0, 0)),
    ]
    out_specs = [
        pl.BlockSpec((sb, tb, A_WIDTH), lambda i, b: (i, b, 0)),
        pl.BlockSpec((sb, A_HEADS, HEAD_DIM, HEAD_DIM), lambda i, b: (i, 0, 0, 0)),
    ]
    return pl.pallas_call(
        functools.partial(_rwkv_kernel, sb=sb, tb=tb, c=c, nb=nb),
        grid=(s // sb, nb),
        in_specs=in_specs,
        out_specs=out_specs,
        out_shape=[jax.ShapeDtypeStruct((s, t, A_WIDTH), BF16),
                   jax.ShapeDtypeStruct((s, A_HEADS, HEAD_DIM, HEAD_DIM), F32)],
        scratch_shapes=[pltpu.VMEM((sb, PAIRS, HEAD_DIM, LANES), F32)],
        compiler_params=_cparams(2),
        name="rwkv",
    )(p3, p3, p3, p3, shiftp, shiftp, vec, mu_l, wl1, wl2, s0)


def _norm_rope(slabs):
    lane = lax.broadcasted_iota(jnp.int32, (1, LANES), 1)
    first = lane < HEAD_DIM
    low_half = (lane % HEAD_DIM) < (HEAD_DIM // 2)
    tiles = [(n, x[:, j * LANES:(j + 1) * LANES]) for n, (x, _, _, _) in enumerate(slabs)
             for j in range(x.shape[1] // LANES)]
    sq = [xs * xs for _, xs in tiles]
    s0 = [jnp.sum(jnp.where(first, s, 0.0), axis=1, keepdims=True) for s in sq]
    s1 = [jnp.sum(jnp.where(first, 0.0, s), axis=1, keepdims=True) for s in sq]
    xn = [xs * lax.rsqrt(jnp.where(first, a, b) * (1.0 / HEAD_DIM) + RMS_EPS) * slabs[n][1]
          for (n, xs), a, b in zip(tiles, s0, s1)]
    up = [pltpu.roll(v, LANES - HEAD_DIM // 2, 1) for v in xn]
    down = [pltpu.roll(v, HEAD_DIM // 2, 1) for v in xn]
    out = [v * slabs[n][2] + jnp.where(low_half, u, w) * slabs[n][3]
           for (n, _), v, u, w in zip(tiles, xn, up, down)]
    res, o = [], 0
    for x, _, _, _ in slabs:
        k = x.shape[1] // LANES
        res.append(jnp.concatenate(out[o:o + k], axis=1) if k > 1 else out[o])
        o += k
    return res


def _sink_softmax_all(scores, sinks):
    ms = [jnp.maximum(jnp.max(s, axis=-1, keepdims=True), k) for s, k in zip(scores, sinks)]
    ps = [jnp.exp(s - m) for s, m in zip(scores, ms)]
    ds = [jnp.sum(p, axis=-1, keepdims=True) + jnp.exp(k - m) for p, k, m in zip(ps, sinks, ms)]
    return [p.astype(BF16) for p in ps], [1.0 / d for d in ds]


def _sink_column(sink_ref, heads, rows):
    return jnp.concatenate([jnp.full((rows, 1), sink_ref[h], F32) for h in heads], axis=0)


_QK_SCALE = HEAD_DIM ** -0.5


def _band_kernel(sink_ref, q_ref, kv_ref, kvp_ref, cq_ref, sq_ref, cp_ref, sp_ref, gq_ref, gk_ref,
                 y_ref, ko_ref, *, tq):
    i = pl.program_id(0)
    hd = HEAD_DIM
    band = WINDOW + CHUNK
    kv = kv_ref[...]
    kvp = kvp_ref[...]
    q, k_cur, k_prev = _norm_rope([
        (q_ref[...].astype(F32), gq_ref[...], cq_ref[...], sq_ref[...]),
        (kv[:, 0:B_KV].astype(F32), gk_ref[...], cq_ref[...], sq_ref[...]),
        (kvp[:, 0:B_KV].astype(F32), gk_ref[...], cp_ref[...], sp_ref[...])])
    q = (q * _QK_SCALE).astype(BF16)
    ko_ref[...] = k_cur
    k_all = jnp.concatenate([k_prev.astype(BF16), k_cur.astype(BF16)], axis=0)
    v_all = jnp.concatenate([kvp[:, B_KV:2 * B_KV], kv[:, B_KV:2 * B_KV]], axis=0)
    k_g = [k_all[:, g * hd:(g + 1) * hd] for g in range(B_KV_HEADS)]
    v_g = [v_all[:, g * hd:(g + 1) * hd] for g in range(B_KV_HEADS)]
    col = lax.broadcasted_iota(jnp.int32, (1, band), 1)
    sink_g = [_sink_column(sink_ref, range(g * B_GROUP, (g + 1) * B_GROUP), CHUNK) for g in range(B_KV_HEADS)]

    jobs = [(k0, g) for k0 in range(0, tq, CHUNK) for g in range(B_KV_HEADS)]
    q_rows = [jnp.concatenate([q[k0:k0 + CHUNK, (g * B_GROUP + j) * hd:(g * B_GROUP + j + 1) * hd]
                               for j in range(B_GROUP)], axis=0) for k0, g in jobs]
    scores = [_dot_nt(qr, k_g[g][k0:k0 + band]) for (k0, g), qr in zip(jobs, q_rows)]
    scores = [jnp.where((col >= WINDOW - k0) | (i > 0), s, -1e30) if k0 < WINDOW else s
              for (k0, g), s in zip(jobs, scores)]
    probs, inv = _sink_softmax_all(scores, [sink_g[g] for k0, g in jobs])
    outs = {job: _dot(p, v_g[job[1]][job[0]:job[0] + band]) * r for job, p, r in zip(jobs, probs, inv)}
    for k0 in range(0, tq, CHUNK):
        heads = [outs[(k0, g)][j * CHUNK:(j + 1) * CHUNK] for g in range(B_KV_HEADS) for j in range(B_GROUP)]
        y_ref[k0:k0 + CHUNK, :] = jnp.concatenate(heads, axis=1).astype(y_ref.dtype)


def _attn_band(p2, cos, sin_signed, gq, gk, sinks, layer, *, tq):
    t = p2.shape[0]
    qb, kvb = _COL_Q // B_Q, _COL_KV // _KV_W
    wpb = tq // WINDOW

    def prev_idx(i):
        return jnp.maximum(i * wpb - 1, 0)

    return pl.pallas_call(
        functools.partial(_band_kernel, tq=tq),
        grid=(t // tq,),
        in_specs=[pl.BlockSpec(memory_space=pltpu.SMEM),
                  pl.BlockSpec((tq, B_Q), lambda i: (i, qb)),
                  pl.BlockSpec((tq, _KV_W), lambda i: (i, kvb)),
                  pl.BlockSpec((WINDOW, _KV_W), lambda i: (prev_idx(i), kvb)),
                  pl.BlockSpec((tq, LANES), lambda i: (i, 0)),
                  pl.BlockSpec((tq, LANES), lambda i: (i, 0)),
                  pl.BlockSpec((WINDOW, LANES), lambda i: (prev_idx(i), 0)),
                  pl.BlockSpec((WINDOW, LANES), lambda i: (prev_idx(i), 0)),
                  pl.BlockSpec((None, 1, LANES), lambda i: (layer, 0, 0)),
                  pl.BlockSpec((None, 1, LANES), lambda i: (layer, 0, 0))],
        out_specs=[pl.BlockSpec((tq, B_Q), lambda i: (i, 0)),
                   pl.BlockSpec((tq, B_KV), lambda i: (i, 0))],
        out_shape=[jax.ShapeDtypeStruct((t, B_Q), BF16),
                   jax.ShapeDtypeStruct((t, B_KV), F32)],
        compiler_params=_cparams(1),
        name="attn_band",
    )(sinks[layer], p2, p2, p2, cos, sin_signed, cos, sin_signed, gq, gk)


def _cached_kernel(sink_ref, q_ref, kv_ref, ck_ref, cv_ref, c_ref, s_ref, gq_ref, gk_ref,
                   y_ref, ko_ref, *, sb):
    hd = HEAD_DIM
    tn = q_ref.shape[1]
    jobs = [(s, g) for s in range(sb) for g in range(B_KV_HEADS)]
    q_rows, k_all, v_all = {}, {}, {}
    cos = jnp.concatenate([c_ref[...]] * sb, axis=0)
    sin = jnp.concatenate([s_ref[...]] * sb, axis=0)
    q_cat = jnp.concatenate([q_ref[s] for s in range(sb)], axis=0).astype(F32)
    kv_cat = jnp.concatenate([kv_ref[s] for s in range(sb)], axis=0)
    q_cat, k_cat_new = _norm_rope([(q_cat, gq_ref[...], cos, sin),
                                   (kv_cat[:, 0:B_KV].astype(F32), gk_ref[...], cos, sin)])
    q_cat = (q_cat * _QK_SCALE).astype(BF16)
    for s in range(sb):
        q = q_cat[s * tn:(s + 1) * tn]
        k_new = k_cat_new[s * tn:(s + 1) * tn]
        ko_ref[s] = k_new
        k_cat = jnp.concatenate([ck_ref[s].astype(BF16), k_new.astype(BF16)], axis=0)
        v_cat = jnp.concatenate([cv_ref[s].astype(BF16), kv_cat[s * tn:(s + 1) * tn, B_KV:2 * B_KV]], axis=0)
        for g in range(B_KV_HEADS):
            q_rows[(s, g)] = jnp.concatenate(
                [q[:, (g * B_GROUP + j) * hd:(g * B_GROUP + j + 1) * hd] for j in range(B_GROUP)], axis=0)
            k_all[(s, g)] = k_cat[:, g * hd:(g + 1) * hd]
            v_all[(s, g)] = v_cat[:, g * hd:(g + 1) * hd]
    scores = [_dot_nt(q_rows[j], k_all[j]) for j in jobs]
    sink_g = [_sink_column(sink_ref, range(g * B_GROUP, (g + 1) * B_GROUP), tn) for g in range(B_KV_HEADS)]
    probs, inv = _sink_softmax_all(scores, [sink_g[g] for s, g in jobs])
    outs = {j: _dot(p, v_all[j]) * r for j, p, r in zip(jobs, probs, inv)}
    for s in range(sb):
        heads = [outs[(s, g)][j * tn:(j + 1) * tn] for g in range(B_KV_HEADS) for j in range(B_GROUP)]
        y_ref[s] = jnp.concatenate(heads, axis=1).astype(y_ref.dtype)


def _attn_cached(p3, ck, cv, cos, sin_signed, gq, gk, sinks, layer, *, sb):
    s, t, _ = p3.shape
    w = ck.shape[2]
    qb, kvb = _COL_Q // B_Q, _COL_KV // _KV_W
    return pl.pallas_call(
        functools.partial(_cached_kernel, sb=sb),
        grid=(s // sb,),
        in_specs=[pl.BlockSpec(memory_space=pltpu.SMEM),
                  pl.BlockSpec((sb, t, B_Q), lambda i: (i, 0, qb)),
                  pl.BlockSpec((sb, t, _KV_W), lambda i: (i, 0, kvb)),
                  pl.BlockSpec((None, sb, w, B_KV), lambda i: (layer, i, 0, 0)),
                  pl.BlockSpec((None, sb, w, B_KV), lambda i: (layer, i, 0, 0)),
                  pl.BlockSpec((t, LANES), lambda i: (0, 0)),
                  pl.BlockSpec((t, LANES), lambda i: (0, 0)),
                  pl.BlockSpec((None, 1, LANES), lambda i: (layer, 0, 0)),
                  pl.BlockSpec((None, 1, LANES), lambda i: (layer, 0, 0))],
        out_specs=[pl.BlockSpec((sb, t, B_Q), lambda i: (i, 0, 0)),
                   pl.BlockSpec((sb, t, B_KV), lambda i: (i, 0, 0))],
        out_shape=[jax.ShapeDtypeStruct((s, t, B_Q), BF16),
                   jax.ShapeDtypeStruct((s, t, B_KV), F32)],
        compiler_params=_cparams(1),
        name="attn_cached",
    )(sinks[layer], p3, p3, ck, cv, cos, sin_signed, gq, gk)


def _outproj_kernel(x_ref, ya_ref, yb_ref, ga_ref, gb_ref, wa_ref, wb_ref, wo_ref, o_ref):
    ua = jnp.dot(ya_ref[...], wa_ref[...], preferred_element_type=F32)
    ub = jnp.dot(yb_ref[...], wb_ref[...], preferred_element_type=F32)
    merged = (jax.nn.sigmoid(ga_ref[...].astype(F32)) * ua + jax.nn.sigmoid(gb_ref[...].astype(F32)) * ub)
    o_ref[...] = x_ref[...] + jnp.dot(merged.astype(BF16), wo_ref[...], preferred_element_type=F32)


def _outproj(x, ya, yb, p2, wa, wb, wo, layer, *, tm):
    t, d = x.shape
    gab, gbb = _COL_GA // d, _COL_GB // d
    return pl.pallas_call(
        _outproj_kernel,
        grid=(t // tm,),
        in_specs=[pl.BlockSpec((tm, d), lambda i: (i, 0)),
                  pl.BlockSpec((tm, A_WIDTH), lambda i: (i, 0)),
                  pl.BlockSpec((tm, B_Q), lambda i: (i, 0)),
                  pl.BlockSpec((tm, d), lambda i: (i, gab)),
                  pl.BlockSpec((tm, d), lambda i: (i, gbb)),
                  _single_buffered((None, A_WIDTH, d), lambda i: (layer, 0, 0)),
                  _single_buffered((None, B_Q, d), lambda i: (layer, 0, 0)),
                  _single_buffered((None, d, d), lambda i: (layer, 0, 0))],
        out_specs=pl.BlockSpec((tm, d), lambda i: (i, 0)),
        out_shape=jax.ShapeDtypeStruct((t, d), F32),
        compiler_params=_cparams(1),
        name="outproj",
    )(x, ya, yb, p2, p2, wa, wb, wo)


def _a_pieces(a):
    o = 0
    out = []
    for w in (A_WIDTH, DECAY_LORA, A_WIDTH, A_WIDTH, ICLR_LORA, GATE_LORA):
        out.append(a[..., o:o + w])
        o += w
    return out


def _pad_last(w, n):
    return jnp.pad(w, [(0, 0)] * (w.ndim - 1) + [(0, n - w.shape[-1])])


def _regroup_a(a):
    r, w_lo, k, v, a_lo, g_lo = _a_pieces(a)
    lora = _pad_last(jnp.concatenate([w_lo, a_lo, g_lo], axis=-1), _LORA_W)
    return jnp.concatenate([r, k, v], axis=-1), lora


def _regroup_w_in(w):
    o = A_COLS
    wq = w[..., o:o + B_Q]
    wk = w[..., o + B_Q:o + B_Q + B_KV]
    wv = w[..., o + B_Q + B_KV:o + B_Q + 2 * B_KV]
    o += B_Q + 2 * B_KV
    wga = w[..., o:o + D_MODEL]
    wgb = w[..., o + D_MODEL:o + 2 * D_MODEL]
    rkv, lora = _regroup_a(w[..., :A_COLS])
    return jnp.concatenate([rkv, wq, wga, wgb, lora, wk, wv], axis=-1).astype(BF16)


def _shift_row_to_a(p_last):
    p_last = p_last.astype(F32)
    r = p_last[..., _COL_R:_COL_R + A_WIDTH]
    k = p_last[..., _COL_K:_COL_K + A_WIDTH]
    v = p_last[..., _COL_V:_COL_V + A_WIDTH]
    o = _COL_LORA
    w_lo = p_last[..., o:o + DECAY_LORA]
    a_lo = p_last[..., o + DECAY_LORA:o + DECAY_LORA + ICLR_LORA]
    g_lo = p_last[..., o + DECAY_LORA + ICLR_LORA:o + DECAY_LORA + ICLR_LORA + GATE_LORA]
    return jnp.concatenate([r, w_lo, k, v, a_lo, g_lo], axis=-1)


def _lora_weights(decay_w2, iclr_a2, gate_g2):
    top = jnp.concatenate([decay_w2, jnp.zeros_like(decay_w2)], axis=-1)
    bot = jnp.concatenate([jnp.zeros_like(iclr_a2), iclr_a2], axis=-1)
    wl1 = jnp.concatenate([top, bot], axis=-2)
    wl2 = jnp.pad(gate_g2, ((0, 0), (0, _LORA_GW - GATE_LORA), (0, 0)))
    return wl1.astype(BF16), wl2.astype(BF16)


def _rope_tables(pos):
    half = HEAD_DIM // 2
    inv = ROPE_THETA ** (-jnp.arange(half, dtype=F32) / half)
    ang = pos.astype(F32)[:, None] * inv[None, :]
    cos = jnp.cos(ang)
    sin = jnp.sin(ang)
    cos_t = jnp.tile(cos, (1, LANES // half))
    sin_t = jnp.tile(jnp.concatenate([-sin, sin], axis=1), (1, LANES // HEAD_DIM))
    return cos_t, sin_t


_FF_TILE = 512
_FF_CAST_TILE = 256


def kernel(x_prompt, x_sample, cache_k, cache_v, state_wkv, state_shift, norm_ff1, ff1_gate, ff1_up, ff1_down,
           norm_mix, w_in, shift_mu, decay_w0, decay_w2, iclr_a0, iclr_a2, gate_g2, k_k, k_a, r_k, gn_gain,
           gn_bias, q_norm, k_norm, sinks, w_up_a, w_up_b, w_o, norm_ff2, ff2_gate, ff2_up, ff2_down):
    depth = norm_ff1.shape[0]
    bp, tp, d = x_prompt.shape
    bs, ts, _ = x_sample.shape
    assert bp == 1 and d == D_MODEL

    cast_tail = ff1_gate.shape[2] % _FF_CAST_TILE
    assert cast_tail % LANES == 0 and cast_tail > 0
    g_ff1 = norm_ff1[:, None, :]
    g_ff2 = norm_ff2[:, None, :]
    g_mix = norm_mix[:, None, :]
    w_in_b = _regroup_w_in(w_in)
    mu_rkv, mu_l = _regroup_a(shift_mu)
    mu_l = mu_l[:, None, :]
    rows = [mu_rkv[:, 0:A_WIDTH], mu_rkv[:, A_WIDTH:2 * A_WIDTH], mu_rkv[:, 2 * A_WIDTH:], decay_w0, iclr_a0,
            k_k, k_a, r_k.reshape(depth, A_WIDTH), gn_gain, gn_bias]
    vec = jnp.stack(rows + [jnp.zeros((depth, A_WIDTH), F32)] * (_VEC_ROWS - len(rows)), axis=1)
    wl1, wl2 = _lora_weights(decay_w2, iclr_a2, gate_g2)
    gq = jnp.tile(q_norm, (1, LANES // HEAD_DIM))[:, None, :]
    gk = jnp.tile(k_norm, (1, LANES // HEAD_DIM))[:, None, :]
    wa = w_up_a.astype(BF16)
    wb = w_up_b.astype(BF16)
    wo = w_o.astype(BF16)
    s_rkv, s_lora = _regroup_a(state_shift)
    shift_s = jnp.concatenate([s_rkv, s_lora], axis=-1)[:, :, None, :]
    zero_shift = jnp.zeros((bp, 1, _RKV_W + _LORA_W), F32)
    zero_state = jnp.zeros((bp, A_HEADS, HEAD_DIM, HEAD_DIM), F32)
    ck = cache_k.reshape(depth, bs, -1, B_KV)
    cv = cache_v.reshape(depth, bs, -1, B_KV)
    cos_p, sin_p = _rope_tables(jnp.arange(tp))
    cos_s, sin_s = _rope_tables(PAST_LEN + jnp.arange(ts))

    xp = x_prompt.reshape(tp, d)
    xs = x_sample.reshape(bs * ts, d)
    outs = {k: [] for k in ("p_wkv", "p_shift", "p_k", "p_v", "s_wkv", "s_shift", "s_k", "s_v")}
    for l in range(depth):
        xp, xs = _ffn_pair(xp, xs, g_ff1, ff1_gate, ff1_up, ff1_down, l,
                           tm=512, tf=_FF_TILE, tf_cast=_FF_CAST_TILE)

        pp = _inproj(xp, g_mix, w_in_b, l, tm=1024, tn=1024)
        ps = _inproj(xs, g_mix, w_in_b, l, tm=512, tn=1024)
        pp3 = pp.reshape(bp, tp, NP_COLS)
        ps3 = ps.reshape(bs, ts, NP_COLS)

        ya_p, wkv_p = _rwkv(pp3, zero_shift, zero_state, vec, mu_l, wl1, wl2, l, sb=1, tb=4 * CHUNK, c=CHUNK)
        ya_s, wkv_s = _rwkv(ps3, shift_s[l], state_wkv[l], vec, mu_l, wl1, wl2, l, sb=4, tb=ts, c=ts)

        yb_p, kr_p = _attn_band(pp, cos_p, sin_p, gq, gk, sinks, l, tq=2 * WINDOW)
        yb_s, kr_s = _attn_cached(ps3, ck, cv, cos_s, sin_s, gq, gk, sinks, l, sb=8)

        xp = _outproj(xp, ya_p.reshape(tp, A_WIDTH), yb_p, pp, wa, wb, wo, l, tm=256)
        xs = _outproj(xs, ya_s.reshape(bs * ts, A_WIDTH), yb_s.reshape(bs * ts, B_Q), ps, wa, wb, wo, l, tm=256)

        xp, xs = _ffn_pair(xp, xs, g_ff2, ff2_gate, ff2_up, ff2_down, l,
                           tm=512, tf=_FF_TILE, tf_cast=_FF_CAST_TILE)

        vcol = _COL_KV + B_KV
        outs["p_wkv"].append(wkv_p)
        outs["p_shift"].append(_shift_row_to_a(pp3[:, -1, :]))
        outs["p_k"].append(kr_p[-WINDOW:].reshape(bp, WINDOW, B_KV_HEADS, HEAD_DIM))
        outs["p_v"].append(pp3[:, -WINDOW:, vcol:vcol + B_KV].astype(F32).reshape(bp, WINDOW, B_KV_HEADS, HEAD_DIM))
        outs["s_wkv"].append(wkv_s)
        outs["s_shift"].append(_shift_row_to_a(ps3[:, -1, :]))
        outs["s_k"].append(kr_s.reshape(bs, ts, B_KV_HEADS, HEAD_DIM))
        outs["s_v"].append(ps3[:, :, vcol:vcol + B_KV].astype(F32).reshape(bs, ts, B_KV_HEADS, HEAD_DIM))

    return (xp.reshape(bp, tp, d), xs.reshape(bs, ts, d),
            jnp.stack(outs["p_wkv"]), jnp.stack(outs["p_shift"]), jnp.stack(outs["p_k"]), jnp.stack(outs["p_v"]),
            jnp.stack(outs["s_wkv"]), jnp.stack(outs["s_shift"]), jnp.stack(outs["s_k"]), jnp.stack(outs["s_v"]))
```
